```python
import math
import jax, jax.numpy as jnp
from jax import lax
import numpy as np

D_MODEL = 2048
BATCH = 4
SEQ = 2048
DEPTH = 1
DEC_BATCH = 128
DEC_SEQ = 8
PAST_LEN = 16384
PAGE_SIZE = 128

N_HEADS = 16
HEAD_K = 128
HEAD_V = 128
KEY_DIM = N_HEADS * HEAD_K
VAL_DIM = N_HEADS * HEAD_V
QKV_DIM = 2 * KEY_DIM + VAL_DIM
SHORT_CONV = 4
CHUNK = 64
CONV_CH = D_MODEL
CONV_WIDTH = 31
D_FF = 5632
PLE_DIM = 256
EPS = 1e-6

O_Z = QKV_DIM
O_BETA = O_Z + VAL_DIM
O_A = O_BETA + N_HEADS
O_GLU = O_A + N_HEADS
O_GATE = O_GLU + 2 * CONV_CH
IN_DIM = O_GATE + 2 * D_MODEL

kernel_name = 'hybrid_gdn_conformer_decoder_step'


def rmsnorm(x, g):
    xf = x.astype(jnp.float32)
    y = xf * lax.rsqrt(jnp.mean(xf * xf, axis=-1, keepdims=True) + EPS)
    return (y * g.astype(jnp.float32)).astype(x.dtype)


def layernorm(x, g, b):
    xf = x.astype(jnp.float32)
    mu = jnp.mean(xf, axis=-1, keepdims=True)
    xc = xf - mu
    y = xc * lax.rsqrt(jnp.mean(xc * xc, axis=-1, keepdims=True) + EPS)
    return (y * g.astype(jnp.float32) + b.astype(jnp.float32)).astype(x.dtype)


def l2norm(x):
    return x * lax.rsqrt(jnp.sum(x * x, axis=-1, keepdims=True) + EPS)


def swiglu(x, w_gu, w_down):
    gu = x @ w_gu
    return (jax.nn.silu(gu[..., :D_FF]) * gu[..., D_FF:]) @ w_down


def causal_dwconv(x_ext, w):
    c = w.shape[-1]
    return lax.conv_general_dilated(
        x_ext, w[:, None, :].astype(x_ext.dtype), window_strides=(1,), padding='VALID',
        dimension_numbers=('NWC', 'WIO', 'NWC'), feature_group_count=c)


def gated_delta_rule(q, k, v, beta, g, s0):
    b, l, h, _ = q.shape
    dv = v.shape[-1]
    n_chunks = -(-l // CHUNK)
    pad = n_chunks * CHUNK - l

    def prep(t):
        t = jnp.pad(t, [(0, 0), (0, pad)] + [(0, 0)] * (t.ndim - 2))
        t = t.reshape((b, n_chunks, CHUNK) + t.shape[2:])
        return jnp.moveaxis(t, 3, 2)

    qc, kc, vc, bc, gc = prep(q), prep(k), prep(v), prep(beta), prep(g)
    gcum = jnp.cumsum(gc, axis=-1)
    idx = jnp.arange(CHUNK)
    causal = idx[:, None] >= idx[None, :]
    strict = idx[:, None] > idx[None, :]
    decay = jnp.exp(jnp.where(causal, gcum[..., :, None] - gcum[..., None, :], -jnp.inf))
    kk = jnp.einsum('bnhtd,bnhsd->bnhts', kc, kc)
    m = jnp.where(strict, bc[..., :, None] * decay * kk, 0.0)
    eye = jnp.eye(CHUNK, dtype=jnp.float32)
    gamma = jnp.exp(gcum)
    rhs = jnp.concatenate([bc[..., None] * vc, (bc * gamma)[..., None] * kc], axis=-1)
    sol = lax.linalg.triangular_solve(eye + m, rhs, left_side=True, lower=True,
                                      unit_diagonal=True)
    uv, wk = sol[..., :dv], sol[..., dv:]
    qk = jnp.einsum('bnhtd,bnhsd->bnhts', qc, kc) * decay
    qg = qc * gamma[..., None]
    kdec = kc * jnp.exp(gcum[..., -1:] - gcum)[..., None]
    g_last = gamma[..., -1]
    xs = tuple(jnp.moveaxis(t, 1, 0) for t in (uv, wk, qk, qg, kdec, g_last))

    def step(s, inp):
        uv_c, wk_c, qk_c, qg_c, kdec_c, gl_c = inp
        u = uv_c - jnp.einsum('bhtk,bhkv->bhtv', wk_c, s)
        o = jnp.einsum('bhtk,bhkv->bhtv', qg_c, s) + jnp.einsum('bhts,bhsv->bhtv', qk_c, u)
        s = gl_c[..., None, None] * s + jnp.einsum('bhtk,bhtv->bhkv', kdec_c, u)
        return s, o

    s_final, o = lax.scan(step, s0, xs)
    o = jnp.transpose(o, (1, 0, 3, 2, 4)).reshape(b, n_chunks * CHUNK, h, dv)[:, :l]
    return o, s_final


def token_mixer(u, s0, qkv_buf, glu_buf, w_in, w_short_conv, a_log, dt_bias, o_norm,
                w_dw_conv, b_dw_conv, ln_g, ln_b, w_branch_a, w_branch_b, w_out):
    b, l, _ = u.shape
    dt = u.dtype
    proj = u @ w_in
    qkv_raw, z, b_raw, a_raw, glu_in, gate_raw = jnp.split(
        proj, [O_Z, O_BETA, O_A, O_GLU, O_GATE], axis=-1)
    qkv_ext = jnp.concatenate([qkv_buf.astype(dt), qkv_raw], axis=1)
    qkv = jax.nn.silu(causal_dwconv(qkv_ext, w_short_conv)).astype(jnp.float32)
    q, k, v = jnp.split(qkv, [KEY_DIM, 2 * KEY_DIM], axis=-1)
    q = l2norm(q.reshape(b, l, N_HEADS, HEAD_K)) * (HEAD_K ** -0.5)
    k = l2norm(k.reshape(b, l, N_HEADS, HEAD_K))
    v = v.reshape(b, l, N_HEADS, HEAD_V)
    beta = jax.nn.sigmoid(b_raw.astype(jnp.float32))
    g = -jnp.exp(a_log.astype(jnp.float32)) * jax.nn.softplus(
        a_raw.astype(jnp.float32) + dt_bias.astype(jnp.float32))
    o, s_new = gated_delta_rule(q, k, v, beta, g, s0.astype(jnp.float32))
    o = rmsnorm(o, o_norm) * jax.nn.silu(z.astype(jnp.float32).reshape(b, l, N_HEADS, HEAD_V))
    branch_a = o.reshape(b, l, VAL_DIM).astype(dt) @ w_branch_a
    glu = glu_in[..., :CONV_CH] * jax.nn.sigmoid(glu_in[..., CONV_CH:])
    glu_ext = jnp.concatenate([glu_buf.astype(dt), glu], axis=1)
    c = causal_dwconv(glu_ext, w_dw_conv) + b_dw_conv
    c = jax.nn.silu(layernorm(c, ln_g, ln_b))
    branch_b = c @ w_branch_b
    gate_a, gate_b = jnp.split(gate_raw, 2, axis=-1)
    merged = jax.nn.sigmoid(gate_a) * branch_a + jax.nn.sigmoid(gate_b) * branch_b
    return (merged @ w_out, s_new.astype(dt),
            qkv_ext[:, -(SHORT_CONV - 1):], glu_ext[:, -(CONV_WIDTH - 1):])


def _layer(x, p, s0, qkv_buf, glu_buf, w):
    (ffn1_pre, ffn1_w_gu, ffn1_w_down, ffn1_post, mix_pre, w_in, w_short_conv, a_log, dt_bias,
     o_norm, w_dw_conv, b_dw_conv, ln_g, ln_b, w_branch_a, w_branch_b, w_out, mix_post,
     ffn2_pre, ffn2_w_gu, ffn2_w_down, ffn2_post, ple_pre, w_ple_gate, w_ple_proj,
     ple_post) = w
    h = x + 0.5 * rmsnorm(swiglu(rmsnorm(x, ffn1_pre), ffn1_w_gu, ffn1_w_down), ffn1_post)
    mix, s_new, qkv_new, glu_new = token_mixer(
        rmsnorm(h, mix_pre), s0, qkv_buf, glu_buf, w_in, w_short_conv, a_log, dt_bias, o_norm,
        w_dw_conv, b_dw_conv, ln_g, ln_b, w_branch_a, w_branch_b, w_out)
    h = h + rmsnorm(mix, mix_post)
    h = h + 0.5 * rmsnorm(swiglu(rmsnorm(h, ffn2_pre), ffn2_w_gu, ffn2_w_down), ffn2_post)
    gate = jax.nn.sigmoid(rmsnorm(h, ple_pre) @ w_ple_gate)
    h = h + rmsnorm(gate * (p.astype(h.dtype) @ w_ple_proj), ple_post)
    return h, s_new, qkv_new, glu_new


def setup_inputs(seed: int = 0) -> dict:
    key = jax.random.key(seed)
    ks = iter(list(jax.random.split(key, 48)))

    def nrm(shape, scale):
        return scale * jax.random.normal(next(ks), shape, jnp.float32)

    def gain(n):
        return 1.0 + nrm((DEPTH, n), 0.02)

    def ffn_w():
        return (nrm((DEPTH, D_MODEL, 2 * D_FF), D_MODEL ** -0.5),
                nrm((DEPTH, D_FF, D_MODEL), D_FF ** -0.5))

    inp = {}
    inp['x_prompt'] = nrm((BATCH, SEQ, D_MODEL), 1.0)
    inp['x_sample'] = nrm((DEC_BATCH, DEC_SEQ, D_MODEL), 1.0)
    inp['p_prompt'] = nrm((DEPTH, BATCH, SEQ, PLE_DIM), 1.0)
    inp['p_sample'] = nrm((DEPTH, DEC_BATCH, DEC_SEQ, PLE_DIM), 1.0)
    inp['state_delta'] = nrm((DEPTH, DEC_BATCH, N_HEADS, HEAD_K, HEAD_V), 0.1)
    inp['state_qkv_conv'] = nrm((DEPTH, DEC_BATCH, SHORT_CONV - 1, QKV_DIM), 1.0)
    inp['state_glu_conv'] = nrm((DEPTH, DEC_BATCH, CONV_WIDTH - 1, CONV_CH), 0.5)
    inp['ffn1_pre'] = gain(D_MODEL)
    inp['ffn1_w_gu'], inp['ffn1_w_down'] = ffn_w()
    inp['ffn1_post'] = gain(D_MODEL)
    inp['mix_pre'] = gain(D_MODEL)
    inp['w_in'] = nrm((DEPTH, D_MODEL, IN_DIM), D_MODEL ** -0.5)
    inp['w_short_conv'] = nrm((DEPTH, SHORT_CONV, QKV_DIM), SHORT_CONV ** -0.5)
    inp['a_log'] = jnp.log(jax.random.uniform(next(ks), (DEPTH, N_HEADS), jnp.float32,
                                              minval=1.0, maxval=16.0))
    dt0 = jnp.exp(jax.random.uniform(next(ks), (DEPTH, N_HEADS), jnp.float32,
                                     minval=math.log(1e-3), maxval=math.log(1e-1)))
    inp['dt_bias'] = dt0 + jnp.log(-jnp.expm1(-dt0))
    inp['o_norm'] = gain(HEAD_V)
    inp['w_dw_conv'] = nrm((DEPTH, CONV_WIDTH, CONV_CH), CONV_WIDTH ** -0.5)
    inp['b_dw_conv'] = nrm((DEPTH, CONV_CH), 0.02)
    inp['ln_g'] = gain(CONV_CH)
    inp['ln_b'] = nrm((DEPTH, CONV_CH), 0.02)
    inp['w_branch_a'] = nrm((DEPTH, VAL_DIM, D_MODEL), VAL_DIM ** -0.5)
    inp['w_branch_b'] = nrm((DEPTH, CONV_CH, D_MODEL), CONV_CH ** -0.5)
    inp['w_out'] = nrm((DEPTH, D_MODEL, D_MODEL), D_MODEL ** -0.5)
    inp['mix_post'] = gain(D_MODEL)
    inp['ffn2_pre'] = gain(D_MODEL)
    inp['ffn2_w_gu'], inp['ffn2_w_down'] = ffn_w()
    inp['ffn2_post'] = gain(D_MODEL)
    inp['ple_pre'] = gain(D_MODEL)
    inp['w_ple_gate'] = nrm((DEPTH, D_MODEL, D_MODEL), D_MODEL ** -0.5)
    inp['w_ple_proj'] = nrm((DEPTH, PLE_DIM, D_MODEL), PLE_DIM ** -0.5)
    inp['ple_post'] = gain(D_MODEL)
    return inp


def reference(x_prompt, x_sample, p_prompt, p_sample, state_delta, state_qkv_conv,
              state_glu_conv, ffn1_pre, ffn1_w_gu, ffn1_w_down, ffn1_post, mix_pre, w_in,
              w_short_conv, a_log, dt_bias, o_norm, w_dw_conv, b_dw_conv, ln_g, ln_b,
              w_branch_a, w_branch_b, w_out, mix_post, ffn2_pre, ffn2_w_gu, ffn2_w_down,
              ffn2_post, ple_pre, w_ple_gate, w_ple_proj, ple_post):
    weights = (ffn1_pre, ffn1_w_gu, ffn1_w_down, ffn1_post, mix_pre, w_in, w_short_conv, a_log,
               dt_bias, o_norm, w_dw_conv, b_dw_conv, ln_g, ln_b, w_branch_a, w_branch_b,
               w_out, mix_post, ffn2_pre, ffn2_w_gu, ffn2_w_down, ffn2_post, ple_pre,
               w_ple_gate, w_ple_proj, ple_post)
    b = x_prompt.shape[0]
    dt = x_prompt.dtype
    zero_s = jnp.zeros((b, N_HEADS, HEAD_K, HEAD_V), dt)
    zero_qkv = jnp.zeros((b, SHORT_CONV - 1, QKV_DIM), dt)
    zero_glu = jnp.zeros((b, CONV_WIDTH - 1, CONV_CH), dt)
    yp, ys = x_prompt, x_sample
    sp_l, qp_l, gp_l, ss_l, qs_l, gs_l = [], [], [], [], [], []
    for i in range(DEPTH):
        wi = tuple(w[i] for w in weights)
        yp, sp, qp, gp = _layer(yp, p_prompt[i], zero_s, zero_qkv, zero_glu, wi)
        ys, ss, qs, gs = _layer(ys, p_sample[i], state_delta[i], state_qkv_conv[i],
                                state_glu_conv[i], wi)
        sp_l.append(sp); qp_l.append(qp); gp_l.append(gp)
        ss_l.append(ss); qs_l.append(qs); gs_l.append(gs)
    new_delta_prompt = jnp.stack(sp_l)
    new_qkv_conv_prompt = jnp.stack(qp_l)
    new_glu_conv_prompt = jnp.stack(gp_l)
    new_delta_sample = jnp.stack(ss_l)
    new_qkv_conv_sample = jnp.stack(qs_l)
    new_glu_conv_sample = jnp.stack(gs_l)
    return (yp, ys, new_delta_prompt, new_qkv_conv_prompt, new_glu_conv_prompt,
            new_delta_sample, new_qkv_conv_sample, new_glu_conv_sample)
```

```python
import functools

import jax
import jax.numpy as jnp
from jax import lax
from jax.experimental import pallas as pl
from jax.experimental.pallas import tpu as pltpu

f32 = jnp.float32
bf16 = jnp.bfloat16

D_MODEL = 2048
N_HEADS = 16
HEAD_K = 128
HEAD_V = 128
KEY_DIM = N_HEADS * HEAD_K
VAL_DIM = N_HEADS * HEAD_V
QKV_DIM = 2 * KEY_DIM + VAL_DIM
SHORT_CONV = 4
CONV_CH = D_MODEL
CONV_WIDTH = 31
D_FF = 5632
PLE_DIM = 256
EPS = 1e-6

O_Z = QKV_DIM
O_BETA = O_Z + VAL_DIM
O_A = O_BETA + N_HEADS
O_GLU = O_A + N_HEADS
O_GATE = O_GLU + 2 * CONV_CH

CHUNK = 64
HG = 4
NHG = N_HEADS // HG
HW = HG * HEAD_K
SEG_S = 8

VMEM_LIMIT = 56 * 1024 * 1024


def _sigmoid(x):
    return 1.0 / (1.0 + jnp.exp(-x))


def _silu(x):
    return x * _sigmoid(x)


def _rms(x, g):
    return x * lax.rsqrt(jnp.mean(x * x, axis=-1, keepdims=True) + EPS) * g


def _dot(a, b):
    return jnp.dot(a, b, preferred_element_type=f32)


def _dot_nt(a, b):
    return lax.dot_general(a, b, (((1,), (1,)), ((), ())), preferred_element_type=f32)


def _dot_tn(a, b):
    return lax.dot_general(a, b, (((0,), (0,)), ((), ())), preferred_element_type=f32)


def _dot_hi(a, b):
    return jnp.dot(a, b, precision=lax.Precision.HIGHEST, preferred_element_type=f32)


def _params(*sem):
    return pltpu.CompilerParams(dimension_semantics=sem, vmem_limit_bytes=VMEM_LIMIT)


def _ffn_kernel(x_ref, gpre_ref, wg_ref, wu_ref, wd_ref, gpost_ref, gnext_ref,
                h_ref, hn_ref, xn_scr, acc_scr):
    j = pl.program_id(1)

    @pl.when(j == 0)
    def _():
        xn_scr[...] = _rms(x_ref[...], gpre_ref[...]).astype(bf16)
        acc_scr[...] = jnp.zeros_like(acc_scr)

    xn = xn_scr[...]
    a = _silu(_dot(xn, wg_ref[...])) * _dot(xn, wu_ref[...])
    acc_scr[...] += _dot(a.astype(bf16), wd_ref[...])

    @pl.when(j == pl.num_programs(1) - 1)
    def _():
        h = x_ref[...] + 0.5 * _rms(acc_scr[...], gpost_ref[...])
        h_ref[...] = h
        hn_ref[...] = _rms(h, gnext_ref[...]).astype(bf16)


def _ffn(x, gpre, w_gu, w_down, gpost, gnext, tm=512, tf=512):
    t = x.shape[0]
    nf = D_FF // tf
    row = lambda i, j: (i, 0)
    vec = pl.BlockSpec((1, D_MODEL), lambda i, j: (0, 0))
    return pl.pallas_call(
        _ffn_kernel,
        grid=(t // tm, nf),
        in_specs=[pl.BlockSpec((tm, D_MODEL), row), vec,
                  pl.BlockSpec((D_MODEL, tf), lambda i, j: (0, j)),
                  pl.BlockSpec((D_MODEL, tf), lambda i, j: (0, j + nf)),
                  pl.BlockSpec((tf, D_MODEL), lambda i, j: (j, 0)), vec, vec],
        out_specs=[pl.BlockSpec((tm, D_MODEL), row), pl.BlockSpec((tm, D_MODEL), row)],
        out_shape=[jax.ShapeDtypeStruct((t, D_MODEL), f32), jax.ShapeDtypeStruct((t, D_MODEL), bf16)],
        scratch_shapes=[pltpu.VMEM((tm, D_MODEL), bf16), pltpu.VMEM((tm, D_MODEL), f32)],
        compiler_params=_params("parallel", "arbitrary"),
        name="ffn",
    )(x, gpre, w_gu, w_gu, w_down, gpost, gnext)


def _proj_kernel(x_ref, w_ref, o_ref, *, act):
    y = _dot(x_ref[...], w_ref[...])
    if act == "sigmoid":
        y = _sigmoid(y)
    o_ref[...] = y.astype(o_ref.dtype)


def _proj(x, w, n0, n, act, out_dtype, tm=512, tn=1024):
    t, k = x.shape
    j0 = n0 // tn
    return pl.pallas_call(
        functools.partial(_proj_kernel, act=act),
        grid=(t // tm, n // tn),
        in_specs=[pl.BlockSpec((tm, k), lambda i, j: (i, 0)),
                  pl.BlockSpec((k, tn), lambda i, j: (0, j + j0))],
        out_specs=pl.BlockSpec((tm, tn), lambda i, j: (i, j)),
        out_shape=jax.ShapeDtypeStruct((t, n), out_dtype),
        compiler_params=_params("parallel", "arbitrary"),
        name="proj_" + act,
    )(x, w)


def _glu_kernel(x_ref, wa_ref, wb_ref, o_ref):
    x = x_ref[...]
    o_ref[...] = _dot(x, wa_ref[...]) * _sigmoid(_dot(x, wb_ref[...]))


def _proj_glu(x, w, tm=512, tn=512):
    t, k = x.shape
    nj = CONV_CH // tn
    return pl.pallas_call(
        _glu_kernel,
        grid=(t // tm, nj),
        in_specs=[pl.BlockSpec((tm, k), lambda i, j: (i, 0)),
                  pl.BlockSpec((k, tn), lambda i, j: (0, j)),
                  pl.BlockSpec((k, tn), lambda i, j: (0, j + nj))],
        out_specs=pl.BlockSpec((tm, tn), lambda i, j: (i, j)),
        out_shape=jax.ShapeDtypeStruct((t, CONV_CH), f32),
        compiler_params=_params("parallel", "arbitrary"),
        name="proj_glu",
    )(x, w, w)


def _ba_kernel(x_ref, w_ref, alog_ref, dtb_ref, o_ref):
    y = _dot(x_ref[...], w_ref[...])
    lane = lax.broadcasted_iota(jnp.int32, y.shape, 1)
    ya = y + dtb_ref[...]
    softplus = jnp.maximum(ya, 0.0) + jnp.log(1.0 + jnp.exp(-jnp.abs(ya)))
    o_ref[...] = jnp.where(lane < HG, _sigmoid(y), -jnp.exp(alog_ref[...]) * softplus)


def _proj_ba(x, w_ba, alog_row, dtb_row, tm=512):
    t, k = x.shape
    vec = pl.BlockSpec((1, 128), lambda i, j: (0, j))
    return pl.pallas_call(
        _ba_kernel,
        grid=(t // tm, NHG),
        in_specs=[pl.BlockSpec((tm, k), lambda i, j: (i, 0)),
                  pl.BlockSpec((k, 128), lambda i, j: (0, j)), vec, vec],
        out_specs=pl.BlockSpec((tm, 128), lambda i, j: (i, j)),
        out_shape=jax.ShapeDtypeStruct((t, NHG * 128), f32),
        compiler_params=_params("parallel", "arbitrary"),
        name="proj_ba",
    )(x, w_ba, alog_row, dtb_row)


def _group_lanes(v):
    r = jnp.zeros((NHG, 128), f32).at[:, HG:2 * HG].set(v.astype(f32).reshape(NHG, HG))
    return r.reshape(1, NHG * 128)


def _ba_weight(w_in):
    wb = w_in[:, O_BETA:O_A].reshape(D_MODEL, NHG, HG)
    wa = w_in[:, O_A:O_GLU].reshape(D_MODEL, NHG, HG)
    w = jnp.zeros((D_MODEL, NHG, 128), w_in.dtype).at[:, :, :HG].set(wb).at[:, :, HG:2 * HG].set(wa)
    return w.reshape(D_MODEL, NHG * 128)


def _solve_masks(seg):
    r = lax.broadcasted_iota(jnp.int32, (CHUNK, CHUNK), 0)
    c = lax.broadcasted_iota(jnp.int32, (CHUNK, CHUNK), 1)
    same = (r // seg) == (c // seg)
    return same & (r >= c), same & (r > c), same, (r // 16) == (c // 16)


def _unit_lower_inverse(m, blk):
    r = lax.broadcasted_iota(jnp.int32, (CHUNK, CHUNK), 0)
    c = lax.broadcasted_iota(jnp.int32, (CHUNK, CHUNK), 1)
    eye = (r == c).astype(f32)
    mm = lambda a, b: _dot(a.astype(bf16), b.astype(bf16))
    d = jnp.where(blk, m, 0.0)
    lo = m - d
    d2 = mm(d, d)
    a1 = eye - d + d2 - mm(d, d2)
    d4 = mm(d2, d2)
    a2 = a1 + mm(a1, d4)
    td = a2 + mm(a2, mm(d4, d4))
    e = mm(td, lo)
    e2 = mm(e, e)
    f = eye - e + e2 - mm(e, e2)
    return mm(f, td)


def _chunk_local(q, k, v, beta, gcol, grow, gtot, masks):
    incl, strict, _, blk = masks
    qn = q * lax.rsqrt(jnp.sum(q * q, axis=-1, keepdims=True) + EPS) * (HEAD_K ** -0.5)
    kn = k * lax.rsqrt(jnp.sum(k * k, axis=-1, keepdims=True) + EPS)
    decay = jnp.exp(jnp.where(incl, gcol - grow, -jnp.inf))
    kb = kn.astype(bf16)
    kk = _dot_nt(kb, kb)
    qk = _dot_nt(qn.astype(bf16), kb)
    m = jnp.where(strict, beta * decay * kk, 0.0)
    gamma = jnp.exp(gcol)
    rhs = jnp.concatenate([beta * v, (beta * gamma) * kn], axis=1)
    sol = _dot(_unit_lower_inverse(m, blk).astype(bf16), rhs.astype(bf16))
    uv, wk = sol[:, :HEAD_V], sol[:, HEAD_V:]
    return uv, wk, qk * decay, qn * gamma, kn * jnp.exp(gtot - gcol)


def _cumsums(bac, masks):
    incl, _, same, _ = masks
    gc = _dot_hi(incl.astype(f32), bac)
    return gc, gc.T, _dot_hi(same.astype(f32), bac)


def _out_gate(o, onorm, z):
    return (_rms(o, onorm) * _silu(z)).astype(bf16)


def _gdn_prompt_kernel(q_ref, k_ref, v_ref, z_ref, ba_ref, wq_ref, wk_ref, wv_ref, onorm_ref,
                       oa_ref, s_ref,
                       eq, ek, ev, s_scr, qc_s, kc_s, vc_s, uv_s, wk_s, qg_s, kd_s, qkd_s, gl_s, *, rows):
    t = pl.program_id(2)
    nc = rows // CHUNK
    masks = _solve_masks(CHUNK)

    @pl.when(t == 0)
    def _():
        s_scr[...] = jnp.zeros_like(s_scr)
        for e in (eq, ek, ev):
            e[0:8, :] = jnp.zeros((8, HW), f32)

    eq[8:, :] = q_ref[...]
    ek[8:, :] = k_ref[...]
    ev[8:, :] = v_ref[...]

    for e, w_ref, dst in ((eq, wq_ref, qc_s), (ek, wk_ref, kc_s), (ev, wv_ref, vc_s)):
        for r0 in range(0, rows, CHUNK):
            acc = w_ref[0:1, :] * e[r0 + 5:r0 + 5 + CHUNK, :]
            for j in range(1, SHORT_CONV):
                acc += w_ref[j:j + 1, :] * e[r0 + 5 + j:r0 + 5 + j + CHUNK, :]
            dst[r0:r0 + CHUNK, :] = _silu(acc)

    def local(c, carry):
        r0 = pl.multiple_of(c * CHUNK, CHUNK)
        rs = pl.ds(r0, CHUNK)
        bac = ba_ref[rs, :]
        gc, gct, gtot = _cumsums(bac, masks)
        gl_s[rs, :] = jnp.exp(gtot)
        for h in range(HG):
            hs = slice(h * HEAD_K, (h + 1) * HEAD_K)
            gl = slice(HG + h, HG + h + 1)
            uv, wk, qkd, qg, kd = _chunk_local(qc_s[rs, hs], kc_s[rs, hs], vc_s[rs, hs], bac[:, h:h + 1],
                                               gc[:, gl], gct[gl, :], gtot[:, gl], masks)
            uv_s[rs, hs] = uv
            wk_s[rs, hs] = wk
            qg_s[rs, hs] = qg
            kd_s[rs, hs] = kd
            qkd_s[h, rs, :] = qkd
        return carry

    lax.fori_loop(0, nc, local, 0)

    def recur(c, carry):
        r0 = pl.multiple_of(c * CHUNK, CHUNK)
        rs = pl.ds(r0, CHUNK)
        for h in range(HG):
            hs = slice(h * HEAD_K, (h + 1) * HEAD_K)
            s = s_scr[h]
            wq = jnp.concatenate([wk_s[rs, hs], qg_s[rs, hs]], axis=0).astype(bf16)
            p = _dot(wq, s.astype(bf16))
            u = (uv_s[rs, hs] - p[:CHUNK]).astype(bf16)
            o = p[CHUNK:] + _dot(qkd_s[h, rs, :].astype(bf16), u)
            gl = gl_s[pl.ds(r0, 1), HG + h:HG + h + 1]
            s_scr[h] = gl * s + _dot_tn(kd_s[rs, hs].astype(bf16), u)
            oa_ref[rs, hs] = _out_gate(o, onorm_ref[...], z_ref[rs, hs])
        return carry

    lax.fori_loop(0, nc, recur, 0)

    for e in (eq, ek, ev):
        e[0:8, :] = e[rows:rows + 8, :]

    @pl.when(t == pl.num_programs(2) - 1)
    def _():
        s_ref[0] = s_scr[...]


def _gdn_prompt(qkvz, ba, w_conv, onorm, nb, seq, t_total, rows=512):
    nt = seq // rows
    rb = lambda b, hg, t: b * nt + t
    col = lambda off: pl.BlockSpec((rows, HW), lambda b, hg, t: (rb(b, hg, t), off * NHG + hg))
    wcol = lambda off: pl.BlockSpec((SHORT_CONV, HW), lambda b, hg, t: (0, off * NHG + hg))
    return pl.pallas_call(
        functools.partial(_gdn_prompt_kernel, rows=rows),
        grid=(nb, NHG, nt),
        in_specs=[col(0), col(1), col(2), col(3),
                  pl.BlockSpec((rows, 128), lambda b, hg, t: (rb(b, hg, t), hg)),
                  wcol(0), wcol(1), wcol(2),
                  pl.BlockSpec((1, HEAD_V), lambda b, hg, t: (0, 0))],
        out_specs=[pl.BlockSpec((rows, HW), lambda b, hg, t: (rb(b, hg, t), hg)),
                   pl.BlockSpec((1, HG, HEAD_K, HEAD_V), lambda b, hg, t: (b, hg, 0, 0))],
        out_shape=[jax.ShapeDtypeStruct((t_total, VAL_DIM), bf16),
                   jax.ShapeDtypeStruct((nb, N_HEADS, HEAD_K, HEAD_V), f32)],
        scratch_shapes=[pltpu.VMEM((rows + 8, HW), f32)] * 3
        + [pltpu.VMEM((HG, HEAD_K, HEAD_V), f32)]
        + [pltpu.VMEM((rows, HW), f32)] * 7
        + [pltpu.VMEM((HG, rows, CHUNK), f32), pltpu.VMEM((rows, 128), f32)],
        compiler_params=_params("parallel", "parallel", "arbitrary"),
        name="gdn_prompt",
    )(qkvz, qkvz, qkvz, qkvz, ba, w_conv, w_conv, w_conv, onorm)


def _gdn_sample_kernel(qe_ref, ke_ref, ve_ref, z_ref, ba_ref, wq_ref, wk_ref, wv_ref, onorm_ref,
                       s0_ref, oa_in_ref, oa_ref, s_ref):
    del oa_in_ref
    nseq = CHUNK // SEG_S
    masks = _solve_masks(SEG_S)

    def conv(e_ref, w_ref):
        acc = w_ref[0:1, :] * e_ref[:, 0:SEG_S, :].reshape(CHUNK, HW)
        for j in range(1, SHORT_CONV):
            acc += w_ref[j:j + 1, :] * e_ref[:, j:j + SEG_S, :].reshape(CHUNK, HW)
        return _silu(acc)

    qc, kc, vc = conv(qe_ref, wq_ref), conv(ke_ref, wk_ref), conv(ve_ref, wv_ref)
    bac = ba_ref[...]
    gc, gct, gtot = _cumsums(bac, masks)
    glv = jnp.exp(gtot)
    for h in range(HG):
        hs = slice(h * HEAD_K, (h + 1) * HEAD_K)
        gl = slice(HG + h, HG + h + 1)
        uv, wk, qkd, qg, kd = _chunk_local(qc[:, hs], kc[:, hs], vc[:, hs], bac[:, h:h + 1],
                                           gc[:, gl], gct[gl, :], gtot[:, gl], masks)
        us, os_ = [], []
        for j in range(nseq):
            js = slice(j * SEG_S, (j + 1) * SEG_S)
            s = s0_ref[j, h]
            wq = jnp.concatenate([wk[js], qg[js]], axis=0).astype(bf16)
            p = _dot(wq, s.astype(bf16))
            u = (uv[js] - p[:SEG_S]).astype(bf16)
            s_ref[j, h] = glv[j * SEG_S:j * SEG_S + 1, gl] * s + _dot_tn(kd[js].astype(bf16), u)
            us.append(u)
            os_.append(p[SEG_S:])
        o = jnp.concatenate(os_, axis=0) + _dot(qkd.astype(bf16), jnp.concatenate(us, axis=0))
        oa_ref[:, hs] = _out_gate(o, onorm_ref[...], z_ref[:, hs])


def _gdn_sample(qkv_ext, qkvz, ba, w_conv, onorm, s0, oa, row0):
    nb = qkv_ext.shape[0]
    nseq = CHUNK // SEG_S
    rb0 = row0 // CHUNK
    ext = lambda off: pl.BlockSpec((nseq, SEG_S + SHORT_CONV - 1, HW), lambda g, hg: (g, 0, off * NHG + hg))
    wcol = lambda off: pl.BlockSpec((SHORT_CONV, HW), lambda g, hg: (0, off * NHG + hg))
    sspec = pl.BlockSpec((nseq, HG, HEAD_K, HEAD_V), lambda g, hg: (g, hg, 0, 0))
    return pl.pallas_call(
        _gdn_sample_kernel,
        grid=(nb // nseq, NHG),
        in_specs=[ext(0), ext(1), ext(2),
                  pl.BlockSpec((CHUNK, HW), lambda g, hg: (rb0 + g, 3 * NHG + hg)),
                  pl.BlockSpec((CHUNK, 128), lambda g, hg: (rb0 + g, hg)),
                  wcol(0), wcol(1), wcol(2),
                  pl.BlockSpec((1, HEAD_V), lambda g, hg: (0, 0)),
                  sspec, pl.BlockSpec(memory_space=pl.ANY)],
        out_specs=[pl.BlockSpec((CHUNK, HW), lambda g, hg: (rb0 + g, hg)), sspec],
        out_shape=[jax.ShapeDtypeStruct(oa.shape, oa.dtype), jax.ShapeDtypeStruct(s0.shape, f32)],
        input_output_aliases={10: 0},
        compiler_params=_params("parallel", "parallel"),
        name="gdn_sample",
    )(qkv_ext, qkv_ext, qkv_ext, qkvz, ba, w_conv, w_conv, w_conv, onorm, s0, oa)


CONV_LC = 128
CONV_RB = 64


def _ln_silu(c, g, b):
    mu = jnp.mean(c, axis=-1, keepdims=True)
    xc = c - mu
    y = xc * lax.rsqrt(jnp.mean(xc * xc, axis=-1, keepdims=True) + EPS) * g + b
    return _silu(y).astype(bf16)


def _cconv_prompt_kernel(x_ref, w_ref, b_ref, g_ref, beta_ref, o_ref, ext, cbuf, *, rows):
    t = pl.program_id(1)
    pad = 32

    @pl.when(t == 0)
    def _():
        ext[0:pad, :] = jnp.zeros((pad, CONV_CH), f32)

    @pl.when(t > 0)
    def _():
        ext[0:pad, :] = ext[rows:rows + pad, :]

    ext[pad:, :] = x_ref[...]
    off = pad - (CONV_WIDTH - 1)
    for lc in range(CONV_CH // CONV_LC):
        ls = slice(lc * CONV_LC, (lc + 1) * CONV_LC)
        for r0 in range(0, rows, CONV_RB):
            acc = jnp.broadcast_to(b_ref[:, ls], (CONV_RB, CONV_LC))
            for j in range(CONV_WIDTH):
                acc += w_ref[j:j + 1, ls] * ext[r0 + off + j:r0 + off + j + CONV_RB, ls]
            cbuf[r0:r0 + CONV_RB, ls] = acc

    def norm(rb, carry):
        rs = pl.ds(pl.multiple_of(rb * 32, 32), 32)
        o_ref[rs, :] = _ln_silu(cbuf[rs, :], g_ref[...], beta_ref[...])
        return carry

    lax.fori_loop(0, rows // 32, norm, 0)


def _cconv_prompt(glu, w, b, g, beta, nb, seq, t_total, rows=128):
    nt = seq // rows
    vec = pl.BlockSpec((1, CONV_CH), lambda bi, t: (0, 0))
    return pl.pallas_call(
        functools.partial(_cconv_prompt_kernel, rows=rows),
        grid=(nb, nt),
        in_specs=[pl.BlockSpec((rows, CONV_CH), lambda bi, t: (bi * nt + t, 0)),
                  pl.BlockSpec((CONV_WIDTH, CONV_CH), lambda bi, t: (0, 0)), vec, vec, vec],
        out_specs=pl.BlockSpec((rows, CONV_CH), lambda bi, t: (bi * nt + t, 0)),
        out_shape=jax.ShapeDtypeStruct((t_total, CONV_CH), bf16),
        scratch_shapes=[pltpu.VMEM((rows + 32, CONV_CH), f32), pltpu.VMEM((rows, CONV_CH), f32)],
        compiler_params=_params("parallel", "arbitrary"),
        name="cconv_prompt",
    )(glu, w, b, g, beta)


def _cconv_sample_kernel(e_ref, w_ref, b_ref, g_ref, beta_ref, c_in_ref, o_ref, cbuf):
    del c_in_ref
    nseq = CHUNK // SEG_S
    for lc in range(CONV_CH // CONV_LC):
        ls = slice(lc * CONV_LC, (lc + 1) * CONV_LC)
        acc = jnp.broadcast_to(b_ref[:, ls], (CHUNK, CONV_LC))
        for j in range(CONV_WIDTH):
            acc += w_ref[j:j + 1, ls] * e_ref[:, j:j + SEG_S, ls].reshape(CHUNK, CONV_LC)
        cbuf[:, ls] = acc
    o_ref[...] = _ln_silu(cbuf[...], g_ref[...], beta_ref[...])


def _cconv_sample(glu_ext, w, b, g, beta, c, row0):
    nb, ext_rows, _ = glu_ext.shape
    nseq = CHUNK // SEG_S
    rb0 = row0 // CHUNK
    vec = pl.BlockSpec((1, CONV_CH), lambda i: (0, 0))
    return pl.pallas_call(
        _cconv_sample_kernel,
        grid=(nb // nseq,),
        in_specs=[pl.BlockSpec((nseq, ext_rows, CONV_CH), lambda i: (i, 0, 0)),
                  pl.BlockSpec((CONV_WIDTH, CONV_CH), lambda i: (0, 0)), vec, vec, vec,
                  pl.BlockSpec(memory_space=pl.ANY)],
        out_specs=pl.BlockSpec((CHUNK, CONV_CH), lambda i: (rb0 + i, 0)),
        out_shape=jax.ShapeDtypeStruct(c.shape, c.dtype),
        scratch_shapes=[pltpu.VMEM((CHUNK, CONV_CH), f32)],
        input_output_aliases={5: 0},
        compiler_params=_params("parallel"),
        name="cconv_sample",
    )(glu_ext, w, b, g, beta, c)


def _merge_kernel(oa_ref, c_ref, ga_ref, gb_ref, h_ref, wa_ref, wb_ref, wo_ref, gpost_ref, o_ref):
    merged = (ga_ref[...].astype(f32) * _dot(oa_ref[...], wa_ref[...])
              + gb_ref[...].astype(f32) * _dot(c_ref[...], wb_ref[...]))
    o_ref[...] = h_ref[...] + _rms(_dot(merged.astype(bf16), wo_ref[...]), gpost_ref[...])


def _merge(oa, c, gates, h, wa, wb, wo, gpost, tm=256):
    t = h.shape[0]
    row = lambda i: (i, 0)
    act = pl.BlockSpec((tm, D_MODEL), row)
    wsp = pl.BlockSpec((D_MODEL, D_MODEL), lambda i: (0, 0), pipeline_mode=pl.Buffered(1))
    return pl.pallas_call(
        _merge_kernel,
        grid=(t // tm,),
        in_specs=[act, act, act, pl.BlockSpec((tm, D_MODEL), lambda i: (i, 1)), act, wsp, wsp, wsp,
                  pl.BlockSpec((1, D_MODEL), lambda i: (0, 0))],
        out_specs=act,
        out_shape=jax.ShapeDtypeStruct((t, D_MODEL), f32),
        compiler_params=_params("parallel"),
        name="merge",
    )(oa, c, gates, gates, h, wa, wb, wo, gpost)


def _ple_kernel(hn_ref, p_ref, h_ref, wg_ref, wp_ref, gpost_ref, o_ref):
    gate = _sigmoid(_dot(hn_ref[...], wg_ref[...]))
    v = gate * _dot(p_ref[...].astype(bf16), wp_ref[...])
    o_ref[...] = h_ref[...] + _rms(v, gpost_ref[...])


def _ple(hn, p, h, wg, wp, gpost, tm=512):
    t = h.shape[0]
    row = lambda i: (i, 0)
    act = pl.BlockSpec((tm, D_MODEL), row)
    return pl.pallas_call(
        _ple_kernel,
        grid=(t // tm,),
        in_specs=[act, pl.BlockSpec((tm, PLE_DIM), row), act,
                  pl.BlockSpec((D_MODEL, D_MODEL), lambda i: (0, 0), pipeline_mode=pl.Buffered(1)),
                  pl.BlockSpec((PLE_DIM, D_MODEL), lambda i: (0, 0)),
                  pl.BlockSpec((1, D_MODEL), lambda i: (0, 0))],
        out_specs=act,
        out_shape=jax.ShapeDtypeStruct((t, D_MODEL), f32),
        compiler_params=_params("parallel"),
        name="ple",
    )(hn, p, h, wg, wp, gpost)


def _layer(x, p, nb, seq, s0, qkv_buf, glu_buf, w):
    (ffn1_pre, ffn1_w_gu, ffn1_w_down, ffn1_post, mix_pre, w_in, w_short_conv, a_log, dt_bias,
     o_norm, w_dw_conv, b_dw_conv, ln_g, ln_b, w_branch_a, w_branch_b, w_out, mix_post,
     ffn2_pre, ffn2_w_gu, ffn2_w_down, ffn2_post, ple_pre, w_ple_gate, w_ple_proj,
     ple_post) = w
    t_total = x.shape[0]
    tp = nb * seq
    ns = s0.shape[0]
    row = lambda v: v.astype(f32).reshape(1, -1)
    cast = lambda v: v.astype(bf16)

    h1, u = _ffn(x, row(ffn1_pre), cast(ffn1_w_gu), cast(ffn1_w_down), row(ffn1_post), row(mix_pre))

    w_in_b = cast(w_in)
    qkvz = _proj(u, w_in_b, 0, O_BETA, "none", f32)
    ba = _proj_ba(u, cast(_ba_weight(w_in)), _group_lanes(a_log), _group_lanes(dt_bias))
    glu = _proj_glu(u, w_in_b[:, O_GLU:O_GATE])
    gates = _proj(u, w_in_b[:, O_GATE:], 0, 2 * D_MODEL, "sigmoid", bf16)

    w_conv = w_short_conv.astype(f32)
    onorm = row(o_norm)
    oa, s_p = _gdn_prompt(qkvz, ba, w_conv, onorm, nb, seq, t_total)
    qkv_s = qkvz[tp:, :QKV_DIM].reshape(ns, SEG_S, QKV_DIM)
    qkv_ext = jnp.concatenate([qkv_buf.astype(f32), qkv_s], axis=1)
    oa, s_s = _gdn_sample(qkv_ext, qkvz, ba, w_conv, onorm, s0.astype(f32), oa, tp)

    cw, cb, lg, lb = w_dw_conv.astype(f32), row(b_dw_conv), row(ln_g), row(ln_b)
    c = _cconv_prompt(glu, cw, cb, lg, lb, nb, seq, t_total)
    glu_ext = jnp.concatenate([glu_buf.astype(f32), glu[tp:].reshape(ns, SEG_S, CONV_CH)], axis=1)
    c = _cconv_sample(glu_ext, cw, cb, lg, lb, c, tp)

    h2 = _merge(oa, c, gates, h1, cast(w_branch_a), cast(w_branch_b), cast(w_out), row(mix_post))
    h3, hn = _ffn(h2, row(ffn2_pre), cast(ffn2_w_gu), cast(ffn2_w_down), row(ffn2_post), row(ple_pre))
    y = _ple(hn, p, h3, cast(w_ple_gate), cast(w_ple_proj), row(ple_post))

    qkv_p = qkvz[:tp, :QKV_DIM].reshape(nb, seq, QKV_DIM)[:, seq - (SHORT_CONV - 1):]
    glu_p = glu[:tp].reshape(nb, seq, CONV_CH)[:, seq - (CONV_WIDTH - 1):]
    return (y, s_p, qkv_p, glu_p, s_s, qkv_ext[:, SEG_S:], glu_ext[:, SEG_S:])


def kernel(x_prompt, x_sample, p_prompt, p_sample, state_delta, state_qkv_conv, state_glu_conv, ffn1_pre, ffn1_w_gu, ffn1_w_down, ffn1_post, mix_pre, w_in, w_short_conv, a_log, dt_bias, o_norm, w_dw_conv, b_dw_conv, ln_g, ln_b, w_branch_a, w_branch_b, w_out, mix_post, ffn2_pre, ffn2_w_gu, ffn2_w_down, ffn2_post, ple_pre, w_ple_gate, w_ple_proj, ple_post):
    weights = (ffn1_pre, ffn1_w_gu, ffn1_w_down, ffn1_post, mix_pre, w_in, w_short_conv, a_log,
               dt_bias, o_norm, w_dw_conv, b_dw_conv, ln_g, ln_b, w_branch_a, w_branch_b,
               w_out, mix_post, ffn2_pre, ffn2_w_gu, ffn2_w_down, ffn2_post, ple_pre,
               w_ple_gate, w_ple_proj, ple_post)
    nb, seq, _ = x_prompt.shape
    ns, ls, _ = x_sample.shape
    assert ls == SEG_S and seq % CHUNK == 0 and ns % (CHUNK // SEG_S) == 0
    depth = ffn1_pre.shape[0]
    tp = nb * seq
    x = jnp.concatenate([x_prompt.reshape(tp, D_MODEL), x_sample.reshape(ns * ls, D_MODEL)], axis=0)
    outs = [[] for _ in range(6)]
    for i in range(depth):
        wi = tuple(wt[i] for wt in weights)
        p = jnp.concatenate([p_prompt[i].reshape(tp, PLE_DIM), p_sample[i].reshape(ns * ls, PLE_DIM)], axis=0)
        x, s_p, q_p, g_p, s_s, q_s, g_s = _layer(x, p, nb, seq, state_delta[i], state_qkv_conv[i],
                                                 state_glu_conv[i], wi)
        for lst, v in zip(outs, (s_p, q_p, g_p, s_s, q_s, g_s)):
            lst.append(v)
    yp = x[:tp].reshape(nb, seq, D_MODEL)
    ys = x[tp:].reshape(ns, ls, D_MODEL)
    return (yp, ys) + tuple(jnp.stack(lst) for lst in outs)
```

```python
import functools

import jax
import jax.numpy as jnp
from jax import lax
from jax.experimental import pallas as pl
from jax.experimental.pallas import tpu as pltpu

f32 = jnp.float32
bf16 = jnp.bfloat16

D_MODEL = 2048
N_HEADS = 16
HEAD_K = 128
HEAD_V = 128
KEY_DIM = N_HEADS * HEAD_K
VAL_DIM = N_HEADS * HEAD_V
QKV_DIM = 2 * KEY_DIM + VAL_DIM
SHORT_CONV = 4
CONV_CH = D_MODEL
CONV_WIDTH = 31
D_FF = 5632
PLE_DIM = 256
EPS = 1e-6

O_Z = QKV_DIM
O_BETA = O_Z + VAL_DIM
O_A = O_BETA + N_HEADS
O_GLU = O_A + N_HEADS
O_GATE = O_GLU + 2 * CONV_CH

CHUNK = 64
HG = 8
NHG = N_HEADS // HG
HW = HG * HEAD_K
SEG_S = 8
LOCAL_CHUNKS = 4

VMEM_LIMIT = 56 * 1024 * 1024


def _sigmoid(x):
    return 1.0 / (1.0 + jnp.exp(-x))


def _silu(x):
    return x * _sigmoid(x)


def _rms(x, g):
    return x * lax.rsqrt(jnp.mean(x * x, axis=-1, keepdims=True) + EPS) * g


def _dot(a, b):
    return jnp.dot(a, b, preferred_element_type=f32)


def _dot_nt(a, b):
    return lax.dot_general(a, b, (((1,), (1,)), ((), ())), preferred_element_type=f32)


def _params(*sem):
    return pltpu.CompilerParams(dimension_semantics=sem, vmem_limit_bytes=VMEM_LIMIT)


def _ffn_kernel(xa_ref, xb_ref, gpre_ref, wg_ref, wu_ref, wd_ref, gpost_ref, gnext_ref,
                h_ref, hn_ref, x_scr, xn_scr, acc_scr, *, na):
    i, j = pl.program_id(0), pl.program_id(1)

    @pl.when(j == 0)
    def _():
        @pl.when(i < na)
        def _():
            x_scr[...] = xa_ref[...]

        @pl.when(i >= na)
        def _():
            x_scr[...] = xb_ref[...]

        xn_scr[...] = _rms(x_scr[...], gpre_ref[...]).astype(bf16)
        acc_scr[...] = jnp.zeros_like(acc_scr)

    xn = xn_scr[...]
    a = _silu(_dot(xn, wg_ref[...])) * _dot(xn, wu_ref[...])
    acc_scr[...] += _dot(a.astype(bf16), wd_ref[...])

    @pl.when(j == pl.num_programs(1) - 1)
    def _():
        h = x_scr[...] + 0.5 * _rms(acc_scr[...], gpost_ref[...])
        h_ref[...] = h
        hn_ref[...] = _rms(h, gnext_ref[...]).astype(bf16)


def _ffn(xa, xb, gpre, w_gu, w_down, gpost, gnext, tm=512, tf=512):
    na = xa.shape[0] // tm
    nb = 0 if xb is None else xb.shape[0] // tm
    xb = xa if xb is None else xb
    t = (na + nb) * tm
    nf = D_FF // tf
    row = lambda i, j: (i, 0)
    vec = pl.BlockSpec((1, D_MODEL), lambda i, j: (0, 0))
    once = pl.Buffered(1)
    return pl.pallas_call(
        functools.partial(_ffn_kernel, na=na),
        grid=(na + nb, nf),
        in_specs=[pl.BlockSpec((tm, D_MODEL), lambda i, j: (jnp.minimum(i, na - 1), 0), pipeline_mode=once),
                  pl.BlockSpec((tm, D_MODEL), lambda i, j: (jnp.maximum(i - na, 0), 0), pipeline_mode=once),
                  vec,
                  pl.BlockSpec((D_MODEL, tf), lambda i, j: (0, j)),
                  pl.BlockSpec((D_MODEL, tf), lambda i, j: (0, j + nf)),
                  pl.BlockSpec((tf, D_MODEL), lambda i, j: (j, 0)), vec, vec],
        out_specs=[pl.BlockSpec((tm, D_MODEL), row), pl.BlockSpec((tm, D_MODEL), row)],
        out_shape=[jax.ShapeDtypeStruct((t, D_MODEL), f32), jax.ShapeDtypeStruct((t, D_MODEL), bf16)],
        scratch_shapes=[pltpu.VMEM((tm, D_MODEL), f32), pltpu.VMEM((tm, D_MODEL), bf16),
                        pltpu.VMEM((tm, D_MODEL), f32)],
        compiler_params=_params("parallel", "arbitrary"),
        name="ffn",
    )(xa, xb, gpre, w_gu, w_gu, w_down, gpost, gnext)


def _proj_kernel(x_ref, w_ref, o_ref, *, act):
    y = _dot(x_ref[...], w_ref[...])
    if act == "sigmoid":
        y = _sigmoid(y)
    o_ref[...] = y.astype(o_ref.dtype)


def _proj(x, w, n0, n, act, out_dtype, tm=1024, tn=1024):
    t, k = x.shape
    j0 = n0 // tn
    return pl.pallas_call(
        functools.partial(_proj_kernel, act=act),
        grid=(t // tm, n // tn),
        in_specs=[pl.BlockSpec((tm, k), lambda i, j: (i, 0)),
                  pl.BlockSpec((k, tn), lambda i, j: (0, j + j0))],
        out_specs=pl.BlockSpec((tm, tn), lambda i, j: (i, j)),
        out_shape=jax.ShapeDtypeStruct((t, n), out_dtype),
        compiler_params=_params("parallel", "arbitrary"),
        name="proj_" + act,
    )(x, w)


def _glu_kernel(x_ref, wa_ref, wb_ref, o_ref):
    x = x_ref[...]
    o_ref[...] = _dot(x, wa_ref[...]) * _sigmoid(_dot(x, wb_ref[...]))


def _proj_glu(x, w, tm=1024, tn=512):
    t, k = x.shape
    nj = CONV_CH // tn
    return pl.pallas_call(
        _glu_kernel,
        grid=(t // tm, nj),
        in_specs=[pl.BlockSpec((tm, k), lambda i, j: (i, 0)),
                  pl.BlockSpec((k, tn), lambda i, j: (0, j)),
                  pl.BlockSpec((k, tn), lambda i, j: (0, j + nj))],
        out_specs=pl.BlockSpec((tm, tn), lambda i, j: (i, j)),
        out_shape=jax.ShapeDtypeStruct((t, CONV_CH), f32),
        compiler_params=_params("parallel", "arbitrary"),
        name="proj_glu",
    )(x, w, w)


def _ba_kernel(x_ref, w_ref, alog_ref, dtb_ref, o_ref):
    y = _dot(x_ref[...], w_ref[...])
    lane = lax.broadcasted_iota(jnp.int32, y.shape, 1)
    ya = y + dtb_ref[...]
    softplus = jnp.maximum(ya, 0.0) + jnp.log(1.0 + jnp.exp(-jnp.abs(ya)))
    o_ref[...] = jnp.where(lane < HG, _sigmoid(y), -jnp.exp(alog_ref[...]) * softplus)


def _proj_ba(x, w_ba, alog_row, dtb_row, tm=1024):
    t, k = x.shape
    vec = pl.BlockSpec((1, 128), lambda i, j: (0, j))
    return pl.pallas_call(
        _ba_kernel,
        grid=(t // tm, NHG),
        in_specs=[pl.BlockSpec((tm, k), lambda i, j: (i, 0)),
                  pl.BlockSpec((k, 128), lambda i, j: (0, j)), vec, vec],
        out_specs=pl.BlockSpec((tm, 128), lambda i, j: (i, j)),
        out_shape=jax.ShapeDtypeStruct((t, NHG * 128), f32),
        compiler_params=_params("parallel", "arbitrary"),
        name="proj_ba",
    )(x, w_ba, alog_row, dtb_row)


def _group_lanes(v):
    r = jnp.zeros((NHG, 128), f32).at[:, HG:2 * HG].set(v.astype(f32).reshape(NHG, HG))
    return r.reshape(1, NHG * 128)


def _ba_weight(w_in):
    wb = w_in[:, O_BETA:O_A].reshape(D_MODEL, NHG, HG)
    wa = w_in[:, O_A:O_GLU].reshape(D_MODEL, NHG, HG)
    w = jnp.zeros((D_MODEL, NHG, 128), w_in.dtype).at[:, :, :HG].set(wb).at[:, :, HG:2 * HG].set(wa)
    return w.reshape(D_MODEL, NHG * 128)


def _seq_masks(n, seg):
    r = lax.broadcasted_iota(jnp.int32, (n, n), 0)
    c = lax.broadcasted_iota(jnp.int32, (n, n), 1)
    same = (r // seg) == (c // seg)
    return same & (r >= c), same & (r > c), same


def _mask_dot(mask, x):
    l = jnp.where(mask, 1.0, 0.0).astype(bf16)
    hi = x.astype(bf16)
    r1 = x - hi.astype(f32)
    mid = r1.astype(bf16)
    lo = (r1 - mid.astype(f32)).astype(bf16)
    return _dot(l, hi) + _dot(l, mid) + _dot(l, lo)


def _mm(a, b):
    return _dot(a.astype(bf16), b.astype(bf16))


def _unit_lower_solve(ms, rhss, seg):
    n = range(len(ms))
    r = lax.broadcasted_iota(jnp.int32, (CHUNK, CHUNK), 0)
    c = lax.broadcasted_iota(jnp.int32, (CHUNK, CHUNK), 1)
    eye = (r == c).astype(f32)
    if seg <= 8:
        d, lo = ms, None
    else:
        blk = (r // 16) == (c // 16)
        d = [jnp.where(blk, m, 0.0) for m in ms]
        lo = [ms[i] - d[i] for i in n]
    d2 = [_mm(d[i], d[i]) for i in n]
    d3 = [_mm(d[i], d2[i]) for i in n]
    d4 = [_mm(d2[i], d2[i]) for i in n]
    a = [eye - d[i] + d2[i] - d3[i] for i in n]
    td = [a[i] + _mm(a[i], d4[i]) for i in n]
    if seg > 8:
        d8 = [_mm(d4[i], d4[i]) for i in n]
        td = [td[i] + _mm(td[i], d8[i]) for i in n]
    y = [_mm(td[i], rhss[i]) for i in n]
    if lo is None:
        return y
    e = [_mm(td[i], lo[i]) for i in n]
    e2 = [_mm(e[i], e[i]) for i in n]
    w = [y[i] - _mm(e[i], y[i]) for i in n]
    return [w[i] + _mm(e2[i], w[i]) for i in n]


def _chunk_local(qs, ks, vs, betas, gcols, grows, gtots, incl, strict, seg):
    n = range(len(qs))
    qn = [q * lax.rsqrt(jnp.sum(q * q, axis=-1, keepdims=True) + EPS) * (HEAD_K ** -0.5) for q in qs]
    kn = [k * lax.rsqrt(jnp.sum(k * k, axis=-1, keepdims=True) + EPS) for k in ks]
    decay = [jnp.exp(jnp.where(incl, gcols[i] - grows[i], -jnp.inf)) for i in n]
    kb = [k.astype(bf16) for k in kn]
    kk = [_dot_nt(kb[i], kb[i]) for i in n]
    qk = [_dot_nt(qn[i].astype(bf16), kb[i]) for i in n]
    m = [jnp.where(strict, betas[i] * decay[i] * kk[i], 0.0) for i in n]
    gamma = [jnp.exp(g) for g in gcols]
    rhs = [jnp.concatenate([betas[i] * vs[i], (betas[i] * gamma[i]) * kn[i]], axis=1) for i in n]
    sol = _unit_lower_solve(m, rhs, seg)
    uv = [s[:, :HEAD_V] for s in sol]
    wk = [s[:, HEAD_V:] for s in sol]
    qkd = [qk[i] * decay[i] for i in n]
    qg = [qn[i] * gamma[i] for i in n]
    kd = [kn[i] * jnp.exp(gtots[i] - gcols[i]) for i in n]
    return uv, wk, qkd, qg, kd


def _out_gate(o, onorm, z):
    return (_rms(o, onorm) * _silu(z)).astype(bf16)


def _gdn_prompt_kernel(q_ref, k_ref, v_ref, z_ref, ba_ref, wq_ref, wk_ref, wv_ref, onorm_ref,
                       oa_ref, s_ref,
                       eq, ek, ev, s_scr, qc_s, kc_s, vc_s, gc_s, gt_s, gl_s, uv_s, wq_s, kdt_s, qkd_s,
                       *, rows):
    t = pl.program_id(2)
    nc = rows // CHUNK
    heads = range(HG)
    hs = [slice(h * HEAD_K, (h + 1) * HEAD_K) for h in heads]
    gs = [slice(HG + h, HG + h + 1) for h in heads]

    @pl.when(t == 0)
    def _():
        s_scr[...] = jnp.zeros_like(s_scr)
        for e in (eq, ek, ev):
            e[:, 0:8, :] = jnp.zeros((HG, 8, HEAD_K), f32)

    off = 8 - (SHORT_CONV - 1)
    for e, x_ref, w_ref, dst in ((eq, q_ref, wq_ref, qc_s), (ek, k_ref, wk_ref, kc_s), (ev, v_ref, wv_ref, vc_s)):
        for h in heads:
            e[h, 8:, :] = x_ref[:, hs[h]]
            for r0 in range(0, rows, CHUNK):
                acc = w_ref[0:1, hs[h]] * e[h, r0 + off:r0 + off + CHUNK, :]
                for j in range(1, SHORT_CONV):
                    acc += w_ref[j:j + 1, hs[h]] * e[h, r0 + off + j:r0 + off + j + CHUNK, :]
                dst[r0:r0 + CHUNK, hs[h]] = _silu(acc)

    incl_b, _, same_b = _seq_masks(rows, CHUNK)
    ba = ba_ref[...]
    gc_s[...] = _mask_dot(incl_b, ba)
    gtot_b = _mask_dot(same_b, ba)
    gt_s[...] = gtot_b
    gl_s[...] = jnp.exp(gtot_b)
    incl, strict, _ = _seq_masks(CHUNK, CHUNK)

    def local(cp, carry):
        cs = [cp * LOCAL_CHUNKS + i for i in range(LOCAL_CHUNKS)]
        rss = [pl.ds(pl.multiple_of(c * CHUNK, CHUNK), CHUNK) for c in cs]
        units = [(i, h) for i in range(LOCAL_CHUNKS) for h in heads]
        bac = [ba_ref[rs, :] for rs in rss]
        gc = [gc_s[rs, :] for rs in rss]
        gtot = [gt_s[rs, :] for rs in rss]
        gct = [g.T for g in gc]
        uv, wk, qkd, qg, kd = _chunk_local(
            [qc_s[rss[i], hs[h]] for i, h in units], [kc_s[rss[i], hs[h]] for i, h in units],
            [vc_s[rss[i], hs[h]] for i, h in units], [bac[i][:, h:h + 1] for i, h in units],
            [gc[i][:, gs[h]] for i, h in units], [gct[i][gs[h], :] for i, h in units],
            [gtot[i][:, gs[h]] for i, h in units], incl, strict, CHUNK)
        for n, (i, h) in enumerate(units):
            uv_s[rss[i], hs[h]] = uv[n]
            wq_s[cs[i], 0:CHUNK, hs[h]] = wk[n].astype(bf16)
            wq_s[cs[i], CHUNK:2 * CHUNK, hs[h]] = qg[n].astype(bf16)
            kdt_s[cs[i], h] = kd[n].T.astype(bf16)
            qkd_s[cs[i], h] = qkd[n].astype(bf16)
        return carry

    lax.fori_loop(0, nc // LOCAL_CHUNKS, local, 0)

    def recur(c, carry):
        r0 = pl.multiple_of(c * CHUNK, CHUNK)
        rs = pl.ds(r0, CHUNK)
        s = [s_scr[h] for h in heads]
        p = [_dot(wq_s[c, :, hs[h]], s[h].astype(bf16)) for h in heads]
        u = [(uv_s[rs, hs[h]] - p[h][:CHUNK]).astype(bf16) for h in heads]
        o = [p[h][CHUNK:] + _dot(qkd_s[c, h], u[h]) for h in heads]
        ds = [_dot(kdt_s[c, h], u[h]) for h in heads]
        for h in heads:
            s_scr[h] = gl_s[pl.ds(r0, 1), gs[h]] * s[h] + ds[h]
            oa_ref[rs, hs[h]] = _out_gate(o[h], onorm_ref[...], z_ref[rs, hs[h]])
        return carry

    lax.fori_loop(0, nc, recur, 0)

    for e in (eq, ek, ev):
        e[:, 0:8, :] = e[:, rows:rows + 8, :]

    @pl.when(t == pl.num_programs(2) - 1)
    def _():
        s_ref[0] = s_scr[...]


def _gdn_prompt(qkvz, ba, w_conv, onorm, nb, seq, t_total, rows=256):
    nt = seq // rows
    nc = rows // CHUNK
    rb = lambda b, hg, t: b * nt + t
    col = lambda off: pl.BlockSpec((rows, HW), lambda b, hg, t: (rb(b, hg, t), off * NHG + hg))
    wcol = lambda off: pl.BlockSpec((SHORT_CONV, HW), lambda b, hg, t: (0, off * NHG + hg))
    return pl.pallas_call(
        functools.partial(_gdn_prompt_kernel, rows=rows),
        grid=(nb, NHG, nt),
        in_specs=[col(0), col(1), col(2), col(3),
                  pl.BlockSpec((rows, 128), lambda b, hg, t: (rb(b, hg, t), hg)),
                  wcol(0), wcol(1), wcol(2),
                  pl.BlockSpec((1, HEAD_V), lambda b, hg, t: (0, 0))],
        out_specs=[pl.BlockSpec((rows, HW), lambda b, hg, t: (rb(b, hg, t), hg)),
                   pl.BlockSpec((1, HG, HEAD_K, HEAD_V), lambda b, hg, t: (b, hg, 0, 0))],
        out_shape=[jax.ShapeDtypeStruct((t_total, VAL_DIM), bf16),
                   jax.ShapeDtypeStruct((nb, N_HEADS, HEAD_K, HEAD_V), f32)],
        scratch_shapes=[pltpu.VMEM((HG, rows + 8, HEAD_K), f32)] * 3
        + [pltpu.VMEM((HG, HEAD_K, HEAD_V), f32)]
        + [pltpu.VMEM((rows, HW), f32)] * 3
        + [pltpu.VMEM((rows, 128), f32)] * 3
        + [pltpu.VMEM((rows, HW), f32),
           pltpu.VMEM((nc, 2 * CHUNK, HW), bf16),
           pltpu.VMEM((nc, HG, HEAD_K, CHUNK), bf16),
           pltpu.VMEM((nc, HG, CHUNK, CHUNK), bf16)],
        compiler_params=_params("parallel", "parallel", "arbitrary"),
        name="gdn_prompt",
    )(qkvz, qkvz, qkvz, qkvz, ba, w_conv, w_conv, w_conv, onorm)


def _gdn_sample_kernel(qe_ref, ke_ref, ve_ref, z_ref, ba_ref, wq_ref, wk_ref, wv_ref, onorm_ref,
                       s0_ref, oa_in_ref, oa_ref, s_ref):
    del oa_in_ref
    nseq = CHUNK // SEG_S
    heads = range(HG)
    seqs = range(nseq)
    hs = [slice(h * HEAD_K, (h + 1) * HEAD_K) for h in heads]
    gs = [slice(HG + h, HG + h + 1) for h in heads]
    js = [slice(j * SEG_S, (j + 1) * SEG_S) for j in seqs]
    incl, strict, same = _seq_masks(CHUNK, SEG_S)

    def conv(e_ref, w_ref):
        acc = w_ref[0:1, :] * e_ref[:, 0:SEG_S, :].reshape(CHUNK, HW)
        for j in range(1, SHORT_CONV):
            acc += w_ref[j:j + 1, :] * e_ref[:, j:j + SEG_S, :].reshape(CHUNK, HW)
        return _silu(acc)

    qc, kc, vc = conv(qe_ref, wq_ref), conv(ke_ref, wk_ref), conv(ve_ref, wv_ref)
    bac = ba_ref[...]
    gc = _mask_dot(incl, bac)
    gtot = _mask_dot(same, bac)
    gct = gc.T
    glv = jnp.exp(gtot)
    uv, wk, qkd, qg, kd = _chunk_local(
        [qc[:, hs[h]] for h in heads], [kc[:, hs[h]] for h in heads], [vc[:, hs[h]] for h in heads],
        [bac[:, h:h + 1] for h in heads], [gc[:, gs[h]] for h in heads], [gct[gs[h], :] for h in heads],
        [gtot[:, gs[h]] for h in heads], incl, strict, SEG_S)
    kdt = [kd[h].T.astype(bf16) for h in heads]
    p = [[_dot(jnp.concatenate([wk[h][js[j]], qg[h][js[j]]], axis=0).astype(bf16),
               s0_ref[j, h].astype(bf16)) for j in seqs] for h in heads]
    u = [[uv[h][js[j]] - p[h][j][:SEG_S] for j in seqs] for h in heads]
    zeros = jnp.zeros((SEG_S, HEAD_V), f32)
    for h in heads:
        for j in seqs:
            u_rows = jnp.concatenate([u[h][j] if i == j else zeros for i in seqs], axis=0).astype(bf16)
            s_ref[j, h] = glv[j * SEG_S:j * SEG_S + 1, gs[h]] * s0_ref[j, h] + _dot(kdt[h], u_rows)
    for h in heads:
        u_all = jnp.concatenate(u[h], axis=0).astype(bf16)
        o = jnp.concatenate([p[h][j][SEG_S:] for j in seqs], axis=0) + _dot(qkd[h].astype(bf16), u_all)
        oa_ref[:, hs[h]] = _out_gate(o, onorm_ref[...], z_ref[:, hs[h]])


def _gdn_sample(qkv_ext, qkvz, ba, w_conv, onorm, s0, oa, row0):
    nb = qkv_ext.shape[0]
    nseq = CHUNK // SEG_S
    rb0 = row0 // CHUNK
    ext = lambda off: pl.BlockSpec((nseq, SEG_S + SHORT_CONV - 1, HW), lambda g, hg: (g, 0, off * NHG + hg))
    wcol = lambda off: pl.BlockSpec((SHORT_CONV, HW), lambda g, hg: (0, off * NHG + hg))
    sspec = pl.BlockSpec((nseq, HG, HEAD_K, HEAD_V), lambda g, hg: (g, hg, 0, 0))
    return pl.pallas_call(
        _gdn_sample_kernel,
        grid=(nb // nseq, NHG),
        in_specs=[ext(0), ext(1), ext(2),
                  pl.BlockSpec((CHUNK, HW), lambda g, hg: (rb0 + g, 3 * NHG + hg)),
                  pl.BlockSpec((CHUNK, 128), lambda g, hg: (rb0 + g, hg)),
                  wcol(0), wcol(1), wcol(2),
                  pl.BlockSpec((1, HEAD_V), lambda g, hg: (0, 0)),
                  sspec, pl.BlockSpec(memory_space=pl.ANY)],
        out_specs=[pl.BlockSpec((CHUNK, HW), lambda g, hg: (rb0 + g, hg)), sspec],
        out_shape=[jax.ShapeDtypeStruct(oa.shape, oa.dtype), jax.ShapeDtypeStruct(s0.shape, f32)],
        input_output_aliases={10: 0},
        compiler_params=_params("parallel", "parallel"),
        name="gdn_sample",
    )(qkv_ext, qkv_ext, qkv_ext, qkvz, ba, w_conv, w_conv, w_conv, onorm, s0, oa)


CONV_LC = 128
CONV_RB = 64


def _ln_silu(c, g, b):
    mu = jnp.mean(c, axis=-1, keepdims=True)
    xc = c - mu
    y = xc * lax.rsqrt(jnp.mean(xc * xc, axis=-1, keepdims=True) + EPS) * g + b
    return _silu(y).astype(bf16)


def _cconv_prompt_kernel(x_ref, w_ref, b_ref, g_ref, beta_ref, o_ref, ext, cbuf, *, rows):
    t = pl.program_id(1)
    pad = 32

    @pl.when(t == 0)
    def _():
        ext[:, 0:pad, :] = jnp.zeros((CONV_CH // CONV_LC, pad, CONV_LC), f32)

    @pl.when(t > 0)
    def _():
        ext[:, 0:pad, :] = ext[:, rows:rows + pad, :]

    off = pad - (CONV_WIDTH - 1)
    for lc in range(CONV_CH // CONV_LC):
        ls = slice(lc * CONV_LC, (lc + 1) * CONV_LC)
        ext[lc, pad:, :] = x_ref[:, ls]
        for r0 in range(0, rows, CONV_RB):
            acc = jnp.broadcast_to(b_ref[:, ls], (CONV_RB, CONV_LC))
            for j in range(CONV_WIDTH):
                acc += w_ref[j:j + 1, ls] * ext[lc, r0 + off + j:r0 + off + j + CONV_RB, :]
            cbuf[r0:r0 + CONV_RB, ls] = acc

    def norm(rb, carry):
        rs = pl.ds(pl.multiple_of(rb * 32, 32), 32)
        o_ref[rs, :] = _ln_silu(cbuf[rs, :], g_ref[...], beta_ref[...])
        return carry

    lax.fori_loop(0, rows // 32, norm, 0)


def _cconv_prompt(glu, w, b, g, beta, nb, seq, t_total, rows=128):
    nt = seq // rows
    vec = pl.BlockSpec((1, CONV_CH), lambda bi, t: (0, 0))
    return pl.pallas_call(
        functools.partial(_cconv_prompt_kernel, rows=rows),
        grid=(nb, nt),
        in_specs=[pl.BlockSpec((rows, CONV_CH), lambda bi, t: (bi * nt + t, 0)),
                  pl.BlockSpec((CONV_WIDTH, CONV_CH), lambda bi, t: (0, 0)), vec, vec, vec],
        out_specs=pl.BlockSpec((rows, CONV_CH), lambda bi, t: (bi * nt + t, 0)),
        out_shape=jax.ShapeDtypeStruct((t_total, CONV_CH), bf16),
        scratch_shapes=[pltpu.VMEM((CONV_CH // CONV_LC, rows + 32, CONV_LC), f32),
                        pltpu.VMEM((rows, CONV_CH), f32)],
        compiler_params=_params("parallel", "arbitrary"),
        name="cconv_prompt",
    )(glu, w, b, g, beta)


def _cconv_sample_kernel(e_ref, w_ref, b_ref, g_ref, beta_ref, c_in_ref, o_ref, cbuf):
    del c_in_ref
    nseq = CHUNK // SEG_S
    for lc in range(CONV_CH // CONV_LC):
        ls = slice(lc * CONV_LC, (lc + 1) * CONV_LC)
        acc = jnp.broadcast_to(b_ref[:, ls], (CHUNK, CONV_LC))
        for j in range(CONV_WIDTH):
            acc += w_ref[j:j + 1, ls] * e_ref[:, j:j + SEG_S, ls].reshape(CHUNK, CONV_LC)
        cbuf[:, ls] = acc
    o_ref[...] = _ln_silu(cbuf[...], g_ref[...], beta_ref[...])


def _cconv_sample(glu_ext, w, b, g, beta, c, row0):
    nb, ext_rows, _ = glu_ext.shape
    nseq = CHUNK // SEG_S
    rb0 = row0 // CHUNK
    vec = pl.BlockSpec((1, CONV_CH), lambda i: (0, 0))
    return pl.pallas_call(
        _cconv_sample_kernel,
        grid=(nb // nseq,),
        in_specs=[pl.BlockSpec((nseq, ext_rows, CONV_CH), lambda i: (i, 0, 0)),
                  pl.BlockSpec((CONV_WIDTH, CONV_CH), lambda i: (0, 0)), vec, vec, vec,
                  pl.BlockSpec(memory_space=pl.ANY)],
        out_specs=pl.BlockSpec((CHUNK, CONV_CH), lambda i: (rb0 + i, 0)),
        out_shape=jax.ShapeDtypeStruct(c.shape, c.dtype),
        scratch_shapes=[pltpu.VMEM((CHUNK, CONV_CH), f32)],
        input_output_aliases={5: 0},
        compiler_params=_params("parallel"),
        name="cconv_sample",
    )(glu_ext, w, b, g, beta, c)


def _merge_kernel(oa_ref, c_ref, ga_ref, gb_ref, h_ref, wa_ref, wb_ref, wo_ref, gpost_ref, o_ref):
    merged = (ga_ref[...].astype(f32) * _dot(oa_ref[...], wa_ref[...])
              + gb_ref[...].astype(f32) * _dot(c_ref[...], wb_ref[...]))
    o_ref[...] = h_ref[...] + _rms(_dot(merged.astype(bf16), wo_ref[...]), gpost_ref[...])


def _merge(oa, c, gates, h, wa, wb, wo, gpost, tm=256):
    t = h.shape[0]
    row = lambda i: (i, 0)
    act = pl.BlockSpec((tm, D_MODEL), row)
    wsp = pl.BlockSpec((D_MODEL, D_MODEL), lambda i: (0, 0), pipeline_mode=pl.Buffered(1))
    return pl.pallas_call(
        _merge_kernel,
        grid=(t // tm,),
        in_specs=[act, act, act, pl.BlockSpec((tm, D_MODEL), lambda i: (i, 1)), act, wsp, wsp, wsp,
                  pl.BlockSpec((1, D_MODEL), lambda i: (0, 0))],
        out_specs=act,
        out_shape=jax.ShapeDtypeStruct((t, D_MODEL), f32),
        compiler_params=_params("parallel"),
        name="merge",
    )(oa, c, gates, gates, h, wa, wb, wo, gpost)


def _ple_kernel(hn_ref, pa_ref, pb_ref, h_ref, wg_ref, wp_ref, gpost_ref, oa_ref, ob_ref, *, na):
    i = pl.program_id(0)
    gate = _sigmoid(_dot(hn_ref[...], wg_ref[...]))

    def out(p_ref):
        v = gate * _dot(p_ref[...].astype(bf16), wp_ref[...])
        return h_ref[...] + _rms(v, gpost_ref[...])

    @pl.when(i < na)
    def _():
        oa_ref[...] = out(pa_ref)

    @pl.when(i >= na)
    def _():
        ob_ref[...] = out(pb_ref)


def _ple(hn, pa, pb, h, wg, wp, gpost, tm=512):
    na, nb = pa.shape[0] // tm, pb.shape[0] // tm
    row = lambda i: (i, 0)
    first = lambda i: (jnp.minimum(i, na - 1), 0)
    second = lambda i: (jnp.maximum(i - na, 0), 0)
    act = pl.BlockSpec((tm, D_MODEL), row)
    return pl.pallas_call(
        functools.partial(_ple_kernel, na=na),
        grid=(na + nb,),
        in_specs=[act, pl.BlockSpec((tm, PLE_DIM), first), pl.BlockSpec((tm, PLE_DIM), second), act,
                  pl.BlockSpec((D_MODEL, D_MODEL), lambda i: (0, 0), pipeline_mode=pl.Buffered(1)),
                  pl.BlockSpec((PLE_DIM, D_MODEL), lambda i: (0, 0)),
                  pl.BlockSpec((1, D_MODEL), lambda i: (0, 0))],
        out_specs=[pl.BlockSpec((tm, D_MODEL), first), pl.BlockSpec((tm, D_MODEL), second)],
        out_shape=[jax.ShapeDtypeStruct((na * tm, D_MODEL), f32), jax.ShapeDtypeStruct((nb * tm, D_MODEL), f32)],
        compiler_params=_params("arbitrary"),
        name="ple",
    )(hn, pa, pb, h, wg, wp, gpost)


def _layer(xp, xs, pp, ps, nb, seq, s0, qkv_buf, glu_buf, w):
    (ffn1_pre, ffn1_w_gu, ffn1_w_down, ffn1_post, mix_pre, w_in, w_short_conv, a_log, dt_bias,
     o_norm, w_dw_conv, b_dw_conv, ln_g, ln_b, w_branch_a, w_branch_b, w_out, mix_post,
     ffn2_pre, ffn2_w_gu, ffn2_w_down, ffn2_post, ple_pre, w_ple_gate, w_ple_proj,
     ple_post) = w
    tp = nb * seq
    t_total = tp + xs.shape[0]
    ns = s0.shape[0]
    row = lambda v: v.astype(f32).reshape(1, -1)
    cast = lambda v: v.astype(bf16)

    h1, u = _ffn(xp, xs, row(ffn1_pre), cast(ffn1_w_gu), cast(ffn1_w_down), row(ffn1_post), row(mix_pre))

    w_in_b = cast(w_in)
    qkvz = _proj(u, w_in_b, 0, O_BETA, "none", f32)
    ba = _proj_ba(u, cast(_ba_weight(w_in)), _group_lanes(a_log), _group_lanes(dt_bias))
    glu = _proj_glu(u, w_in_b[:, O_GLU:O_GATE])
    gates = _proj(u, w_in_b[:, O_GATE:], 0, 2 * D_MODEL, "sigmoid", bf16)

    w_conv = w_short_conv.astype(f32)
    onorm = row(o_norm)
    oa, s_p = _gdn_prompt(qkvz, ba, w_conv, onorm, nb, seq, t_total)
    qkv_s = qkvz[tp:, :QKV_DIM].reshape(ns, SEG_S, QKV_DIM)
    qkv_ext = jnp.concatenate([qkv_buf.astype(f32), qkv_s], axis=1)
    oa, s_s = _gdn_sample(qkv_ext, qkvz, ba, w_conv, onorm, s0.astype(f32), oa, tp)

    cw, cb, lg, lb = w_dw_conv.astype(f32), row(b_dw_conv), row(ln_g), row(ln_b)
    c = _cconv_prompt(glu, cw, cb, lg, lb, nb, seq, t_total)
    glu_ext = jnp.concatenate([glu_buf.astype(f32), glu[tp:].reshape(ns, SEG_S, CONV_CH)], axis=1)
    c = _cconv_sample(glu_ext, cw, cb, lg, lb, c, tp)

    h2 = _merge(oa, c, gates, h1, cast(w_branch_a), cast(w_branch_b), cast(w_out), row(mix_post))
    h3, hn = _ffn(h2, None, row(ffn2_pre), cast(ffn2_w_gu), cast(ffn2_w_down), row(ffn2_post), row(ple_pre))
    yp, ys = _ple(hn, pp, ps, h3, cast(w_ple_gate), cast(w_ple_proj), row(ple_post))

    qkv_p = qkvz[:tp, :QKV_DIM].reshape(nb, seq, QKV_DIM)[:, seq - (SHORT_CONV - 1):]
    glu_p = glu[:tp].reshape(nb, seq, CONV_CH)[:, seq - (CONV_WIDTH - 1):]
    return (yp, ys, s_p, qkv_p, glu_p, s_s, qkv_ext[:, SEG_S:], glu_ext[:, SEG_S:])


def kernel(x_prompt, x_sample, p_prompt, p_sample, state_delta, state_qkv_conv, state_glu_conv, ffn1_pre, ffn1_w_gu, ffn1_w_down, ffn1_post, mix_pre, w_in, w_short_conv, a_log, dt_bias, o_norm, w_dw_conv, b_dw_conv, ln_g, ln_b, w_branch_a, w_branch_b, w_out, mix_post, ffn2_pre, ffn2_w_gu, ffn2_w_down, ffn2_post, ple_pre, w_ple_gate, w_ple_proj, ple_post):
    weights = (ffn1_pre, ffn1_w_gu, ffn1_w_down, ffn1_post, mix_pre, w_in, w_short_conv, a_log,
               dt_bias, o_norm, w_dw_conv, b_dw_conv, ln_g, ln_b, w_branch_a, w_branch_b,
               w_out, mix_post, ffn2_pre, ffn2_w_gu, ffn2_w_down, ffn2_post, ple_pre,
               w_ple_gate, w_ple_proj, ple_post)
    nb, seq, _ = x_prompt.shape
    ns, ls, _ = x_sample.shape
    assert ls == SEG_S and seq % CHUNK == 0 and ns % (CHUNK // SEG_S) == 0
    depth = ffn1_pre.shape[0]
    tp = nb * seq
    xp, xs = x_prompt.reshape(tp, D_MODEL), x_sample.reshape(ns * ls, D_MODEL)
    outs = [[] for _ in range(6)]
    for i in range(depth):
        wi = tuple(wt[i] for wt in weights)
        xp, xs, s_p, q_p, g_p, s_s, q_s, g_s = _layer(
            xp, xs, p_prompt[i].reshape(tp, PLE_DIM), p_sample[i].reshape(ns * ls, PLE_DIM), nb, seq,
            state_delta[i], state_qkv_conv[i], state_glu_conv[i], wi)
        for lst, v in zip(outs, (s_p, q_p, g_p, s_s, q_s, g_s)):
            lst.append(v)
    return (xp.reshape(nb, seq, D_MODEL), xs.reshape(ns, ls, D_MODEL)) + tuple(jnp.stack(lst) for lst in outs)
```

```python
import functools

import jax
import jax.numpy as jnp
from jax import lax
from jax.experimental import pallas as pl
from jax.experimental.pallas import tpu as pltpu

f32 = jnp.float32
bf16 = jnp.bfloat16

D_MODEL = 2048
N_HEADS = 16
HEAD_K = 128
HEAD_V = 128
KEY_DIM = N_HEADS * HEAD_K
VAL_DIM = N_HEADS * HEAD_V
QKV_DIM = 2 * KEY_DIM + VAL_DIM
SHORT_CONV = 4
CONV_CH = D_MODEL
CONV_WIDTH = 31
D_FF = 5632
PLE_DIM = 256
EPS = 1e-6

O_Z = QKV_DIM
O_BETA = O_Z + VAL_DIM
O_A = O_BETA + N_HEADS
O_GLU = O_A + N_HEADS
O_GATE = O_GLU + 2 * CONV_CH

CHUNK = 64
HG = 8
NHG = N_HEADS // HG
HW = HG * HEAD_K
SEG_S = 8
LOCAL_CHUNKS = 4

VMEM_LIMIT = 56 * 1024 * 1024


def _sigmoid(x):
    return 1.0 / (1.0 + jnp.exp(-x))


def _silu(x):
    return x * _sigmoid(x)


def _rms(x, g):
    return x * lax.rsqrt(jnp.mean(x * x, axis=-1, keepdims=True) + EPS) * g


def _dot(a, b):
    return jnp.dot(a, b, preferred_element_type=f32)


def _dot_nt(a, b):
    return lax.dot_general(a, b, (((1,), (1,)), ((), ())), preferred_element_type=f32)


def _params(*sem):
    return pltpu.CompilerParams(dimension_semantics=sem, vmem_limit_bytes=VMEM_LIMIT)


def _per_source(i, na, refs, fn):
    if len(refs) == 1:
        fn(refs[0])
    else:
        pl.when(i < na)(lambda: fn(refs[0]))
        pl.when(i >= na)(lambda: fn(refs[1]))


def _split_specs(block, na, nsrc):
    if nsrc == 1:
        return [pl.BlockSpec(block, lambda i, *_: (i, 0))]
    return [pl.BlockSpec(block, lambda i, *_: (jnp.minimum(i, na - 1), 0)),
            pl.BlockSpec(block, lambda i, *_: (jnp.maximum(i - na, 0), 0))]


def _ffn_kernel(*refs, na, nsrc):
    x_refs = refs[:nsrc]
    gpre_ref, wg_ref, wu_ref, wd_ref, gpost_ref, gnext_ref, h_ref, hn_ref, xn_scr, acc_scr = refs[nsrc:]
    i, j = pl.program_id(0), pl.program_id(1)

    @pl.when(j == 0)
    def _():
        def norm_in(x_ref):
            xn_scr[...] = _rms(x_ref[...], gpre_ref[...]).astype(bf16)

        _per_source(i, na, x_refs, norm_in)
        acc_scr[...] = jnp.zeros_like(acc_scr)

    xn = xn_scr[...]
    a = _silu(_dot(xn, wg_ref[...])) * _dot(xn, wu_ref[...])
    acc_scr[...] += _dot(a.astype(bf16), wd_ref[...])

    @pl.when(j == pl.num_programs(1) - 1)
    def _():
        def residual_out(x_ref):
            h = x_ref[...] + 0.5 * _rms(acc_scr[...], gpost_ref[...])
            h_ref[...] = h
            hn_ref[...] = _rms(h, gnext_ref[...]).astype(bf16)

        _per_source(i, na, x_refs, residual_out)


def _ffn(xs, gpre, w_gu, w_down, gpost, gnext, tm=512, tf=512):
    na = xs[0].shape[0] // tm
    nt = sum(x.shape[0] for x in xs) // tm
    nf = D_FF // tf
    row = lambda i, j: (i, 0)
    vec = pl.BlockSpec((1, D_MODEL), lambda i, j: (0, 0))
    return pl.pallas_call(
        functools.partial(_ffn_kernel, na=na, nsrc=len(xs)),
        grid=(nt, nf),
        in_specs=_split_specs((tm, D_MODEL), na, len(xs))
        + [vec,
           pl.BlockSpec((D_MODEL, tf), lambda i, j: (0, j)),
           pl.BlockSpec((D_MODEL, tf), lambda i, j: (0, j + nf)),
           pl.BlockSpec((tf, D_MODEL), lambda i, j: (j, 0)), vec, vec],
        out_specs=[pl.BlockSpec((tm, D_MODEL), row), pl.BlockSpec((tm, D_MODEL), row)],
        out_shape=[jax.ShapeDtypeStruct((nt * tm, D_MODEL), f32), jax.ShapeDtypeStruct((nt * tm, D_MODEL), bf16)],
        scratch_shapes=[pltpu.VMEM((tm, D_MODEL), bf16), pltpu.VMEM((tm, D_MODEL), f32)],
        compiler_params=_params("parallel", "arbitrary"),
        name="ffn",
    )(*xs, gpre, w_gu, w_gu, w_down, gpost, gnext)


def _proj_kernel(x_ref, w_ref, o_ref, *, act):
    y = _dot(x_ref[...], w_ref[...])
    if act == "sigmoid":
        y = _sigmoid(y)
    o_ref[...] = y.astype(o_ref.dtype)


def _proj(x, w, n0, n, act, out_dtype, tm=1024, tn=1024):
    t, k = x.shape
    j0 = n0 // tn
    return pl.pallas_call(
        functools.partial(_proj_kernel, act=act),
        grid=(t // tm, n // tn),
        in_specs=[pl.BlockSpec((tm, k), lambda i, j: (i, 0)),
                  pl.BlockSpec((k, tn), lambda i, j: (0, j + j0))],
        out_specs=pl.BlockSpec((tm, tn), lambda i, j: (i, j)),
        out_shape=jax.ShapeDtypeStruct((t, n), out_dtype),
        compiler_params=_params("parallel", "arbitrary"),
        name="proj_" + act,
    )(x, w)


def _glu_kernel(x_ref, wa_ref, wb_ref, o_ref):
    x = x_ref[...]
    o_ref[...] = _dot(x, wa_ref[...]) * _sigmoid(_dot(x, wb_ref[...]))


def _proj_glu(x, w, tm=1024, tn=512):
    t, k = x.shape
    nj = CONV_CH // tn
    return pl.pallas_call(
        _glu_kernel,
        grid=(t // tm, nj),
        in_specs=[pl.BlockSpec((tm, k), lambda i, j: (i, 0)),
                  pl.BlockSpec((k, tn), lambda i, j: (0, j)),
                  pl.BlockSpec((k, tn), lambda i, j: (0, j + nj))],
        out_specs=pl.BlockSpec((tm, tn), lambda i, j: (i, j)),
        out_shape=jax.ShapeDtypeStruct((t, CONV_CH), f32),
        compiler_params=_params("parallel", "arbitrary"),
        name="proj_glu",
    )(x, w, w)


def _ba_kernel(x_ref, w_ref, alog_ref, dtb_ref, o_ref):
    y = _dot(x_ref[...], w_ref[...])
    lane = lax.broadcasted_iota(jnp.int32, y.shape, 1)
    ya = y + dtb_ref[...]
    softplus = jnp.maximum(ya, 0.0) + jnp.log(1.0 + jnp.exp(-jnp.abs(ya)))
    o_ref[...] = jnp.where(lane < HG, _sigmoid(y), -jnp.exp(alog_ref[...]) * softplus)


def _proj_ba(x, w_ba, alog_row, dtb_row, tm=1024):
    t, k = x.shape
    vec = pl.BlockSpec((1, 128), lambda i, j: (0, j))
    return pl.pallas_call(
        _ba_kernel,
        grid=(t // tm, NHG),
        in_specs=[pl.BlockSpec((tm, k), lambda i, j: (i, 0)),
                  pl.BlockSpec((k, 128), lambda i, j: (0, j)), vec, vec],
        out_specs=pl.BlockSpec((tm, 128), lambda i, j: (i, j)),
        out_shape=jax.ShapeDtypeStruct((t, NHG * 128), f32),
        compiler_params=_params("parallel", "arbitrary"),
        name="proj_ba",
    )(x, w_ba, alog_row, dtb_row)


def _group_lanes(v):
    r = jnp.zeros((NHG, 128), f32).at[:, HG:2 * HG].set(v.astype(f32).reshape(NHG, HG))
    return r.reshape(1, NHG * 128)


def _ba_weight(w_in):
    wb = w_in[:, O_BETA:O_A].reshape(D_MODEL, NHG, HG)
    wa = w_in[:, O_A:O_GLU].reshape(D_MODEL, NHG, HG)
    w = jnp.zeros((D_MODEL, NHG, 128), w_in.dtype).at[:, :, :HG].set(wb).at[:, :, HG:2 * HG].set(wa)
    return w.reshape(D_MODEL, NHG * 128)


def _seq_masks(n, seg):
    r = lax.broadcasted_iota(jnp.int32, (n, n), 0)
    c = lax.broadcasted_iota(jnp.int32, (n, n), 1)
    same = (r // seg) == (c // seg)
    return same & (r >= c), same & (r > c), same


def _mask_dot(mask, x):
    l = jnp.where(mask, 1.0, 0.0).astype(bf16)
    hi = x.astype(bf16)
    r1 = x - hi.astype(f32)
    mid = r1.astype(bf16)
    lo = (r1 - mid.astype(f32)).astype(bf16)
    return _dot(l, hi) + _dot(l, mid) + _dot(l, lo)


def _mm(a, b):
    return _dot(a.astype(bf16), b.astype(bf16))


def _unit_lower_solve(ms, rhss, seg):
    n = range(len(ms))
    r = lax.broadcasted_iota(jnp.int32, (CHUNK, CHUNK), 0)
    c = lax.broadcasted_iota(jnp.int32, (CHUNK, CHUNK), 1)
    eye = (r == c).astype(f32)
    if seg <= 8:
        d, lo = ms, None
    else:
        blk = (r // 16) == (c // 16)
        d = [jnp.where(blk, m, 0.0) for m in ms]
        lo = [ms[i] - d[i] for i in n]
    d2 = [_mm(d[i], d[i]) for i in n]
    d3 = [_mm(d[i], d2[i]) for i in n]
    d4 = [_mm(d2[i], d2[i]) for i in n]
    a = [eye - d[i] + d2[i] - d3[i] for i in n]
    td = [a[i] + _mm(a[i], d4[i]) for i in n]
    if seg > 8:
        d8 = [_mm(d4[i], d4[i]) for i in n]
        td = [td[i] + _mm(td[i], d8[i]) for i in n]
    y = [_mm(td[i], rhss[i]) for i in n]
    if lo is None:
        return y
    e = [_mm(td[i], lo[i]) for i in n]
    e2 = [_mm(e[i], e[i]) for i in n]
    w = [y[i] - _mm(e[i], y[i]) for i in n]
    return [w[i] + _mm(e2[i], w[i]) for i in n]


def _chunk_local(qs, ks, vs, betas, gcols, grows, gtots, incl, strict, seg):
    n = range(len(qs))
    qn = [q * lax.rsqrt(jnp.sum(q * q, axis=-1, keepdims=True) + EPS) * (HEAD_K ** -0.5) for q in qs]
    kn = [k * lax.rsqrt(jnp.sum(k * k, axis=-1, keepdims=True) + EPS) for k in ks]
    decay = [jnp.exp(jnp.where(incl, gcols[i] - grows[i], -jnp.inf)) for i in n]
    kb = [k.astype(bf16) for k in kn]
    kk = [_dot_nt(kb[i], kb[i]) for i in n]
    qk = [_dot_nt(qn[i].astype(bf16), kb[i]) for i in n]
    m = [jnp.where(strict, betas[i] * decay[i] * kk[i], 0.0) for i in n]
    gamma = [jnp.exp(g) for g in gcols]
    rhs = [jnp.concatenate([betas[i] * vs[i], (betas[i] * gamma[i]) * kn[i]], axis=1) for i in n]
    sol = _unit_lower_solve(m, rhs, seg)
    uv = [s[:, :HEAD_V] for s in sol]
    wk = [s[:, HEAD_V:] for s in sol]
    qkd = [qk[i] * decay[i] for i in n]
    qg = [qn[i] * gamma[i] for i in n]
    kd = [kn[i] * jnp.exp(gtots[i] - gcols[i]) for i in n]
    return uv, wk, qkd, qg, kd


def _out_gate(o, onorm, z):
    return (_rms(o, onorm) * _silu(z)).astype(bf16)


def _gdn_prompt_kernel(q_ref, k_ref, v_ref, z_ref, ba_ref, wq_ref, wk_ref, wv_ref, onorm_ref,
                       oa_ref, s_ref,
                       eq, ek, ev, s_scr, qc_s, kc_s, vc_s, gc_s, gt_s, gl_s, uv_s, wq_s, kdt_s, qkd_s,
                       *, rows):
    t = pl.program_id(2)
    nc = rows // CHUNK
    heads = range(HG)
    hs = [slice(h * HEAD_K, (h + 1) * HEAD_K) for h in heads]
    gs = [slice(HG + h, HG + h + 1) for h in heads]

    @pl.when(t == 0)
    def _():
        s_scr[...] = jnp.zeros_like(s_scr)
        for e in (eq, ek, ev):
            e[:, 0:8, :] = jnp.zeros((HG, 8, HEAD_K), f32)

    off = 8 - (SHORT_CONV - 1)
    for e, x_ref, w_ref, dst in ((eq, q_ref, wq_ref, qc_s), (ek, k_ref, wk_ref, kc_s), (ev, v_ref, wv_ref, vc_s)):
        for h in heads:
            e[h, 8:, :] = x_ref[:, hs[h]]
            for r0 in range(0, rows, CHUNK):
                acc = w_ref[0:1, hs[h]] * e[h, r0 + off:r0 + off + CHUNK, :]
                for j in range(1, SHORT_CONV):
                    acc += w_ref[j:j + 1, hs[h]] * e[h, r0 + off + j:r0 + off + j + CHUNK, :]
                dst[r0:r0 + CHUNK, hs[h]] = _silu(acc)

    incl_b, _, same_b = _seq_masks(rows, CHUNK)
    ba = ba_ref[...]
    gc_s[...] = _mask_dot(incl_b, ba)
    gtot_b = _mask_dot(same_b, ba)
    gt_s[...] = gtot_b
    gl_s[...] = jnp.exp(gtot_b)
    incl, strict, _ = _seq_masks(CHUNK, CHUNK)

    def local(cp, carry):
        cs = [cp * LOCAL_CHUNKS + i for i in range(LOCAL_CHUNKS)]
        rss = [pl.ds(pl.multiple_of(c * CHUNK, CHUNK), CHUNK) for c in cs]
        units = [(i, h) for i in range(LOCAL_CHUNKS) for h in heads]
        bac = [ba_ref[rs, :] for rs in rss]
        gc = [gc_s[rs, :] for rs in rss]
        gtot = [gt_s[rs, :] for rs in rss]
        gct = [g.T for g in gc]
        uv, wk, qkd, qg, kd = _chunk_local(
            [qc_s[rss[i], hs[h]] for i, h in units], [kc_s[rss[i], hs[h]] for i, h in units],
            [vc_s[rss[i], hs[h]] for i, h in units], [bac[i][:, h:h + 1] for i, h in units],
            [gc[i][:, gs[h]] for i, h in units], [gct[i][gs[h], :] for i, h in units],
            [gtot[i][:, gs[h]] for i, h in units], incl, strict, CHUNK)
        for n, (i, h) in enumerate(units):
            uv_s[rss[i], hs[h]] = uv[n]
            wq_s[cs[i], 0:CHUNK, hs[h]] = wk[n].astype(bf16)
            wq_s[cs[i], CHUNK:2 * CHUNK, hs[h]] = qg[n].astype(bf16)
            kdt_s[cs[i], h] = kd[n].T.astype(bf16)
            qkd_s[cs[i], h] = qkd[n].astype(bf16)
        return carry

    lax.fori_loop(0, nc // LOCAL_CHUNKS, local, 0)

    def recur(c, carry):
        r0 = pl.multiple_of(c * CHUNK, CHUNK)
        rs = pl.ds(r0, CHUNK)
        s = [s_scr[h] for h in heads]
        p = [_dot(wq_s[c, :, hs[h]], s[h].astype(bf16)) for h in heads]
        u = [(uv_s[rs, hs[h]] - p[h][:CHUNK]).astype(bf16) for h in heads]
        o = [p[h][CHUNK:] + _dot(qkd_s[c, h], u[h]) for h in heads]
        ds = [_dot(kdt_s[c, h], u[h]) for h in heads]
        for h in heads:
            s_scr[h] = gl_s[pl.ds(r0, 1), gs[h]] * s[h] + ds[h]
            oa_ref[rs, hs[h]] = _out_gate(o[h], onorm_ref[...], z_ref[rs, hs[h]])
        return carry

    lax.fori_loop(0, nc, recur, 0)

    for e in (eq, ek, ev):
        e[:, 0:8, :] = e[:, rows:rows + 8, :]

    @pl.when(t == pl.num_programs(2) - 1)
    def _():
        s_ref[0] = s_scr[...]


def _gdn_prompt(qkvz, ba, w_conv, onorm, nb, seq, rows=256):
    nt = seq // rows
    nc = rows // CHUNK
    rb = lambda b, hg, t: b * nt + t
    col = lambda off: pl.BlockSpec((rows, HW), lambda b, hg, t: (rb(b, hg, t), off * NHG + hg))
    wcol = lambda off: pl.BlockSpec((SHORT_CONV, HW), lambda b, hg, t: (0, off * NHG + hg))
    return pl.pallas_call(
        functools.partial(_gdn_prompt_kernel, rows=rows),
        grid=(nb, NHG, nt),
        in_specs=[col(0), col(1), col(2), col(3),
                  pl.BlockSpec((rows, 128), lambda b, hg, t: (rb(b, hg, t), hg)),
                  wcol(0), wcol(1), wcol(2),
                  pl.BlockSpec((1, HEAD_V), lambda b, hg, t: (0, 0))],
        out_specs=[pl.BlockSpec((rows, HW), lambda b, hg, t: (rb(b, hg, t), hg)),
                   pl.BlockSpec((1, HG, HEAD_K, HEAD_V), lambda b, hg, t: (b, hg, 0, 0))],
        out_shape=[jax.ShapeDtypeStruct((nb * seq, VAL_DIM), bf16),
                   jax.ShapeDtypeStruct((nb, N_HEADS, HEAD_K, HEAD_V), f32)],
        scratch_shapes=[pltpu.VMEM((HG, rows + 8, HEAD_K), f32)] * 3
        + [pltpu.VMEM((HG, HEAD_K, HEAD_V), f32)]
        + [pltpu.VMEM((rows, HW), f32)] * 3
        + [pltpu.VMEM((rows, 128), f32)] * 3
        + [pltpu.VMEM((rows, HW), f32),
           pltpu.VMEM((nc, 2 * CHUNK, HW), bf16),
           pltpu.VMEM((nc, HG, HEAD_K, CHUNK), bf16),
           pltpu.VMEM((nc, HG, CHUNK, CHUNK), bf16)],
        compiler_params=_params("parallel", "parallel", "arbitrary"),
        name="gdn_prompt",
    )(qkvz, qkvz, qkvz, qkvz, ba, w_conv, w_conv, w_conv, onorm)


def _gdn_sample_kernel(q_ref, k_ref, v_ref, qst_ref, kst_ref, vst_ref, z_ref, ba_ref, wq_ref, wk_ref, wv_ref,
                       onorm_ref, s0_ref, oa_ref, s_ref, e_scr):
    nseq = CHUNK // SEG_S
    heads = range(HG)
    seqs = range(nseq)
    hs = [slice(h * HEAD_K, (h + 1) * HEAD_K) for h in heads]
    gs = [slice(HG + h, HG + h + 1) for h in heads]
    js = [slice(j * SEG_S, (j + 1) * SEG_S) for j in seqs]
    incl, strict, same = _seq_masks(CHUNK, SEG_S)
    hist = SHORT_CONV - 1

    def conv(x_ref, st_ref, w_ref):
        e_scr[:, SEG_S - hist:SEG_S, :] = st_ref[...]
        e_scr[:, SEG_S:, :] = x_ref[...].reshape(nseq, SEG_S, HW)
        off = SEG_S - hist
        acc = w_ref[0:1, :] * e_scr[:, off:off + SEG_S, :].reshape(CHUNK, HW)
        for j in range(1, SHORT_CONV):
            acc += w_ref[j:j + 1, :] * e_scr[:, off + j:off + j + SEG_S, :].reshape(CHUNK, HW)
        return _silu(acc)

    qc, kc, vc = conv(q_ref, qst_ref, wq_ref), conv(k_ref, kst_ref, wk_ref), conv(v_ref, vst_ref, wv_ref)
    bac = ba_ref[...]
    gc = _mask_dot(incl, bac)
    gtot = _mask_dot(same, bac)
    gct = gc.T
    glv = jnp.exp(gtot)
    uv, wk, qkd, qg, kd = _chunk_local(
        [qc[:, hs[h]] for h in heads], [kc[:, hs[h]] for h in heads], [vc[:, hs[h]] for h in heads],
        [bac[:, h:h + 1] for h in heads], [gc[:, gs[h]] for h in heads], [gct[gs[h], :] for h in heads],
        [gtot[:, gs[h]] for h in heads], incl, strict, SEG_S)
    kdt = [kd[h].T.astype(bf16) for h in heads]
    p = [[_dot(jnp.concatenate([wk[h][js[j]], qg[h][js[j]]], axis=0).astype(bf16),
               s0_ref[j, h].astype(bf16)) for j in seqs] for h in heads]
    u = [[uv[h][js[j]] - p[h][j][:SEG_S] for j in seqs] for h in heads]
    zeros = jnp.zeros((SEG_S, HEAD_V), f32)
    for h in heads:
        for j in seqs:
            u_rows = jnp.concatenate([u[h][j] if i == j else zeros for i in seqs], axis=0).astype(bf16)
            s_ref[j, h] = glv[j * SEG_S:j * SEG_S + 1, gs[h]] * s0_ref[j, h] + _dot(kdt[h], u_rows)
    for h in heads:
        u_all = jnp.concatenate(u[h], axis=0).astype(bf16)
        o = jnp.concatenate([p[h][j][SEG_S:] for j in seqs], axis=0) + _dot(qkd[h].astype(bf16), u_all)
        oa_ref[:, hs[h]] = _out_gate(o, onorm_ref[...], z_ref[:, hs[h]])


def _gdn_sample(qkvz, qkv_state, ba, w_conv, onorm, s0, row0):
    nb = qkv_state.shape[0]
    nseq = CHUNK // SEG_S
    rb0 = row0 // CHUNK
    col = lambda off: pl.BlockSpec((CHUNK, HW), lambda g, hg: (rb0 + g, off * NHG + hg))
    st = lambda off: pl.BlockSpec((nseq, SHORT_CONV - 1, HW), lambda g, hg: (g, 0, off * NHG + hg))
    wcol = lambda off: pl.BlockSpec((SHORT_CONV, HW), lambda g, hg: (0, off * NHG + hg))
    sspec = pl.BlockSpec((nseq, HG, HEAD_K, HEAD_V), lambda g, hg: (g, hg, 0, 0))
    return pl.pallas_call(
        _gdn_sample_kernel,
        grid=(nb // nseq, NHG),
        in_specs=[col(0), col(1), col(2), st(0), st(1), st(2), col(3),
                  pl.BlockSpec((CHUNK, 128), lambda g, hg: (rb0 + g, hg)),
                  wcol(0), wcol(1), wcol(2),
                  pl.BlockSpec((1, HEAD_V), lambda g, hg: (0, 0)),
                  sspec],
        out_specs=[pl.BlockSpec((CHUNK, HW), lambda g, hg: (g, hg)), sspec],
        out_shape=[jax.ShapeDtypeStruct((nb * SEG_S, VAL_DIM), bf16), jax.ShapeDtypeStruct(s0.shape, f32)],
        scratch_shapes=[pltpu.VMEM((nseq, 2 * SEG_S, HW), f32)],
        compiler_params=_params("parallel", "parallel"),
        name="gdn_sample",
    )(qkvz, qkvz, qkvz, qkv_state, qkv_state, qkv_state, qkvz, ba, w_conv, w_conv, w_conv, onorm, s0)


CONV_LC = 128
CONV_RB = 64


def _ln_silu(c, g, b):
    mu = jnp.mean(c, axis=-1, keepdims=True)
    xc = c - mu
    y = xc * lax.rsqrt(jnp.mean(xc * xc, axis=-1, keepdims=True) + EPS) * g + b
    return _silu(y).astype(bf16)


def _cconv_prompt_kernel(x_ref, w_ref, b_ref, g_ref, beta_ref, o_ref, ext, cbuf, *, rows):
    t = pl.program_id(1)
    pad = 32

    @pl.when(t == 0)
    def _():
        ext[:, 0:pad, :] = jnp.zeros((CONV_CH // CONV_LC, pad, CONV_LC), f32)

    @pl.when(t > 0)
    def _():
        ext[:, 0:pad, :] = ext[:, rows:rows + pad, :]

    off = pad - (CONV_WIDTH - 1)
    for lc in range(CONV_CH // CONV_LC):
        ls = slice(lc * CONV_LC, (lc + 1) * CONV_LC)
        ext[lc, pad:, :] = x_ref[:, ls]
        for r0 in range(0, rows, CONV_RB):
            acc = jnp.broadcast_to(b_ref[:, ls], (CONV_RB, CONV_LC))
            for j in range(CONV_WIDTH):
                acc += w_ref[j:j + 1, ls] * ext[lc, r0 + off + j:r0 + off + j + CONV_RB, :]
            cbuf[r0:r0 + CONV_RB, ls] = acc

    def norm(rb, carry):
        rs = pl.ds(pl.multiple_of(rb * 32, 32), 32)
        o_ref[rs, :] = _ln_silu(cbuf[rs, :], g_ref[...], beta_ref[...])
        return carry

    lax.fori_loop(0, rows // 32, norm, 0)


def _cconv_prompt(glu, w, b, g, beta, nb, seq, rows=128):
    nt = seq // rows
    vec = pl.BlockSpec((1, CONV_CH), lambda bi, t: (0, 0))
    return pl.pallas_call(
        functools.partial(_cconv_prompt_kernel, rows=rows),
        grid=(nb, nt),
        in_specs=[pl.BlockSpec((rows, CONV_CH), lambda bi, t: (bi * nt + t, 0)),
                  pl.BlockSpec((CONV_WIDTH, CONV_CH), lambda bi, t: (0, 0)), vec, vec, vec],
        out_specs=pl.BlockSpec((rows, CONV_CH), lambda bi, t: (bi * nt + t, 0)),
        out_shape=jax.ShapeDtypeStruct((nb * seq, CONV_CH), bf16),
        scratch_shapes=[pltpu.VMEM((CONV_CH // CONV_LC, rows + 32, CONV_LC), f32),
                        pltpu.VMEM((rows, CONV_CH), f32)],
        compiler_params=_params("parallel", "arbitrary"),
        name="cconv_prompt",
    )(glu, w, b, g, beta)


def _cconv_sample_kernel(x_ref, st_ref, w_ref, b_ref, g_ref, beta_ref, o_ref, e_scr, cbuf):
    nseq = CHUNK // SEG_S
    hist = CONV_WIDTH - 1
    e_scr[:, 0:hist, :] = st_ref[...]
    e_scr[:, hist:hist + SEG_S, :] = x_ref[...].reshape(nseq, SEG_S, CONV_CH)
    for lc in range(CONV_CH // CONV_LC):
        ls = slice(lc * CONV_LC, (lc + 1) * CONV_LC)
        acc = jnp.broadcast_to(b_ref[:, ls], (CHUNK, CONV_LC))
        for j in range(CONV_WIDTH):
            acc += w_ref[j:j + 1, ls] * e_scr[:, j:j + SEG_S, ls].reshape(CHUNK, CONV_LC)
        cbuf[:, ls] = acc
    o_ref[...] = _ln_silu(cbuf[...], g_ref[...], beta_ref[...])


def _cconv_sample(glu, glu_state, w, b, g, beta, row0):
    nb, hist, _ = glu_state.shape
    nseq = CHUNK // SEG_S
    rb0 = row0 // CHUNK
    vec = pl.BlockSpec((1, CONV_CH), lambda i: (0, 0))
    return pl.pallas_call(
        _cconv_sample_kernel,
        grid=(nb // nseq,),
        in_specs=[pl.BlockSpec((CHUNK, CONV_CH), lambda i: (rb0 + i, 0)),
                  pl.BlockSpec((nseq, hist, CONV_CH), lambda i: (i, 0, 0)),
                  pl.BlockSpec((CONV_WIDTH, CONV_CH), lambda i: (0, 0)), vec, vec, vec],
        out_specs=pl.BlockSpec((CHUNK, CONV_CH), lambda i: (i, 0)),
        out_shape=jax.ShapeDtypeStruct((nb * SEG_S, CONV_CH), bf16),
        scratch_shapes=[pltpu.VMEM((nseq, hist + SEG_S, CONV_CH), f32), pltpu.VMEM((CHUNK, CONV_CH), f32)],
        compiler_params=_params("parallel"),
        name="cconv_sample",
    )(glu, glu_state, w, b, g, beta)


def _merge_kernel(oap_ref, oas_ref, cp_ref, cs_ref, ga_ref, gb_ref, h_ref, wa_ref, wb_ref, wo_ref, gpost_ref,
                  o_ref, *, na):
    def run(oa_ref, c_ref):
        merged = (ga_ref[...].astype(f32) * _dot(oa_ref[...], wa_ref[...])
                  + gb_ref[...].astype(f32) * _dot(c_ref[...], wb_ref[...]))
        o_ref[...] = h_ref[...] + _rms(_dot(merged.astype(bf16), wo_ref[...]), gpost_ref[...])

    i = pl.program_id(0)
    pl.when(i < na)(lambda: run(oap_ref, cp_ref))
    pl.when(i >= na)(lambda: run(oas_ref, cs_ref))


def _merge(oa_p, oa_s, c_p, c_s, gates, h, wa, wb, wo, gpost, tm=256):
    t = h.shape[0]
    na = oa_p.shape[0] // tm
    act = pl.BlockSpec((tm, D_MODEL), lambda i: (i, 0))
    src = _split_specs((tm, D_MODEL), na, 2)
    wsp = pl.BlockSpec((D_MODEL, D_MODEL), lambda i: (0, 0), pipeline_mode=pl.Buffered(1))
    return pl.pallas_call(
        functools.partial(_merge_kernel, na=na),
        grid=(t // tm,),
        in_specs=src + src + [act, pl.BlockSpec((tm, D_MODEL), lambda i: (i, 1)), act, wsp, wsp, wsp,
                              pl.BlockSpec((1, D_MODEL), lambda i: (0, 0))],
        out_specs=act,
        out_shape=jax.ShapeDtypeStruct((t, D_MODEL), f32),
        compiler_params=_params("parallel"),
        name="merge",
    )(oa_p, oa_s, c_p, c_s, gates, gates, h, wa, wb, wo, gpost)


def _ple_kernel(hn_ref, pa_ref, pb_ref, h_ref, wg_ref, wp_ref, gpost_ref, oa_ref, ob_ref, *, na):
    i = pl.program_id(0)
    gate = _sigmoid(_dot(hn_ref[...], wg_ref[...]))

    def out(p_ref):
        v = gate * _dot(p_ref[...].astype(bf16), wp_ref[...])
        return h_ref[...] + _rms(v, gpost_ref[...])

    @pl.when(i < na)
    def _():
        oa_ref[...] = out(pa_ref)

    @pl.when(i >= na)
    def _():
        ob_ref[...] = out(pb_ref)


def _ple(hn, pa, pb, h, wg, wp, gpost, tm=512):
    na, nb = pa.shape[0] // tm, pb.shape[0] // tm
    row = lambda i: (i, 0)
    first = lambda i: (jnp.minimum(i, na - 1), 0)
    second = lambda i: (jnp.maximum(i - na, 0), 0)
    act = pl.BlockSpec((tm, D_MODEL), row)
    return pl.pallas_call(
        functools.partial(_ple_kernel, na=na),
        grid=(na + nb,),
        in_specs=[act, pl.BlockSpec((tm, PLE_DIM), first), pl.BlockSpec((tm, PLE_DIM), second), act,
                  pl.BlockSpec((D_MODEL, D_MODEL), lambda i: (0, 0), pipeline_mode=pl.Buffered(1)),
                  pl.BlockSpec((PLE_DIM, D_MODEL), lambda i: (0, 0)),
                  pl.BlockSpec((1, D_MODEL), lambda i: (0, 0))],
        out_specs=[pl.BlockSpec((tm, D_MODEL), first), pl.BlockSpec((tm, D_MODEL), second)],
        out_shape=[jax.ShapeDtypeStruct((na * tm, D_MODEL), f32), jax.ShapeDtypeStruct((nb * tm, D_MODEL), f32)],
        compiler_params=_params("arbitrary"),
        name="ple",
    )(hn, pa, pb, h, wg, wp, gpost)


def _layer(xp, xs, pp, ps, nb, seq, s0, qkv_buf, glu_buf, w):
    (ffn1_pre, ffn1_w_gu, ffn1_w_down, ffn1_post, mix_pre, w_in, w_short_conv, a_log, dt_bias,
     o_norm, w_dw_conv, b_dw_conv, ln_g, ln_b, w_branch_a, w_branch_b, w_out, mix_post,
     ffn2_pre, ffn2_w_gu, ffn2_w_down, ffn2_post, ple_pre, w_ple_gate, w_ple_proj,
     ple_post) = w
    tp = nb * seq
    ns = s0.shape[0]
    row = lambda v: v.astype(f32).reshape(1, -1)
    cast = lambda v: v.astype(bf16)

    h1, u = _ffn([xp, xs], row(ffn1_pre), cast(ffn1_w_gu), cast(ffn1_w_down), row(ffn1_post), row(mix_pre))

    qkvz = _proj(u, cast(w_in[:, :O_BETA]), 0, O_BETA, "none", f32)
    ba = _proj_ba(u, cast(_ba_weight(w_in)), _group_lanes(a_log), _group_lanes(dt_bias))
    glu = _proj_glu(u, cast(w_in[:, O_GLU:O_GATE]))
    gates = _proj(u, cast(w_in[:, O_GATE:]), 0, 2 * D_MODEL, "sigmoid", bf16)

    w_conv = w_short_conv.astype(f32)
    onorm = row(o_norm)
    oa_p, s_p = _gdn_prompt(qkvz, ba, w_conv, onorm, nb, seq)
    oa_s, s_s = _gdn_sample(qkvz, qkv_buf.astype(f32), ba, w_conv, onorm, s0.astype(f32), tp)

    cw, cb, lg, lb = w_dw_conv.astype(f32), row(b_dw_conv), row(ln_g), row(ln_b)
    c_p = _cconv_prompt(glu, cw, cb, lg, lb, nb, seq)
    c_s = _cconv_sample(glu, glu_buf.astype(f32), cw, cb, lg, lb, tp)

    h2 = _merge(oa_p, oa_s, c_p, c_s, gates, h1, cast(w_branch_a), cast(w_branch_b), cast(w_out), row(mix_post))
    h3, hn = _ffn([h2], row(ffn2_pre), cast(ffn2_w_gu), cast(ffn2_w_down), row(ffn2_post), row(ple_pre))
    yp, ys = _ple(hn, pp, ps, h3, cast(w_ple_gate), cast(w_ple_proj), row(ple_post))

    tail = lambda a, width, n: jnp.stack([a[(b + 1) * seq - n:(b + 1) * seq, :width] for b in range(nb)])
    qkv_p = tail(qkvz, QKV_DIM, SHORT_CONV - 1)
    glu_p = tail(glu, CONV_CH, CONV_WIDTH - 1)
    qkv_s = qkvz[tp:, :QKV_DIM].reshape(ns, SEG_S, QKV_DIM)[:, SEG_S - (SHORT_CONV - 1):]
    glu_s = jnp.concatenate([glu_buf.astype(f32)[:, SEG_S:], glu[tp:].reshape(ns, SEG_S, CONV_CH)], axis=1)
    return (yp, ys, s_p, qkv_p, glu_p, s_s, qkv_s, glu_s)


def kernel(x_prompt, x_sample, p_prompt, p_sample, state_delta, state_qkv_conv, state_glu_conv, ffn1_pre, ffn1_w_gu, ffn1_w_down, ffn1_post, mix_pre, w_in, w_short_conv, a_log, dt_bias, o_norm, w_dw_conv, b_dw_conv, ln_g, ln_b, w_branch_a, w_branch_b, w_out, mix_post, ffn2_pre, ffn2_w_gu, ffn2_w_down, ffn2_post, ple_pre, w_ple_gate, w_ple_proj, ple_post):
    weights = (ffn1_pre, ffn1_w_gu, ffn1_w_down, ffn1_post, mix_pre, w_in, w_short_conv, a_log,
               dt_bias, o_norm, w_dw_conv, b_dw_conv, ln_g, ln_b, w_branch_a, w_branch_b,
               w_out, mix_post, ffn2_pre, ffn2_w_gu, ffn2_w_down, ffn2_post, ple_pre,
               w_ple_gate, w_ple_proj, ple_post)
    nb, seq, _ = x_prompt.shape
    ns, ls, _ = x_sample.shape
    assert ls == SEG_S and seq % CHUNK == 0 and ns % (CHUNK // SEG_S) == 0
    depth = ffn1_pre.shape[0]
    tp = nb * seq
    xp, xs = x_prompt.reshape(tp, D_MODEL), x_sample.reshape(ns * ls, D_MODEL)
    outs = [[] for _ in range(6)]
    for i in range(depth):
        wi = tuple(wt[i] for wt in weights)
        xp, xs, s_p, q_p, g_p, s_s, q_s, g_s = _layer(
            xp, xs, p_prompt[i].reshape(tp, PLE_DIM), p_sample[i].reshape(ns * ls, PLE_DIM), nb, seq,
            state_delta[i], state_qkv_conv[i], state_glu_conv[i], wi)
        for lst, v in zip(outs, (s_p, q_p, g_p, s_s, q_s, g_s)):
            lst.append(v)
    return (xp.reshape(nb, seq, D_MODEL), xs.reshape(ns, ls, D_MODEL)) + tuple(jnp.stack(lst) for lst in outs)
```

```python
import functools

import jax
import jax.numpy as jnp
from jax import lax
from jax.experimental import pallas as pl
from jax.experimental.pallas import tpu as pltpu

f32 = jnp.float32
bf16 = jnp.bfloat16

D_MODEL = 2048
N_HEADS = 16
HEAD_K = 128
HEAD_V = 128
KEY_DIM = N_HEADS * HEAD_K
VAL_DIM = N_HEADS * HEAD_V
QKV_DIM = 2 * KEY_DIM + VAL_DIM
SHORT_CONV = 4
CONV_CH = D_MODEL
CONV_WIDTH = 31
D_FF = 5632
PLE_DIM = 256
EPS = 1e-6

O_Z = QKV_DIM
O_BETA = O_Z + VAL_DIM
O_A = O_BETA + N_HEADS
O_GLU = O_A + N_HEADS
O_GATE = O_GLU + 2 * CONV_CH

CHUNK = 64
HG = 8
NHG = N_HEADS // HG
HW = HG * HEAD_K
SEG_S = 8
LOCAL_CHUNKS = 4

VMEM_LIMIT = 56 * 1024 * 1024


def _sigmoid(x):
    return 1.0 / (1.0 + jnp.exp(-x))


def _silu(x):
    return x * _sigmoid(x)


def _rms(x, g):
    return x * lax.rsqrt(jnp.mean(x * x, axis=-1, keepdims=True) + EPS) * g


def _dot(a, b):
    return jnp.dot(a, b, preferred_element_type=f32)


def _dot_nt(a, b):
    return lax.dot_general(a, b, (((1,), (1,)), ((), ())), preferred_element_type=f32)


def _params(*sem):
    return pltpu.CompilerParams(dimension_semantics=sem, vmem_limit_bytes=VMEM_LIMIT)


def _per_source(i, na, refs, fn):
    if len(refs) == 1:
        fn(refs[0])
    else:
        pl.when(i < na)(lambda: fn(refs[0]))
        pl.when(i >= na)(lambda: fn(refs[1]))


def _split_specs(block, na, nsrc, col=0):
    if nsrc == 1:
        return [pl.BlockSpec(block, lambda i, *_: (i, col))]
    return [pl.BlockSpec(block, lambda i, *_: (jnp.minimum(i, na - 1), col)),
            pl.BlockSpec(block, lambda i, *_: (jnp.maximum(i - na, 0), col))]


def _ffn_kernel(*refs, na, nsrc):
    x_refs = refs[:nsrc]
    gpre_ref, wg_ref, wu_ref, wd_ref, gpost_ref, gnext_ref, h_ref, hn_ref, xn_scr, acc_scr = refs[nsrc:]
    i, j = pl.program_id(0), pl.program_id(1)

    @pl.when(j == 0)
    def _():
        def norm_in(x_ref):
            xn_scr[...] = _rms(x_ref[...], gpre_ref[...]).astype(bf16)

        _per_source(i, na, x_refs, norm_in)
        acc_scr[...] = jnp.zeros_like(acc_scr)

    xn = xn_scr[...]
    a = _silu(_dot(xn, wg_ref[...])) * _dot(xn, wu_ref[...])
    acc_scr[...] += _dot(a.astype(bf16), wd_ref[...])

    @pl.when(j == pl.num_programs(1) - 1)
    def _():
        def residual_out(x_ref):
            h = x_ref[...] + 0.5 * _rms(acc_scr[...], gpost_ref[...])
            h_ref[...] = h
            hn_ref[...] = _rms(h, gnext_ref[...]).astype(bf16)

        _per_source(i, na, x_refs, residual_out)


def _ffn(xs, gpre, w_gu, w_down, gpost, gnext, tm=512, tf=512):
    na = xs[0].shape[0] // tm
    nt = sum(x.shape[0] for x in xs) // tm
    nf = D_FF // tf
    row = lambda i, j: (i, 0)
    vec = pl.BlockSpec((1, D_MODEL), lambda i, j: (0, 0))
    return pl.pallas_call(
        functools.partial(_ffn_kernel, na=na, nsrc=len(xs)),
        grid=(nt, nf),
        in_specs=_split_specs((tm, D_MODEL), na, len(xs))
        + [vec,
           pl.BlockSpec((D_MODEL, tf), lambda i, j: (0, j)),
           pl.BlockSpec((D_MODEL, tf), lambda i, j: (0, j + nf)),
           pl.BlockSpec((tf, D_MODEL), lambda i, j: (j, 0)), vec, vec],
        out_specs=[pl.BlockSpec((tm, D_MODEL), row), pl.BlockSpec((tm, D_MODEL), row)],
        out_shape=[jax.ShapeDtypeStruct((nt * tm, D_MODEL), f32), jax.ShapeDtypeStruct((nt * tm, D_MODEL), bf16)],
        scratch_shapes=[pltpu.VMEM((tm, D_MODEL), bf16), pltpu.VMEM((tm, D_MODEL), f32)],
        compiler_params=_params("parallel", "arbitrary"),
        name="ffn",
    )(*xs, gpre, w_gu, w_gu, w_down, gpost, gnext)


CAST_TN = 512
W_IN_SHIFT = O_GLU - O_BETA


def _cast_w_in_kernel(a_ref, b0_ref, b1_ref, oa_ref, ob_ref):
    oa_ref[...] = a_ref[...].astype(bf16)
    keep = CAST_TN - W_IN_SHIFT
    lane = lax.broadcasted_iota(jnp.int32, b0_ref.shape, 1)
    moved = jnp.where(lane < keep, pltpu.roll(b0_ref[...], keep, 1), pltpu.roll(b1_ref[...], keep, 1))
    ob_ref[...] = moved.astype(bf16)


def _cast_w_in(w_in):
    n_tail = w_in.shape[1] - O_GLU
    assert O_BETA % CAST_TN == 0 and n_tail == O_BETA and 0 < W_IN_SHIFT < CAST_TN
    nb = O_BETA // CAST_TN
    blk = lambda off: pl.BlockSpec((D_MODEL, CAST_TN), lambda j: (0, j + off))
    return pl.pallas_call(
        _cast_w_in_kernel,
        grid=(nb,),
        in_specs=[blk(0), blk(nb), blk(nb + 1)],
        out_specs=[blk(0), blk(0)],
        out_shape=[jax.ShapeDtypeStruct((D_MODEL, O_BETA), bf16), jax.ShapeDtypeStruct((D_MODEL, n_tail), bf16)],
        compiler_params=_params("parallel"),
        name="cast_w_in",
    )(w_in, w_in, w_in)


def _proj_kernel(x_ref, w_ref, o_ref, *, act):
    y = _dot(x_ref[...], w_ref[...])
    if act == "sigmoid":
        y = _sigmoid(y)
    o_ref[...] = y.astype(o_ref.dtype)


def _proj(x, w, n0, n, act, out_dtype, r0=0, rows=None, tm=1024, tn=1024):
    t, k = x.shape
    rows = t if rows is None else rows
    i0, j0 = r0 // tm, n0 // tn
    return pl.pallas_call(
        functools.partial(_proj_kernel, act=act),
        grid=(rows // tm, n // tn),
        in_specs=[pl.BlockSpec((tm, k), lambda i, j: (i + i0, 0)),
                  pl.BlockSpec((k, tn), lambda i, j: (0, j + j0))],
        out_specs=pl.BlockSpec((tm, tn), lambda i, j: (i, j)),
        out_shape=jax.ShapeDtypeStruct((rows, n), out_dtype),
        compiler_params=_params("parallel", "arbitrary"),
        name="proj_" + act,
    )(x, w)


def _glu_kernel(x_ref, wa_ref, wb_ref, o_ref):
    x = x_ref[...]
    o_ref[...] = _dot(x, wa_ref[...]) * _sigmoid(_dot(x, wb_ref[...]))


def _proj_glu(x, w, tm=1024, tn=512):
    t, k = x.shape
    nj = CONV_CH // tn
    return pl.pallas_call(
        _glu_kernel,
        grid=(t // tm, nj),
        in_specs=[pl.BlockSpec((tm, k), lambda i, j: (i, 0)),
                  pl.BlockSpec((k, tn), lambda i, j: (0, j)),
                  pl.BlockSpec((k, tn), lambda i, j: (0, j + nj))],
        out_specs=pl.BlockSpec((tm, tn), lambda i, j: (i, j)),
        out_shape=jax.ShapeDtypeStruct((t, CONV_CH), f32),
        compiler_params=_params("parallel", "arbitrary"),
        name="proj_glu",
    )(x, w, w)


def _ba_kernel(x_ref, w_ref, alog_ref, dtb_ref, o_ref):
    y = _dot(x_ref[...], w_ref[...])
    lane = lax.broadcasted_iota(jnp.int32, y.shape, 1)
    ya = y + dtb_ref[...]
    softplus = jnp.maximum(ya, 0.0) + jnp.log(1.0 + jnp.exp(-jnp.abs(ya)))
    o_ref[...] = jnp.where(lane < HG, _sigmoid(y), -jnp.exp(alog_ref[...]) * softplus)


def _proj_ba(x, w_ba, alog_row, dtb_row, tm=1024):
    t, k = x.shape
    vec = pl.BlockSpec((1, 128), lambda i, j: (0, j))
    return pl.pallas_call(
        _ba_kernel,
        grid=(t // tm, NHG),
        in_specs=[pl.BlockSpec((tm, k), lambda i, j: (i, 0)),
                  pl.BlockSpec((k, 128), lambda i, j: (0, j)), vec, vec],
        out_specs=pl.BlockSpec((tm, 128), lambda i, j: (i, j)),
        out_shape=jax.ShapeDtypeStruct((t, NHG * 128), f32),
        compiler_params=_params("parallel", "arbitrary"),
        name="proj_ba",
    )(x, w_ba, alog_row, dtb_row)


def _group_lanes(v):
    r = jnp.zeros((NHG, 128), f32).at[:, HG:2 * HG].set(v.astype(f32).reshape(NHG, HG))
    return r.reshape(1, NHG * 128)


def _ba_weight(w_in):
    wb = w_in[:, O_BETA:O_A].reshape(D_MODEL, NHG, HG)
    wa = w_in[:, O_A:O_GLU].reshape(D_MODEL, NHG, HG)
    w = jnp.zeros((D_MODEL, NHG, 128), w_in.dtype).at[:, :, :HG].set(wb).at[:, :, HG:2 * HG].set(wa)
    return w.reshape(D_MODEL, NHG * 128)


def _seq_masks(n, seg):
    r = lax.broadcasted_iota(jnp.int32, (n, n), 0)
    c = lax.broadcasted_iota(jnp.int32, (n, n), 1)
    same = (r // seg) == (c // seg)
    return same & (r >= c), same & (r > c), same


def _mask_dot(mask, x):
    l = jnp.where(mask, 1.0, 0.0).astype(bf16)
    hi = x.astype(bf16)
    r1 = x - hi.astype(f32)
    mid = r1.astype(bf16)
    lo = (r1 - mid.astype(f32)).astype(bf16)
    return _dot(l, hi) + _dot(l, mid) + _dot(l, lo)


def _mm(a, b):
    return _dot(a.astype(bf16), b.astype(bf16))


def _unit_lower_solve(ms, rhss, seg):
    n = range(len(ms))
    r = lax.broadcasted_iota(jnp.int32, (CHUNK, CHUNK), 0)
    c = lax.broadcasted_iota(jnp.int32, (CHUNK, CHUNK), 1)
    eye = (r == c).astype(f32)
    if seg <= 8:
        d, lo = ms, None
    else:
        blk = (r // 16) == (c // 16)
        d = [jnp.where(blk, m, 0.0) for m in ms]
        lo = [ms[i] - d[i] for i in n]
    d2 = [_mm(d[i], d[i]) for i in n]
    d3 = [_mm(d[i], d2[i]) for i in n]
    d4 = [_mm(d2[i], d2[i]) for i in n]
    a = [eye - d[i] + d2[i] - d3[i] for i in n]
    td = [a[i] + _mm(a[i], d4[i]) for i in n]
    if seg > 8:
        d8 = [_mm(d4[i], d4[i]) for i in n]
        td = [td[i] + _mm(td[i], d8[i]) for i in n]
    y = [_mm(td[i], rhss[i]) for i in n]
    if lo is None:
        return y
    e = [_mm(td[i], lo[i]) for i in n]
    e2 = [_mm(e[i], e[i]) for i in n]
    w = [y[i] - _mm(e[i], y[i]) for i in n]
    return [w[i] + _mm(e2[i], w[i]) for i in n]


def _chunk_local(qs, ks, vs, betas, gcols, grows, gtots, incl, strict, seg):
    n = range(len(qs))
    qn = [q * lax.rsqrt(jnp.sum(q * q, axis=-1, keepdims=True) + EPS) * (HEAD_K ** -0.5) for q in qs]
    kn = [k * lax.rsqrt(jnp.sum(k * k, axis=-1, keepdims=True) + EPS) for k in ks]
    decay = [jnp.exp(jnp.where(incl, gcols[i] - grows[i], -jnp.inf)) for i in n]
    kb = [k.astype(bf16) for k in kn]
    kk = [_dot_nt(kb[i], kb[i]) for i in n]
    qk = [_dot_nt(qn[i].astype(bf16), kb[i]) for i in n]
    m = [jnp.where(strict, betas[i] * decay[i] * kk[i], 0.0) for i in n]
    gamma = [jnp.exp(g) for g in gcols]
    rhs = [jnp.concatenate([betas[i] * vs[i], (betas[i] * gamma[i]) * kn[i]], axis=1) for i in n]
    sol = _unit_lower_solve(m, rhs, seg)
    uv = [s[:, :HEAD_V] for s in sol]
    wk = [s[:, HEAD_V:] for s in sol]
    qkd = [qk[i] * decay[i] for i in n]
    qg = [qn[i] * gamma[i] for i in n]
    kd = [kn[i] * jnp.exp(gtots[i] - gcols[i]) for i in n]
    return uv, wk, qkd, qg, kd


def _out_gate(o, onorm, z):
    return (_rms(o, onorm) * _silu(z)).astype(bf16)


def _gdn_prompt_kernel(q_ref, k_ref, v_ref, z_ref, ba_ref, wq_ref, wk_ref, wv_ref, onorm_ref,
                       oa_ref, s_ref,
                       eq, ek, ev, s_scr, qc_s, kc_s, vc_s, gc_s, gt_s, gl_s, uv_s, wq_s, kdt_s, qkd_s,
                       *, rows):
    t = pl.program_id(2)
    nc = rows // CHUNK
    heads = range(HG)
    hs = [slice(h * HEAD_K, (h + 1) * HEAD_K) for h in heads]
    gs = [slice(HG + h, HG + h + 1) for h in heads]

    @pl.when(t == 0)
    def _():
        s_scr[...] = jnp.zeros_like(s_scr)
        for e in (eq, ek, ev):
            e[:, 0:8, :] = jnp.zeros((HG, 8, HEAD_K), f32)

    off = 8 - (SHORT_CONV - 1)
    for e, x_ref, w_ref, dst in ((eq, q_ref, wq_ref, qc_s), (ek, k_ref, wk_ref, kc_s), (ev, v_ref, wv_ref, vc_s)):
        for h in heads:
            e[h, 8:, :] = x_ref[:, hs[h]]
            for r0 in range(0, rows, CHUNK):
                acc = w_ref[0:1, hs[h]] * e[h, r0 + off:r0 + off + CHUNK, :]
                for j in range(1, SHORT_CONV):
                    acc += w_ref[j:j + 1, hs[h]] * e[h, r0 + off + j:r0 + off + j + CHUNK, :]
                dst[r0:r0 + CHUNK, hs[h]] = _silu(acc)

    incl_b, _, same_b = _seq_masks(rows, CHUNK)
    ba = ba_ref[...]
    gc_s[...] = _mask_dot(incl_b, ba)
    gtot_b = _mask_dot(same_b, ba)
    gt_s[...] = gtot_b
    gl_s[...] = jnp.exp(gtot_b)
    incl, strict, _ = _seq_masks(CHUNK, CHUNK)

    def local(cp, carry):
        cs = [cp * LOCAL_CHUNKS + i for i in range(LOCAL_CHUNKS)]
        rss = [pl.ds(pl.multiple_of(c * CHUNK, CHUNK), CHUNK) for c in cs]
        units = [(i, h) for i in range(LOCAL_CHUNKS) for h in heads]
        bac = [ba_ref[rs, :] for rs in rss]
        gc = [gc_s[rs, :] for rs in rss]
        gtot = [gt_s[rs, :] for rs in rss]
        gct = [g.T for g in gc]
        uv, wk, qkd, qg, kd = _chunk_local(
            [qc_s[rss[i], hs[h]] for i, h in units], [kc_s[rss[i], hs[h]] for i, h in units],
            [vc_s[rss[i], hs[h]] for i, h in units], [bac[i][:, h:h + 1] for i, h in units],
            [gc[i][:, gs[h]] for i, h in units], [gct[i][gs[h], :] for i, h in units],
            [gtot[i][:, gs[h]] for i, h in units], incl, strict, CHUNK)
        for n, (i, h) in enumerate(units):
            uv_s[rss[i], hs[h]] = uv[n]
            wq_s[cs[i], 0:CHUNK, hs[h]] = wk[n].astype(bf16)
            wq_s[cs[i], CHUNK:2 * CHUNK, hs[h]] = qg[n].astype(bf16)
            kdt_s[cs[i], h] = kd[n].T.astype(bf16)
            qkd_s[cs[i], h] = qkd[n].astype(bf16)
        return carry

    lax.fori_loop(0, nc // LOCAL_CHUNKS, local, 0)

    def recur(c, carry):
        r0 = pl.multiple_of(c * CHUNK, CHUNK)
        rs = pl.ds(r0, CHUNK)
        s = [s_scr[h] for h in heads]
        p = [_dot(wq_s[c, :, hs[h]], s[h].astype(bf16)) for h in heads]
        u = [(uv_s[rs, hs[h]] - p[h][:CHUNK]).astype(bf16) for h in heads]
        o = [p[h][CHUNK:] + _dot(qkd_s[c, h], u[h]) for h in heads]
        ds = [_dot(kdt_s[c, h], u[h]) for h in heads]
        for h in heads:
            s_scr[h] = gl_s[pl.ds(r0, 1), gs[h]] * s[h] + ds[h]
            oa_ref[rs, hs[h]] = _out_gate(o[h], onorm_ref[...], z_ref[rs, hs[h]])
        return carry

    lax.fori_loop(0, nc, recur, 0)

    for e in (eq, ek, ev):
        e[:, 0:8, :] = e[:, rows:rows + 8, :]

    @pl.when(t == pl.num_programs(2) - 1)
    def _():
        s_ref[0] = s_scr[...]


def _gdn_prompt(qkvz, ba, w_conv, onorm, nb, seq, rows=256):
    nt = seq // rows
    nc = rows // CHUNK
    rb = lambda b, hg, t: b * nt + t
    col = lambda off: pl.BlockSpec((rows, HW), lambda b, hg, t: (rb(b, hg, t), off * NHG + hg))
    wcol = lambda off: pl.BlockSpec((SHORT_CONV, HW), lambda b, hg, t: (0, off * NHG + hg))
    return pl.pallas_call(
        functools.partial(_gdn_prompt_kernel, rows=rows),
        grid=(nb, NHG, nt),
        in_specs=[col(0), col(1), col(2), col(3),
                  pl.BlockSpec((rows, 128), lambda b, hg, t: (rb(b, hg, t), hg)),
                  wcol(0), wcol(1), wcol(2),
                  pl.BlockSpec((1, HEAD_V), lambda b, hg, t: (0, 0))],
        out_specs=[pl.BlockSpec((rows, HW), lambda b, hg, t: (rb(b, hg, t), hg)),
                   pl.BlockSpec((1, HG, HEAD_K, HEAD_V), lambda b, hg, t: (b, hg, 0, 0))],
        out_shape=[jax.ShapeDtypeStruct((nb * seq, VAL_DIM), bf16),
                   jax.ShapeDtypeStruct((nb, N_HEADS, HEAD_K, HEAD_V), f32)],
        scratch_shapes=[pltpu.VMEM((HG, rows + 8, HEAD_K), f32)] * 3
        + [pltpu.VMEM((HG, HEAD_K, HEAD_V), f32)]
        + [pltpu.VMEM((rows, HW), f32)] * 3
        + [pltpu.VMEM((rows, 128), f32)] * 3
        + [pltpu.VMEM((rows, HW), f32),
           pltpu.VMEM((nc, 2 * CHUNK, HW), bf16),
           pltpu.VMEM((nc, HG, HEAD_K, CHUNK), bf16),
           pltpu.VMEM((nc, HG, CHUNK, CHUNK), bf16)],
        compiler_params=_params("parallel", "parallel", "arbitrary"),
        name="gdn_prompt",
    )(qkvz, qkvz, qkvz, qkvz, ba, w_conv, w_conv, w_conv, onorm)


def _gdn_sample_kernel(q_ref, k_ref, v_ref, qst_ref, kst_ref, vst_ref, z_ref, ba_ref, wq_ref, wk_ref, wv_ref,
                       onorm_ref, s0_ref, oa_ref, s_ref, e_scr):
    nseq = CHUNK // SEG_S
    heads = range(HG)
    seqs = range(nseq)
    hs = [slice(h * HEAD_K, (h + 1) * HEAD_K) for h in heads]
    gs = [slice(HG + h, HG + h + 1) for h in heads]
    js = [slice(j * SEG_S, (j + 1) * SEG_S) for j in seqs]
    incl, strict, same = _seq_masks(CHUNK, SEG_S)
    hist = SHORT_CONV - 1

    def conv(x_ref, st_ref, w_ref):
        e_scr[:, SEG_S - hist:SEG_S, :] = st_ref[...]
        e_scr[:, SEG_S:, :] = x_ref[...].reshape(nseq, SEG_S, HW)
        off = SEG_S - hist
        acc = w_ref[0:1, :] * e_scr[:, off:off + SEG_S, :].reshape(CHUNK, HW)
        for j in range(1, SHORT_CONV):
            acc += w_ref[j:j + 1, :] * e_scr[:, off + j:off + j + SEG_S, :].reshape(CHUNK, HW)
        return _silu(acc)

    qc, kc, vc = conv(q_ref, qst_ref, wq_ref), conv(k_ref, kst_ref, wk_ref), conv(v_ref, vst_ref, wv_ref)
    bac = ba_ref[...]
    gc = _mask_dot(incl, bac)
    gtot = _mask_dot(same, bac)
    gct = gc.T
    glv = jnp.exp(gtot)
    uv, wk, qkd, qg, kd = _chunk_local(
        [qc[:, hs[h]] for h in heads], [kc[:, hs[h]] for h in heads], [vc[:, hs[h]] for h in heads],
        [bac[:, h:h + 1] for h in heads], [gc[:, gs[h]] for h in heads], [gct[gs[h], :] for h in heads],
        [gtot[:, gs[h]] for h in heads], incl, strict, SEG_S)
    kdt = [kd[h].T.astype(bf16) for h in heads]
    p = [[_dot(jnp.concatenate([wk[h][js[j]], qg[h][js[j]]], axis=0).astype(bf16),
               s0_ref[j, h].astype(bf16)) for j in seqs] for h in heads]
    u = [[uv[h][js[j]] - p[h][j][:SEG_S] for j in seqs] for h in heads]
    zeros = jnp.zeros((SEG_S, HEAD_V), f32)
    for h in heads:
        for j in seqs:
            u_rows = jnp.concatenate([u[h][j] if i == j else zeros for i in seqs], axis=0).astype(bf16)
            s_ref[j, h] = glv[j * SEG_S:j * SEG_S + 1, gs[h]] * s0_ref[j, h] + _dot(kdt[h], u_rows)
    for h in heads:
        u_all = jnp.concatenate(u[h], axis=0).astype(bf16)
        o = jnp.concatenate([p[h][j][SEG_S:] for j in seqs], axis=0) + _dot(qkd[h].astype(bf16), u_all)
        oa_ref[:, hs[h]] = _out_gate(o, onorm_ref[...], z_ref[:, hs[h]])


def _gdn_sample(qkvz, qkv_state, ba, w_conv, onorm, s0, row0):
    nb = qkv_state.shape[0]
    nseq = CHUNK // SEG_S
    rb0 = row0 // CHUNK
    col = lambda off: pl.BlockSpec((CHUNK, HW), lambda g, hg: (rb0 + g, off * NHG + hg))
    st = lambda off: pl.BlockSpec((nseq, SHORT_CONV - 1, HW), lambda g, hg: (g, 0, off * NHG + hg))
    wcol = lambda off: pl.BlockSpec((SHORT_CONV, HW), lambda g, hg: (0, off * NHG + hg))
    sspec = pl.BlockSpec((nseq, HG, HEAD_K, HEAD_V), lambda g, hg: (g, hg, 0, 0))
    return pl.pallas_call(
        _gdn_sample_kernel,
        grid=(nb // nseq, NHG),
        in_specs=[col(0), col(1), col(2), st(0), st(1), st(2), col(3),
                  pl.BlockSpec((CHUNK, 128), lambda g, hg: (rb0 + g, hg)),
                  wcol(0), wcol(1), wcol(2),
                  pl.BlockSpec((1, HEAD_V), lambda g, hg: (0, 0)),
                  sspec],
        out_specs=[pl.BlockSpec((CHUNK, HW), lambda g, hg: (g, hg)), sspec],
        out_shape=[jax.ShapeDtypeStruct((nb * SEG_S, VAL_DIM), bf16), jax.ShapeDtypeStruct(s0.shape, f32)],
        scratch_shapes=[pltpu.VMEM((nseq, 2 * SEG_S, HW), f32)],
        compiler_params=_params("parallel", "parallel"),
        name="gdn_sample",
    )(qkvz, qkvz, qkvz, qkv_state, qkv_state, qkv_state, qkvz, ba, w_conv, w_conv, w_conv, onorm, s0)


CONV_LC = 128
CONV_RB = 64


def _ln_silu(c, g, b):
    mu = jnp.mean(c, axis=-1, keepdims=True)
    xc = c - mu
    y = xc * lax.rsqrt(jnp.mean(xc * xc, axis=-1, keepdims=True) + EPS) * g + b
    return _silu(y).astype(bf16)


CONV_PAD = 32


def _gates_conv_kernel(u_ref, wg_ref, x_ref, cw_ref, cb_ref, g_ref, c_ref, ext, *, tiles_per_seq):
    j, i = pl.program_id(0), pl.program_id(1)
    tm, cw = x_ref.shape

    @pl.when((i == 0) & (j == 0))
    def _():
        ext[:, tm:, :] = jnp.zeros((cw // CONV_LC, CONV_PAD, CONV_LC), f32)

    first = (i % tiles_per_seq) == 0
    off = CONV_PAD - (CONV_WIDTH - 1)
    nlc = cw // CONV_LC
    gw = g_ref.shape[1] // nlc
    for lc in range(nlc):
        gs = slice(lc * gw, (lc + 1) * gw)
        g_ref[:, gs] = _sigmoid(_dot(u_ref[...], wg_ref[:, gs])).astype(g_ref.dtype)
        ls = slice(lc * CONV_LC, (lc + 1) * CONV_LC)
        ext[lc, 0:CONV_PAD, :] = jnp.where(first, 0.0, ext[lc, tm:tm + CONV_PAD, :])
        ext[lc, CONV_PAD:, :] = x_ref[:, ls]
        for r0 in range(0, tm, CONV_RB):
            acc = jnp.broadcast_to(cb_ref[:, ls], (CONV_RB, CONV_LC))
            for t in range(CONV_WIDTH):
                acc += cw_ref[t:t + 1, ls] * ext[lc, r0 + off + t:r0 + off + t + CONV_RB, :]
            c_ref[r0:r0 + CONV_RB, ls] = acc


def _gates_conv_prompt(u, w_tail, glu, cw, cb, nb, seq, tm=1024, tn=1024):
    k = u.shape[1]
    ng = 2 * D_MODEL // tn
    cwid = CONV_CH // ng
    j0 = (O_GATE - O_GLU) // tn
    return pl.pallas_call(
        functools.partial(_gates_conv_kernel, tiles_per_seq=seq // tm),
        grid=(ng, nb * seq // tm),
        in_specs=[pl.BlockSpec((tm, k), lambda j, i: (i, 0)),
                  pl.BlockSpec((k, tn), lambda j, i: (0, j + j0)),
                  pl.BlockSpec((tm, cwid), lambda j, i: (i, j)),
                  pl.BlockSpec((CONV_WIDTH, cwid), lambda j, i: (0, j)),
                  pl.BlockSpec((1, cwid), lambda j, i: (0, j))],
        out_specs=[pl.BlockSpec((tm, tn), lambda j, i: (i, j)), pl.BlockSpec((tm, cwid), lambda j, i: (i, j))],
        out_shape=[jax.ShapeDtypeStruct((nb * seq, 2 * D_MODEL), bf16),
                   jax.ShapeDtypeStruct((nb * seq, CONV_CH), f32)],
        scratch_shapes=[pltpu.VMEM((cwid // CONV_LC, tm + CONV_PAD, CONV_LC), f32)],
        compiler_params=_params("arbitrary", "arbitrary"),
        name="gates_conv",
    )(u, w_tail, glu, cw, cb)


def _cconv_sample_kernel(x_ref, st_ref, w_ref, b_ref, g_ref, beta_ref, o_ref, e_scr, cbuf):
    nseq = CHUNK // SEG_S
    hist = CONV_WIDTH - 1
    e_scr[:, 0:hist, :] = st_ref[...]
    e_scr[:, hist:hist + SEG_S, :] = x_ref[...].reshape(nseq, SEG_S, CONV_CH)
    for lc in range(CONV_CH // CONV_LC):
        ls = slice(lc * CONV_LC, (lc + 1) * CONV_LC)
        acc = jnp.broadcast_to(b_ref[:, ls], (CHUNK, CONV_LC))
        for j in range(CONV_WIDTH):
            acc += w_ref[j:j + 1, ls] * e_scr[:, j:j + SEG_S, ls].reshape(CHUNK, CONV_LC)
        cbuf[:, ls] = acc
    o_ref[...] = _ln_silu(cbuf[...], g_ref[...], beta_ref[...])


def _cconv_sample(glu, glu_state, w, b, g, beta, row0):
    nb, hist, _ = glu_state.shape
    nseq = CHUNK // SEG_S
    rb0 = row0 // CHUNK
    vec = pl.BlockSpec((1, CONV_CH), lambda i: (0, 0))
    return pl.pallas_call(
        _cconv_sample_kernel,
        grid=(nb // nseq,),
        in_specs=[pl.BlockSpec((CHUNK, CONV_CH), lambda i: (rb0 + i, 0)),
                  pl.BlockSpec((nseq, hist, CONV_CH), lambda i: (i, 0, 0)),
                  pl.BlockSpec((CONV_WIDTH, CONV_CH), lambda i: (0, 0)), vec, vec, vec],
        out_specs=pl.BlockSpec((CHUNK, CONV_CH), lambda i: (i, 0)),
        out_shape=jax.ShapeDtypeStruct((nb * SEG_S, CONV_CH), bf16),
        scratch_shapes=[pltpu.VMEM((nseq, hist + SEG_S, CONV_CH), f32), pltpu.VMEM((CHUNK, CONV_CH), f32)],
        compiler_params=_params("parallel"),
        name="cconv_sample",
    )(glu, glu_state, w, b, g, beta)


def _merge_kernel(oap_ref, oas_ref, cp_ref, cs_ref, gap_ref, gas_ref, gbp_ref, gbs_ref, h_ref,
                  wa_ref, wb_ref, wo_ref, lng_ref, lnb_ref, gpost_ref, o_ref, *, na):
    def run(oa_ref, c, ga_ref, gb_ref):
        merged = (ga_ref[...].astype(f32) * _dot(oa_ref[...], wa_ref[...])
                  + gb_ref[...].astype(f32) * _dot(c, wb_ref[...]))
        o_ref[...] = h_ref[...] + _rms(_dot(merged.astype(bf16), wo_ref[...]), gpost_ref[...])

    i = pl.program_id(0)
    pl.when(i < na)(lambda: run(oap_ref, _ln_silu(cp_ref[...], lng_ref[...], lnb_ref[...]), gap_ref, gbp_ref))
    pl.when(i >= na)(lambda: run(oas_ref, cs_ref[...], gas_ref, gbs_ref))


def _merge(oa_p, oa_s, c_p, c_s, gates_p, gates_s, h, wa, wb, wo, ln_g, ln_b, gpost, tm=256):
    t = h.shape[0]
    na = oa_p.shape[0] // tm
    act = pl.BlockSpec((tm, D_MODEL), lambda i: (i, 0))
    vec = pl.BlockSpec((1, D_MODEL), lambda i: (0, 0))
    src = lambda col: _split_specs((tm, D_MODEL), na, 2, col)
    wsp = pl.BlockSpec((D_MODEL, D_MODEL), lambda i: (0, 0), pipeline_mode=pl.Buffered(1))
    return pl.pallas_call(
        functools.partial(_merge_kernel, na=na),
        grid=(t // tm,),
        in_specs=src(0) + src(0) + src(0) + src(1) + [act, wsp, wsp, wsp, vec, vec, vec],
        out_specs=act,
        out_shape=jax.ShapeDtypeStruct((t, D_MODEL), f32),
        compiler_params=_params("parallel"),
        name="merge",
    )(oa_p, oa_s, c_p, c_s, gates_p, gates_s, gates_p, gates_s, h, wa, wb, wo, ln_g, ln_b, gpost)


def _ple_kernel(hn_ref, pa_ref, pb_ref, h_ref, wg_ref, wp_ref, gpost_ref, oa_ref, ob_ref, *, na):
    i = pl.program_id(0)
    gate = _sigmoid(_dot(hn_ref[...], wg_ref[...]))

    def out(p_ref):
        v = gate * _dot(p_ref[...].astype(bf16), wp_ref[...])
        return h_ref[...] + _rms(v, gpost_ref[...])

    @pl.when(i < na)
    def _():
        oa_ref[...] = out(pa_ref)

    @pl.when(i >= na)
    def _():
        ob_ref[...] = out(pb_ref)


def _ple(hn, pa, pb, h, wg, wp, gpost, tm=512):
    na, nb = pa.shape[0] // tm, pb.shape[0] // tm
    row = lambda i: (i, 0)
    first = lambda i: (jnp.minimum(i, na - 1), 0)
    second = lambda i: (jnp.maximum(i - na, 0), 0)
    act = pl.BlockSpec((tm, D_MODEL), row)
    return pl.pallas_call(
        functools.partial(_ple_kernel, na=na),
        grid=(na + nb,),
        in_specs=[act, pl.BlockSpec((tm, PLE_DIM), first), pl.BlockSpec((tm, PLE_DIM), second), act,
                  pl.BlockSpec((D_MODEL, D_MODEL), lambda i: (0, 0), pipeline_mode=pl.Buffered(1)),
                  pl.BlockSpec((PLE_DIM, D_MODEL), lambda i: (0, 0)),
                  pl.BlockSpec((1, D_MODEL), lambda i: (0, 0))],
        out_specs=[pl.BlockSpec((tm, D_MODEL), first), pl.BlockSpec((tm, D_MODEL), second)],
        out_shape=[jax.ShapeDtypeStruct((na * tm, D_MODEL), f32), jax.ShapeDtypeStruct((nb * tm, D_MODEL), f32)],
        compiler_params=_params("arbitrary"),
        name="ple",
    )(hn, pa, pb, h, wg, wp, gpost)


def _layer(xp, xs, pp, ps, nb, seq, s0, qkv_buf, glu_buf, w):
    (ffn1_pre, ffn1_w_gu, ffn1_w_down, ffn1_post, mix_pre, w_in, w_short_conv, a_log, dt_bias,
     o_norm, w_dw_conv, b_dw_conv, ln_g, ln_b, w_branch_a, w_branch_b, w_out, mix_post,
     ffn2_pre, ffn2_w_gu, ffn2_w_down, ffn2_post, ple_pre, w_ple_gate, w_ple_proj,
     ple_post) = w
    tp = nb * seq
    ns = s0.shape[0]
    row = lambda v: v.astype(f32).reshape(1, -1)
    cast = lambda v: v.astype(bf16)

    h1, u = _ffn([xp, xs], row(ffn1_pre), cast(ffn1_w_gu), cast(ffn1_w_down), row(ffn1_post), row(mix_pre))

    w_qkvz, w_tail = _cast_w_in(w_in)
    qkvz = _proj(u, w_qkvz, 0, O_BETA, "none", f32)
    ba = _proj_ba(u, cast(_ba_weight(w_in)), _group_lanes(a_log), _group_lanes(dt_bias))
    glu = _proj_glu(u, w_tail)

    w_conv = w_short_conv.astype(f32)
    onorm = row(o_norm)
    oa_p, s_p = _gdn_prompt(qkvz, ba, w_conv, onorm, nb, seq)
    oa_s, s_s = _gdn_sample(qkvz, qkv_buf.astype(f32), ba, w_conv, onorm, s0.astype(f32), tp)

    cw, cb, lg, lb = w_dw_conv.astype(f32), row(b_dw_conv), row(ln_g), row(ln_b)
    gates_p, c_p = _gates_conv_prompt(u, w_tail, glu, cw, cb, nb, seq)
    gates_s = _proj(u, w_tail, O_GATE - O_GLU, 2 * D_MODEL, "sigmoid", bf16, r0=tp, rows=xs.shape[0])
    c_s = _cconv_sample(glu, glu_buf.astype(f32), cw, cb, lg, lb, tp)

    h2 = _merge(oa_p, oa_s, c_p, c_s, gates_p, gates_s, h1, cast(w_branch_a), cast(w_branch_b), cast(w_out),
                lg, lb, row(mix_post))
    h3, hn = _ffn([h2], row(ffn2_pre), cast(ffn2_w_gu), cast(ffn2_w_down), row(ffn2_post), row(ple_pre))
    yp, ys = _ple(hn, pp, ps, h3, cast(w_ple_gate), cast(w_ple_proj), row(ple_post))

    tail = lambda a, width, n: jnp.stack([a[(b + 1) * seq - n:(b + 1) * seq, :width] for b in range(nb)])
    qkv_p = tail(qkvz, QKV_DIM, SHORT_CONV - 1)
    glu_p = tail(glu, CONV_CH, CONV_WIDTH - 1)
    qkv_s = qkvz[tp:, :QKV_DIM].reshape(ns, SEG_S, QKV_DIM)[:, SEG_S - (SHORT_CONV - 1):]
    glu_s = jnp.concatenate([glu_buf.astype(f32)[:, SEG_S:], glu[tp:].reshape(ns, SEG_S, CONV_CH)], axis=1)
    return (yp, ys, s_p, qkv_p, glu_p, s_s, qkv_s, glu_s)


def kernel(x_prompt, x_sample, p_prompt, p_sample, state_delta, state_qkv_conv, state_glu_conv, ffn1_pre, ffn1_w_gu, ffn1_w_down, ffn1_post, mix_pre, w_in, w_short_conv, a_log, dt_bias, o_norm, w_dw_conv, b_dw_conv, ln_g, ln_b, w_branch_a, w_branch_b, w_out, mix_post, ffn2_pre, ffn2_w_gu, ffn2_w_down, ffn2_post, ple_pre, w_ple_gate, w_ple_proj, ple_post):
    weights = (ffn1_pre, ffn1_w_gu, ffn1_w_down, ffn1_post, mix_pre, w_in, w_short_conv, a_log,
               dt_bias, o_norm, w_dw_conv, b_dw_conv, ln_g, ln_b, w_branch_a, w_branch_b,
               w_out, mix_post, ffn2_pre, ffn2_w_gu, ffn2_w_down, ffn2_post, ple_pre,
               w_ple_gate, w_ple_proj, ple_post)
    nb, seq, _ = x_prompt.shape
    ns, ls, _ = x_sample.shape
    assert ls == SEG_S and seq % CHUNK == 0 and ns % (CHUNK // SEG_S) == 0
    depth = ffn1_pre.shape[0]
    tp = nb * seq
    xp, xs = x_prompt.reshape(tp, D_MODEL), x_sample.reshape(ns * ls, D_MODEL)
    outs = [[] for _ in range(6)]
    for i in range(depth):
        wi = tuple(wt[i] for wt in weights)
        xp, xs, s_p, q_p, g_p, s_s, q_s, g_s = _layer(
            xp, xs, p_prompt[i].reshape(tp, PLE_DIM), p_sample[i].reshape(ns * ls, PLE_DIM), nb, seq,
            state_delta[i], state_qkv_conv[i], state_glu_conv[i], wi)
        for lst, v in zip(outs, (s_p, q_p, g_p, s_s, q_s, g_s)):
            lst.append(v)
    return (xp.reshape(nb, seq, D_MODEL), xs.reshape(ns, ls, D_MODEL)) + tuple(jnp.stack(lst) for lst in outs)
```

```python
import functools

import jax
import jax.numpy as jnp
from jax import lax
from jax.experimental import pallas as pl
from jax.experimental.pallas import tpu as pltpu

f32 = jnp.float32
bf16 = jnp.bfloat16

D_MODEL = 2048
N_HEADS = 16
HEAD_K = 128
HEAD_V = 128
KEY_DIM = N_HEADS * HEAD_K
VAL_DIM = N_HEADS * HEAD_V
QKV_DIM = 2 * KEY_DIM + VAL_DIM
SHORT_CONV = 4
CONV_CH = D_MODEL
CONV_WIDTH = 31
D_FF = 5632
PLE_DIM = 256
EPS = 1e-6

O_Z = QKV_DIM
O_BETA = O_Z + VAL_DIM
O_A = O_BETA + N_HEADS
O_GLU = O_A + N_HEADS
O_GATE = O_GLU + 2 * CONV_CH

CHUNK = 64
HG = 8
NHG = N_HEADS // HG
HW = HG * HEAD_K
SEG_S = 8
LOCAL_CHUNKS = 4

VMEM_LIMIT = 56 * 1024 * 1024


def _sigmoid(x):
    return 1.0 / (1.0 + jnp.exp(-x))


def _silu(x):
    return x * _sigmoid(x)


def _rms(x, g):
    return x * lax.rsqrt(jnp.mean(x * x, axis=-1, keepdims=True) + EPS) * g


def _dot(a, b):
    return jnp.dot(a, b, preferred_element_type=f32)


def _dot_nt(a, b):
    return lax.dot_general(a, b, (((1,), (1,)), ((), ())), preferred_element_type=f32)


def _params(*sem):
    return pltpu.CompilerParams(dimension_semantics=sem, vmem_limit_bytes=VMEM_LIMIT)


def _per_source(i, na, refs, fn):
    if len(refs) == 1:
        fn(refs[0])
    else:
        pl.when(i < na)(lambda: fn(refs[0]))
        pl.when(i >= na)(lambda: fn(refs[1]))


def _split_specs(block, na, nsrc, col=0):
    if nsrc == 1:
        return [pl.BlockSpec(block, lambda i, *_: (i, col))]
    return [pl.BlockSpec(block, lambda i, *_: (jnp.minimum(i, na - 1), col)),
            pl.BlockSpec(block, lambda i, *_: (jnp.maximum(i - na, 0), col))]


def _ffn_kernel(*refs, na, nsrc):
    x_refs = refs[:nsrc]
    gpre_ref, wg_ref, wu_ref, wd_ref, gpost_ref, gnext_ref, h_ref, hn_ref, xn_scr, acc_scr = refs[nsrc:]
    i, j = pl.program_id(0), pl.program_id(1)

    @pl.when(j == 0)
    def _():
        def norm_in(x_ref):
            xn_scr[...] = _rms(x_ref[...], gpre_ref[...]).astype(bf16)

        _per_source(i, na, x_refs, norm_in)
        acc_scr[...] = jnp.zeros_like(acc_scr)

    xn = xn_scr[...]
    a = _silu(_dot(xn, wg_ref[...])) * _dot(xn, wu_ref[...])
    acc_scr[...] += _dot(a.astype(bf16), wd_ref[...])

    @pl.when(j == pl.num_programs(1) - 1)
    def _():
        def residual_out(x_ref):
            h = x_ref[...] + 0.5 * _rms(acc_scr[...], gpost_ref[...])
            h_ref[...] = h
            hn_ref[...] = _rms(h, gnext_ref[...]).astype(bf16)

        _per_source(i, na, x_refs, residual_out)


def _ffn(xs, gpre, w_gu, w_down, gpost, gnext, tm=512, tf=512):
    na = xs[0].shape[0] // tm
    nt = sum(x.shape[0] for x in xs) // tm
    nf = D_FF // tf
    row = lambda i, j: (i, 0)
    vec = pl.BlockSpec((1, D_MODEL), lambda i, j: (0, 0))
    return pl.pallas_call(
        functools.partial(_ffn_kernel, na=na, nsrc=len(xs)),
        grid=(nt, nf),
        in_specs=_split_specs((tm, D_MODEL), na, len(xs))
        + [vec,
           pl.BlockSpec((D_MODEL, tf), lambda i, j: (0, j)),
           pl.BlockSpec((D_MODEL, tf), lambda i, j: (0, j + nf)),
           pl.BlockSpec((tf, D_MODEL), lambda i, j: (j, 0)), vec, vec],
        out_specs=[pl.BlockSpec((tm, D_MODEL), row), pl.BlockSpec((tm, D_MODEL), row)],
        out_shape=[jax.ShapeDtypeStruct((nt * tm, D_MODEL), f32), jax.ShapeDtypeStruct((nt * tm, D_MODEL), bf16)],
        scratch_shapes=[pltpu.VMEM((tm, D_MODEL), bf16), pltpu.VMEM((tm, D_MODEL), f32)],
        compiler_params=_params("parallel", "arbitrary"),
        name="ffn",
    )(*xs, gpre, w_gu, w_gu, w_down, gpost, gnext)


CAST_TN = 512
W_IN_SHIFT = O_GLU - O_BETA


def _cast_w_in_kernel(a_ref, b0_ref, b1_ref, oa_ref, ob_ref, oc_ref):
    oa_ref[...] = a_ref[...].astype(bf16)
    ob_ref[...] = jnp.concatenate([b0_ref[W_IN_SHIFT:, :], b1_ref[:W_IN_SHIFT, :]], axis=0).astype(bf16)

    @pl.when(pl.program_id(0) == 0)
    def _():
        oc_ref[...] = b0_ref[:W_IN_SHIFT, :]


def _cast_w_in(w_in_t):
    n_tail = w_in_t.shape[1] - O_GLU
    assert O_BETA % CAST_TN == 0 and n_tail == O_BETA and 0 < W_IN_SHIFT < CAST_TN and W_IN_SHIFT % 16 == 0
    nb = O_BETA // CAST_TN
    src = lambda off: pl.BlockSpec((None, CAST_TN, D_MODEL), lambda j: (0, j + off, 0))
    blk = pl.BlockSpec((CAST_TN, D_MODEL), lambda j: (j, 0))
    return pl.pallas_call(
        _cast_w_in_kernel,
        grid=(nb,),
        in_specs=[src(0), src(nb), src(nb + 1)],
        out_specs=[blk, blk, pl.BlockSpec((W_IN_SHIFT, D_MODEL), lambda j: (0, 0))],
        out_shape=[jax.ShapeDtypeStruct((O_BETA, D_MODEL), bf16), jax.ShapeDtypeStruct((n_tail, D_MODEL), bf16),
                   jax.ShapeDtypeStruct((W_IN_SHIFT, D_MODEL), f32)],
        compiler_params=_params("arbitrary"),
        name="cast_w_in",
    )(w_in_t, w_in_t, w_in_t)


def _proj_kernel(x_ref, wt_ref, o_ref, *, act):
    y = _dot_nt(x_ref[...], wt_ref[...])
    if act == "sigmoid":
        y = _sigmoid(y)
    o_ref[...] = y.astype(o_ref.dtype)


def _proj(x, wt, n0, n, act, out_dtype, r0=0, rows=None, tm=1024, tn=1024):
    t, k = x.shape
    rows = t if rows is None else rows
    i0, j0 = r0 // tm, n0 // tn
    return pl.pallas_call(
        functools.partial(_proj_kernel, act=act),
        grid=(rows // tm, n // tn),
        in_specs=[pl.BlockSpec((tm, k), lambda i, j: (i + i0, 0)),
                  pl.BlockSpec((tn, k), lambda i, j: (j + j0, 0))],
        out_specs=pl.BlockSpec((tm, tn), lambda i, j: (i, j)),
        out_shape=jax.ShapeDtypeStruct((rows, n), out_dtype),
        compiler_params=_params("parallel", "arbitrary"),
        name="proj_" + act,
    )(x, wt)


def _glu_kernel(x_ref, wa_ref, wb_ref, o_ref):
    x = x_ref[...]
    o_ref[...] = _dot_nt(x, wa_ref[...]) * _sigmoid(_dot_nt(x, wb_ref[...]))


def _proj_glu(x, wt, tm=1024, tn=512):
    t, k = x.shape
    nj = CONV_CH // tn
    return pl.pallas_call(
        _glu_kernel,
        grid=(t // tm, nj),
        in_specs=[pl.BlockSpec((tm, k), lambda i, j: (i, 0)),
                  pl.BlockSpec((tn, k), lambda i, j: (j, 0)),
                  pl.BlockSpec((tn, k), lambda i, j: (j + nj, 0))],
        out_specs=pl.BlockSpec((tm, tn), lambda i, j: (i, j)),
        out_shape=jax.ShapeDtypeStruct((t, CONV_CH), f32),
        compiler_params=_params("parallel", "arbitrary"),
        name="proj_glu",
    )(x, wt, wt)


def _ba_kernel(x_ref, w_ref, alog_ref, dtb_ref, o_ref):
    y = _dot(x_ref[...], w_ref[...])
    lane = lax.broadcasted_iota(jnp.int32, y.shape, 1)
    ya = y + dtb_ref[...]
    softplus = jnp.maximum(ya, 0.0) + jnp.log(1.0 + jnp.exp(-jnp.abs(ya)))
    o_ref[...] = jnp.where(lane < HG, _sigmoid(y), -jnp.exp(alog_ref[...]) * softplus)


def _proj_ba(x, w_ba, alog_row, dtb_row, tm=1024):
    t, k = x.shape
    vec = pl.BlockSpec((1, 128), lambda i, j: (0, j))
    return pl.pallas_call(
        _ba_kernel,
        grid=(t // tm, NHG),
        in_specs=[pl.BlockSpec((tm, k), lambda i, j: (i, 0)),
                  pl.BlockSpec((k, 128), lambda i, j: (0, j)), vec, vec],
        out_specs=pl.BlockSpec((tm, 128), lambda i, j: (i, j)),
        out_shape=jax.ShapeDtypeStruct((t, NHG * 128), f32),
        compiler_params=_params("parallel", "arbitrary"),
        name="proj_ba",
    )(x, w_ba, alog_row, dtb_row)


def _group_lanes(v):
    r = jnp.zeros((NHG, 128), f32).at[:, HG:2 * HG].set(v.astype(f32).reshape(NHG, HG))
    return r.reshape(1, NHG * 128)


def _ba_weight(w_ba):
    wb = w_ba[:, :N_HEADS].reshape(D_MODEL, NHG, HG)
    wa = w_ba[:, N_HEADS:].reshape(D_MODEL, NHG, HG)
    w = jnp.zeros((D_MODEL, NHG, 128), w_ba.dtype).at[:, :, :HG].set(wb).at[:, :, HG:2 * HG].set(wa)
    return w.reshape(D_MODEL, NHG * 128)


def _seq_masks(n, seg):
    r = lax.broadcasted_iota(jnp.int32, (n, n), 0)
    c = lax.broadcasted_iota(jnp.int32, (n, n), 1)
    same = (r // seg) == (c // seg)
    return same & (r >= c), same & (r > c), same


def _mask_dot(mask, x):
    l = jnp.where(mask, 1.0, 0.0).astype(bf16)
    hi = x.astype(bf16)
    r1 = x - hi.astype(f32)
    mid = r1.astype(bf16)
    lo = (r1 - mid.astype(f32)).astype(bf16)
    return _dot(l, hi) + _dot(l, mid) + _dot(l, lo)


def _mm(a, b):
    return _dot(a.astype(bf16), b.astype(bf16))


def _unit_lower_solve(ms, rhss, seg):
    n = range(len(ms))
    r = lax.broadcasted_iota(jnp.int32, (CHUNK, CHUNK), 0)
    c = lax.broadcasted_iota(jnp.int32, (CHUNK, CHUNK), 1)
    eye = (r == c).astype(f32)
    if seg <= 8:
        d, lo = ms, None
    else:
        blk = (r // 16) == (c // 16)
        d = [jnp.where(blk, m, 0.0) for m in ms]
        lo = [ms[i] - d[i] for i in n]
    d2 = [_mm(d[i], d[i]) for i in n]
    d3 = [_mm(d[i], d2[i]) for i in n]
    d4 = [_mm(d2[i], d2[i]) for i in n]
    a = [eye - d[i] + d2[i] - d3[i] for i in n]
    td = [a[i] + _mm(a[i], d4[i]) for i in n]
    if seg > 8:
        d8 = [_mm(d4[i], d4[i]) for i in n]
        td = [td[i] + _mm(td[i], d8[i]) for i in n]
    y = [_mm(td[i], rhss[i]) for i in n]
    if lo is None:
        return y
    e = [_mm(td[i], lo[i]) for i in n]
    e2 = [_mm(e[i], e[i]) for i in n]
    w = [y[i] - _mm(e[i], y[i]) for i in n]
    return [w[i] + _mm(e2[i], w[i]) for i in n]


def _chunk_local(qs, ks, vs, betas, gcols, grows, gtots, incl, strict, seg):
    n = range(len(qs))
    qn = [q * lax.rsqrt(jnp.sum(q * q, axis=-1, keepdims=True) + EPS) * (HEAD_K ** -0.5) for q in qs]
    kn = [k * lax.rsqrt(jnp.sum(k * k, axis=-1, keepdims=True) + EPS) for k in ks]
    decay = [jnp.exp(jnp.where(incl, gcols[i] - grows[i], -jnp.inf)) for i in n]
    kb = [k.astype(bf16) for k in kn]
    kk = [_dot_nt(kb[i], kb[i]) for i in n]
    qk = [_dot_nt(qn[i].astype(bf16), kb[i]) for i in n]
    m = [jnp.where(strict, betas[i] * decay[i] * kk[i], 0.0) for i in n]
    gamma = [jnp.exp(g) for g in gcols]
    rhs = [jnp.concatenate([betas[i] * vs[i], (betas[i] * gamma[i]) * kn[i]], axis=1) for i in n]
    sol = _unit_lower_solve(m, rhs, seg)
    uv = [s[:, :HEAD_V] for s in sol]
    wk = [s[:, HEAD_V:] for s in sol]
    qkd = [qk[i] * decay[i] for i in n]
    qg = [qn[i] * gamma[i] for i in n]
    kd = [kn[i] * jnp.exp(gtots[i] - gcols[i]) for i in n]
    return uv, wk, qkd, qg, kd


def _out_gate(o, onorm, z):
    return (_rms(o, onorm) * _silu(z)).astype(bf16)


def _gdn_prompt_kernel(q_ref, k_ref, v_ref, z_ref, ba_ref, wq_ref, wk_ref, wv_ref, onorm_ref,
                       oa_ref, s_ref,
                       eq, ek, ev, s_scr, qc_s, kc_s, vc_s, gc_s, gt_s, gl_s, uv_s, wq_s, kdt_s, qkd_s,
                       *, rows):
    t = pl.program_id(2)
    nc = rows // CHUNK
    heads = range(HG)
    hs = [slice(h * HEAD_K, (h + 1) * HEAD_K) for h in heads]
    gs = [slice(HG + h, HG + h + 1) for h in heads]

    @pl.when(t == 0)
    def _():
        s_scr[...] = jnp.zeros_like(s_scr)
        for e in (eq, ek, ev):
            e[:, 0:8, :] = jnp.zeros((HG, 8, HEAD_K), f32)

    off = 8 - (SHORT_CONV - 1)
    for e, x_ref, w_ref, dst in ((eq, q_ref, wq_ref, qc_s), (ek, k_ref, wk_ref, kc_s), (ev, v_ref, wv_ref, vc_s)):
        for h in heads:
            e[h, 8:, :] = x_ref[:, hs[h]]
            for r0 in range(0, rows, CHUNK):
                acc = w_ref[0:1, hs[h]] * e[h, r0 + off:r0 + off + CHUNK, :]
                for j in range(1, SHORT_CONV):
                    acc += w_ref[j:j + 1, hs[h]] * e[h, r0 + off + j:r0 + off + j + CHUNK, :]
                dst[r0:r0 + CHUNK, hs[h]] = _silu(acc)

    incl_b, _, same_b = _seq_masks(rows, CHUNK)
    ba = ba_ref[...]
    gc_s[...] = _mask_dot(incl_b, ba)
    gtot_b = _mask_dot(same_b, ba)
    gt_s[...] = gtot_b
    gl_s[...] = jnp.exp(gtot_b)
    incl, strict, _ = _seq_masks(CHUNK, CHUNK)

    def local(cp, carry):
        cs = [cp * LOCAL_CHUNKS + i for i in range(LOCAL_CHUNKS)]
        rss = [pl.ds(pl.multiple_of(c * CHUNK, CHUNK), CHUNK) for c in cs]
        units = [(i, h) for i in range(LOCAL_CHUNKS) for h in heads]
        bac = [ba_ref[rs, :] for rs in rss]
        gc = [gc_s[rs, :] for rs in rss]
        gtot = [gt_s[rs, :] for rs in rss]
        gct = [g.T for g in gc]
        uv, wk, qkd, qg, kd = _chunk_local(
            [qc_s[rss[i], hs[h]] for i, h in units], [kc_s[rss[i], hs[h]] for i, h in units],
            [vc_s[rss[i], hs[h]] for i, h in units], [bac[i][:, h:h + 1] for i, h in units],
            [gc[i][:, gs[h]] for i, h in units], [gct[i][gs[h], :] for i, h in units],
            [gtot[i][:, gs[h]] for i, h in units], incl, strict, CHUNK)
        for n, (i, h) in enumerate(units):
            uv_s[rss[i], hs[h]] = uv[n]
            wq_s[cs[i], 0:CHUNK, hs[h]] = wk[n].astype(bf16)
            wq_s[cs[i], CHUNK:2 * CHUNK, hs[h]] = qg[n].astype(bf16)
            kdt_s[cs[i], h] = kd[n].T.astype(bf16)
            qkd_s[cs[i], h] = qkd[n].astype(bf16)
        return carry

    lax.fori_loop(0, nc // LOCAL_CHUNKS, local, 0)

    def recur(c, carry):
        r0 = pl.multiple_of(c * CHUNK, CHUNK)
        rs = pl.ds(r0, CHUNK)
        s = [s_scr[h] for h in heads]
        p = [_dot(wq_s[c, :, hs[h]], s[h].astype(bf16)) for h in heads]
        u = [(uv_s[rs, hs[h]] - p[h][:CHUNK]).astype(bf16) for h in heads]
        o = [p[h][CHUNK:] + _dot(qkd_s[c, h], u[h]) for h in heads]
        ds = [_dot(kdt_s[c, h], u[h]) for h in heads]
        for h in heads:
            s_scr[h] = gl_s[pl.ds(r0, 1), gs[h]] * s[h] + ds[h]
            oa_ref[rs, hs[h]] = _out_gate(o[h], onorm_ref[...], z_ref[rs, hs[h]])
        return carry

    lax.fori_loop(0, nc, recur, 0)

    for e in (eq, ek, ev):
        e[:, 0:8, :] = e[:, rows:rows + 8, :]

    @pl.when(t == pl.num_programs(2) - 1)
    def _():
        s_ref[0] = s_scr[...]


def _gdn_prompt(qkvz, ba, w_conv, onorm, nb, seq, rows=256):
    nt = seq // rows
    nc = rows // CHUNK
    rb = lambda b, hg, t: b * nt + t
    col = lambda off: pl.BlockSpec((rows, HW), lambda b, hg, t: (rb(b, hg, t), off * NHG + hg))
    wcol = lambda off: pl.BlockSpec((SHORT_CONV, HW), lambda b, hg, t: (0, off * NHG + hg))
    return pl.pallas_call(
        functools.partial(_gdn_prompt_kernel, rows=rows),
        grid=(nb, NHG, nt),
        in_specs=[col(0), col(1), col(2), col(3),
                  pl.BlockSpec((rows, 128), lambda b, hg, t: (rb(b, hg, t), hg)),
                  wcol(0), wcol(1), wcol(2),
                  pl.BlockSpec((1, HEAD_V), lambda b, hg, t: (0, 0))],
        out_specs=[pl.BlockSpec((rows, HW), lambda b, hg, t: (rb(b, hg, t), hg)),
                   pl.BlockSpec((1, HG, HEAD_K, HEAD_V), lambda b, hg, t: (b, hg, 0, 0))],
        out_shape=[jax.ShapeDtypeStruct((nb * seq, VAL_DIM), bf16),
                   jax.ShapeDtypeStruct((nb, N_HEADS, HEAD_K, HEAD_V), f32)],
        scratch_shapes=[pltpu.VMEM((HG, rows + 8, HEAD_K), f32)] * 3
        + [pltpu.VMEM((HG, HEAD_K, HEAD_V), f32)]
        + [pltpu.VMEM((rows, HW), f32)] * 3
        + [pltpu.VMEM((rows, 128), f32)] * 3
        + [pltpu.VMEM((rows, HW), f32),
           pltpu.VMEM((nc, 2 * CHUNK, HW), bf16),
           pltpu.VMEM((nc, HG, HEAD_K, CHUNK), bf16),
           pltpu.VMEM((nc, HG, CHUNK, CHUNK), bf16)],
        compiler_params=_params("parallel", "parallel", "arbitrary"),
        name="gdn_prompt",
    )(qkvz, qkvz, qkvz, qkvz, ba, w_conv, w_conv, w_conv, onorm)


def _gdn_sample_kernel(q_ref, k_ref, v_ref, qst_ref, kst_ref, vst_ref, z_ref, ba_ref, wq_ref, wk_ref, wv_ref,
                       onorm_ref, s0_ref, oa_ref, s_ref, e_scr):
    nseq = CHUNK // SEG_S
    heads = range(HG)
    seqs = range(nseq)
    hs = [slice(h * HEAD_K, (h + 1) * HEAD_K) for h in heads]
    gs = [slice(HG + h, HG + h + 1) for h in heads]
    js = [slice(j * SEG_S, (j + 1) * SEG_S) for j in seqs]
    incl, strict, same = _seq_masks(CHUNK, SEG_S)
    hist = SHORT_CONV - 1

    def conv(x_ref, st_ref, w_ref):
        e_scr[:, SEG_S - hist:SEG_S, :] = st_ref[...]
        e_scr[:, SEG_S:, :] = x_ref[...].reshape(nseq, SEG_S, HW)
        off = SEG_S - hist
        acc = w_ref[0:1, :] * e_scr[:, off:off + SEG_S, :].reshape(CHUNK, HW)
        for j in range(1, SHORT_CONV):
            acc += w_ref[j:j + 1, :] * e_scr[:, off + j:off + j + SEG_S, :].reshape(CHUNK, HW)
        return _silu(acc)

    qc, kc, vc = conv(q_ref, qst_ref, wq_ref), conv(k_ref, kst_ref, wk_ref), conv(v_ref, vst_ref, wv_ref)
    bac = ba_ref[...]
    gc = _mask_dot(incl, bac)
    gtot = _mask_dot(same, bac)
    gct = gc.T
    glv = jnp.exp(gtot)
    uv, wk, qkd, qg, kd = _chunk_local(
        [qc[:, hs[h]] for h in heads], [kc[:, hs[h]] for h in heads], [vc[:, hs[h]] for h in heads],
        [bac[:, h:h + 1] for h in heads], [gc[:, gs[h]] for h in heads], [gct[gs[h], :] for h in heads],
        [gtot[:, gs[h]] for h in heads], incl, strict, SEG_S)
    kdt = [kd[h].T.astype(bf16) for h in heads]
    p = [[_dot(jnp.concatenate([wk[h][js[j]], qg[h][js[j]]], axis=0).astype(bf16),
               s0_ref[j, h].astype(bf16)) for j in seqs] for h in heads]
    u = [[uv[h][js[j]] - p[h][j][:SEG_S] for j in seqs] for h in heads]
    zeros = jnp.zeros((SEG_S, HEAD_V), f32)
    for h in heads:
        for j in seqs:
            u_rows = jnp.concatenate([u[h][j] if i == j else zeros for i in seqs], axis=0).astype(bf16)
            s_ref[j, h] = glv[j * SEG_S:j * SEG_S + 1, gs[h]] * s0_ref[j, h] + _dot(kdt[h], u_rows)
    for h in heads:
        u_all = jnp.concatenate(u[h], axis=0).astype(bf16)
        o = jnp.concatenate([p[h][j][SEG_S:] for j in seqs], axis=0) + _dot(qkd[h].astype(bf16), u_all)
        oa_ref[:, hs[h]] = _out_gate(o, onorm_ref[...], z_ref[:, hs[h]])


def _gdn_sample(qkvz, qkv_state, ba, w_conv, onorm, s0, row0):
    nb = qkv_state.shape[0]
    nseq = CHUNK // SEG_S
    rb0 = row0 // CHUNK
    col = lambda off: pl.BlockSpec((CHUNK, HW), lambda g, hg: (rb0 + g, off * NHG + hg))
    st = lambda off: pl.BlockSpec((nseq, SHORT_CONV - 1, HW), lambda g, hg: (g, 0, off * NHG + hg))
    wcol = lambda off: pl.BlockSpec((SHORT_CONV, HW), lambda g, hg: (0, off * NHG + hg))
    sspec = pl.BlockSpec((nseq, HG, HEAD_K, HEAD_V), lambda g, hg: (g, hg, 0, 0))
    return pl.pallas_call(
        _gdn_sample_kernel,
        grid=(nb // nseq, NHG),
        in_specs=[col(0), col(1), col(2), st(0), st(1), st(2), col(3),
                  pl.BlockSpec((CHUNK, 128), lambda g, hg: (rb0 + g, hg)),
                  wcol(0), wcol(1), wcol(2),
                  pl.BlockSpec((1, HEAD_V), lambda g, hg: (0, 0)),
                  sspec],
        out_specs=[pl.BlockSpec((CHUNK, HW), lambda g, hg: (g, hg)), sspec],
        out_shape=[jax.ShapeDtypeStruct((nb * SEG_S, VAL_DIM), bf16), jax.ShapeDtypeStruct(s0.shape, f32)],
        scratch_shapes=[pltpu.VMEM((nseq, 2 * SEG_S, HW), f32)],
        compiler_params=_params("parallel", "parallel"),
        name="gdn_sample",
    )(qkvz, qkvz, qkvz, qkv_state, qkv_state, qkv_state, qkvz, ba, w_conv, w_conv, w_conv, onorm, s0)


CONV_LC = 128
CONV_RB = 64


def _ln_silu(c, g, b):
    mu = jnp.mean(c, axis=-1, keepdims=True)
    xc = c - mu
    y = xc * lax.rsqrt(jnp.mean(xc * xc, axis=-1, keepdims=True) + EPS) * g + b
    return _silu(y).astype(bf16)


CONV_PAD = 32


def _gates_conv_kernel(u_ref, wg_ref, x_ref, cw_ref, cb_ref, g_ref, c_ref, ext, *, tiles_per_seq):
    j, i = pl.program_id(0), pl.program_id(1)
    tm, cw = x_ref.shape

    @pl.when((i == 0) & (j == 0))
    def _():
        ext[:, tm:, :] = jnp.zeros((cw // CONV_LC, CONV_PAD, CONV_LC), f32)

    first = (i % tiles_per_seq) == 0
    off = CONV_PAD - (CONV_WIDTH - 1)
    nlc = cw // CONV_LC
    gw = g_ref.shape[1] // nlc
    for lc in range(nlc):
        gs = slice(lc * gw, (lc + 1) * gw)
        g_ref[:, gs] = _sigmoid(_dot_nt(u_ref[...], wg_ref[gs, :])).astype(g_ref.dtype)
        ls = slice(lc * CONV_LC, (lc + 1) * CONV_LC)
        ext[lc, 0:CONV_PAD, :] = jnp.where(first, 0.0, ext[lc, tm:tm + CONV_PAD, :])
        ext[lc, CONV_PAD:, :] = x_ref[:, ls]
        for r0 in range(0, tm, CONV_RB):
            acc = jnp.broadcast_to(cb_ref[:, ls], (CONV_RB, CONV_LC))
            for t in range(CONV_WIDTH):
                acc += cw_ref[t:t + 1, ls] * ext[lc, r0 + off + t:r0 + off + t + CONV_RB, :]
            c_ref[r0:r0 + CONV_RB, ls] = acc


def _gates_conv_prompt(u, w_tail, glu, cw, cb, nb, seq, tm=1024, tn=1024):
    k = u.shape[1]
    ng = 2 * D_MODEL // tn
    cwid = CONV_CH // ng
    j0 = (O_GATE - O_GLU) // tn
    return pl.pallas_call(
        functools.partial(_gates_conv_kernel, tiles_per_seq=seq // tm),
        grid=(ng, nb * seq // tm),
        in_specs=[pl.BlockSpec((tm, k), lambda j, i: (i, 0)),
                  pl.BlockSpec((tn, k), lambda j, i: (j + j0, 0)),
                  pl.BlockSpec((tm, cwid), lambda j, i: (i, j)),
                  pl.BlockSpec((CONV_WIDTH, cwid), lambda j, i: (0, j)),
                  pl.BlockSpec((1, cwid), lambda j, i: (0, j))],
        out_specs=[pl.BlockSpec((tm, tn), lambda j, i: (i, j)), pl.BlockSpec((tm, cwid), lambda j, i: (i, j))],
        out_shape=[jax.ShapeDtypeStruct((nb * seq, 2 * D_MODEL), bf16),
                   jax.ShapeDtypeStruct((nb * seq, CONV_CH), f32)],
        scratch_shapes=[pltpu.VMEM((cwid // CONV_LC, tm + CONV_PAD, CONV_LC), f32)],
        compiler_params=_params("arbitrary", "arbitrary"),
        name="gates_conv",
    )(u, w_tail, glu, cw, cb)


def _cconv_sample_kernel(x_ref, st_ref, w_ref, b_ref, g_ref, beta_ref, o_ref, e_scr, cbuf):
    nseq = CHUNK // SEG_S
    hist = CONV_WIDTH - 1
    e_scr[:, 0:hist, :] = st_ref[...]
    e_scr[:, hist:hist + SEG_S, :] = x_ref[...].reshape(nseq, SEG_S, CONV_CH)
    for lc in range(CONV_CH // CONV_LC):
        ls = slice(lc * CONV_LC, (lc + 1) * CONV_LC)
        acc = jnp.broadcast_to(b_ref[:, ls], (CHUNK, CONV_LC))
        for j in range(CONV_WIDTH):
            acc += w_ref[j:j + 1, ls] * e_scr[:, j:j + SEG_S, ls].reshape(CHUNK, CONV_LC)
        cbuf[:, ls] = acc
    o_ref[...] = _ln_silu(cbuf[...], g_ref[...], beta_ref[...])


def _cconv_sample(glu, glu_state, w, b, g, beta, row0):
    nb, hist, _ = glu_state.shape
    nseq = CHUNK // SEG_S
    rb0 = row0 // CHUNK
    vec = pl.BlockSpec((1, CONV_CH), lambda i: (0, 0))
    return pl.pallas_call(
        _cconv_sample_kernel,
        grid=(nb // nseq,),
        in_specs=[pl.BlockSpec((CHUNK, CONV_CH), lambda i: (rb0 + i, 0)),
                  pl.BlockSpec((nseq, hist, CONV_CH), lambda i: (i, 0, 0)),
                  pl.BlockSpec((CONV_WIDTH, CONV_CH), lambda i: (0, 0)), vec, vec, vec],
        out_specs=pl.BlockSpec((CHUNK, CONV_CH), lambda i: (i, 0)),
        out_shape=jax.ShapeDtypeStruct((nb * SEG_S, CONV_CH), bf16),
        scratch_shapes=[pltpu.VMEM((nseq, hist + SEG_S, CONV_CH), f32), pltpu.VMEM((CHUNK, CONV_CH), f32)],
        compiler_params=_params("parallel"),
        name="cconv_sample",
    )(glu, glu_state, w, b, g, beta)


def _merge_kernel(oap_ref, oas_ref, cp_ref, cs_ref, gap_ref, gas_ref, gbp_ref, gbs_ref, h_ref,
                  wa_ref, wb_ref, wo_ref, lng_ref, lnb_ref, gpost_ref, o_ref, *, na):
    def run(oa_ref, c, ga_ref, gb_ref):
        merged = (ga_ref[...].astype(f32) * _dot(oa_ref[...], wa_ref[...])
                  + gb_ref[...].astype(f32) * _dot(c, wb_ref[...]))
        o_ref[...] = h_ref[...] + _rms(_dot(merged.astype(bf16), wo_ref[...]), gpost_ref[...])

    i = pl.program_id(0)
    pl.when(i < na)(lambda: run(oap_ref, _ln_silu(cp_ref[...], lng_ref[...], lnb_ref[...]), gap_ref, gbp_ref))
    pl.when(i >= na)(lambda: run(oas_ref, cs_ref[...], gas_ref, gbs_ref))


def _merge(oa_p, oa_s, c_p, c_s, gates_p, gates_s, h, wa, wb, wo, ln_g, ln_b, gpost, tm=256):
    t = h.shape[0]
    na = oa_p.shape[0] // tm
    act = pl.BlockSpec((tm, D_MODEL), lambda i: (i, 0))
    vec = pl.BlockSpec((1, D_MODEL), lambda i: (0, 0))
    src = lambda col: _split_specs((tm, D_MODEL), na, 2, col)
    wsp = pl.BlockSpec((D_MODEL, D_MODEL), lambda i: (0, 0), pipeline_mode=pl.Buffered(1))
    return pl.pallas_call(
        functools.partial(_merge_kernel, na=na),
        grid=(t // tm,),
        in_specs=src(0) + src(0) + src(0) + src(1) + [act, wsp, wsp, wsp, vec, vec, vec],
        out_specs=act,
        out_shape=jax.ShapeDtypeStruct((t, D_MODEL), f32),
        compiler_params=_params("parallel"),
        name="merge",
    )(oa_p, oa_s, c_p, c_s, gates_p, gates_s, gates_p, gates_s, h, wa, wb, wo, ln_g, ln_b, gpost)


def _ple_kernel(hn_ref, pa_ref, pb_ref, h_ref, wg_ref, wp_ref, gpost_ref, oa_ref, ob_ref, *, na):
    i = pl.program_id(0)
    gate = _sigmoid(_dot(hn_ref[...], wg_ref[...]))

    def out(p_ref):
        v = gate * _dot(p_ref[...].astype(bf16), wp_ref[...])
        return h_ref[...] + _rms(v, gpost_ref[...])

    @pl.when(i < na)
    def _():
        oa_ref[...] = out(pa_ref)

    @pl.when(i >= na)
    def _():
        ob_ref[...] = out(pb_ref)


def _ple(hn, pa, pb, h, wg, wp, gpost, tm=512):
    na, nb = pa.shape[0] // tm, pb.shape[0] // tm
    row = lambda i: (i, 0)
    first = lambda i: (jnp.minimum(i, na - 1), 0)
    second = lambda i: (jnp.maximum(i - na, 0), 0)
    act = pl.BlockSpec((tm, D_MODEL), row)
    return pl.pallas_call(
        functools.partial(_ple_kernel, na=na),
        grid=(na + nb,),
        in_specs=[act, pl.BlockSpec((tm, PLE_DIM), first), pl.BlockSpec((tm, PLE_DIM), second), act,
                  pl.BlockSpec((D_MODEL, D_MODEL), lambda i: (0, 0), pipeline_mode=pl.Buffered(1)),
                  pl.BlockSpec((PLE_DIM, D_MODEL), lambda i: (0, 0)),
                  pl.BlockSpec((1, D_MODEL), lambda i: (0, 0))],
        out_specs=[pl.BlockSpec((tm, D_MODEL), first), pl.BlockSpec((tm, D_MODEL), second)],
        out_shape=[jax.ShapeDtypeStruct((na * tm, D_MODEL), f32), jax.ShapeDtypeStruct((nb * tm, D_MODEL), f32)],
        compiler_params=_params("arbitrary"),
        name="ple",
    )(hn, pa, pb, h, wg, wp, gpost)


def _layer(xp, xs, pp, ps, nb, seq, s0, qkv_buf, glu_buf, w):
    (ffn1_pre, ffn1_w_gu, ffn1_w_down, ffn1_post, mix_pre, w_in, w_short_conv, a_log, dt_bias,
     o_norm, w_dw_conv, b_dw_conv, ln_g, ln_b, w_branch_a, w_branch_b, w_out, mix_post,
     ffn2_pre, ffn2_w_gu, ffn2_w_down, ffn2_post, ple_pre, w_ple_gate, w_ple_proj,
     ple_post) = w
    tp = nb * seq
    ns = s0.shape[0]
    row = lambda v: v.astype(f32).reshape(1, -1)
    cast = lambda v: v.astype(bf16)

    h1, u = _ffn([xp, xs], row(ffn1_pre), cast(ffn1_w_gu), cast(ffn1_w_down), row(ffn1_post), row(mix_pre))

    w_qkvz, w_tail, w_mid = _cast_w_in(jnp.swapaxes(w_in, 1, 2))
    qkvz = _proj(u, w_qkvz, 0, O_BETA, "none", f32)
    ba = _proj_ba(u, cast(_ba_weight(w_mid.T)), _group_lanes(a_log), _group_lanes(dt_bias))
    glu = _proj_glu(u, w_tail)

    w_conv = w_short_conv.astype(f32)
    onorm = row(o_norm)
    oa_p, s_p = _gdn_prompt(qkvz, ba, w_conv, onorm, nb, seq)
    oa_s, s_s = _gdn_sample(qkvz, qkv_buf.astype(f32), ba, w_conv, onorm, s0.astype(f32), tp)

    cw, cb, lg, lb = w_dw_conv.astype(f32), row(b_dw_conv), row(ln_g), row(ln_b)
    gates_p, c_p = _gates_conv_prompt(u, w_tail, glu, cw, cb, nb, seq)
    gates_s = _proj(u, w_tail, O_GATE - O_GLU, 2 * D_MODEL, "sigmoid", bf16, r0=tp, rows=xs.shape[0])
    c_s = _cconv_sample(glu, glu_buf.astype(f32), cw, cb, lg, lb, tp)

    h2 = _merge(oa_p, oa_s, c_p, c_s, gates_p, gates_s, h1, cast(w_branch_a), cast(w_branch_b), cast(w_out),
                lg, lb, row(mix_post))
    h3, hn = _ffn([h2], row(ffn2_pre), cast(ffn2_w_gu), cast(ffn2_w_down), row(ffn2_post), row(ple_pre))
    yp, ys = _ple(hn, pp, ps, h3, cast(w_ple_gate), cast(w_ple_proj), row(ple_post))

    tail = lambda a, width, n: jnp.stack([a[(b + 1) * seq - n:(b + 1) * seq, :width] for b in range(nb)])
    qkv_p = tail(qkvz, QKV_DIM, SHORT_CONV - 1)
    glu_p = tail(glu, CONV_CH, CONV_WIDTH - 1)
    qkv_s = qkvz[tp:, :QKV_DIM].reshape(ns, SEG_S, QKV_DIM)[:, SEG_S - (SHORT_CONV - 1):]
    glu_s = jnp.concatenate([glu_buf.astype(f32)[:, SEG_S:], glu[tp:].reshape(ns, SEG_S, CONV_CH)], axis=1)
    return (yp, ys, s_p, qkv_p, glu_p, s_s, qkv_s, glu_s)


def kernel(x_prompt, x_sample, p_prompt, p_sample, state_delta, state_qkv_conv, state_glu_conv, ffn1_pre, ffn1_w_gu, ffn1_w_down, ffn1_post, mix_pre, w_in, w_short_conv, a_log, dt_bias, o_norm, w_dw_conv, b_dw_conv, ln_g, ln_b, w_branch_a, w_branch_b, w_out, mix_post, ffn2_pre, ffn2_w_gu, ffn2_w_down, ffn2_post, ple_pre, w_ple_gate, w_ple_proj, ple_post):
    weights = (ffn1_pre, ffn1_w_gu, ffn1_w_down, ffn1_post, mix_pre, w_in, w_short_conv, a_log,
               dt_bias, o_norm, w_dw_conv, b_dw_conv, ln_g, ln_b, w_branch_a, w_branch_b,
               w_out, mix_post, ffn2_pre, ffn2_w_gu, ffn2_w_down, ffn2_post, ple_pre,
               w_ple_gate, w_ple_proj, ple_post)
    nb, seq, _ = x_prompt.shape
    ns, ls, _ = x_sample.shape
    assert ls == SEG_S and seq % CHUNK == 0 and ns % (CHUNK // SEG_S) == 0
    depth = ffn1_pre.shape[0]
    tp = nb * seq
    xp, xs = x_prompt.reshape(tp, D_MODEL), x_sample.reshape(ns * ls, D_MODEL)
    outs = [[] for _ in range(6)]
    for i in range(depth):
        wi = tuple(wt[i:i + 1] if wt is w_in else wt[i] for wt in weights)
        xp, xs, s_p, q_p, g_p, s_s, q_s, g_s = _layer(
            xp, xs, p_prompt[i].reshape(tp, PLE_DIM), p_sample[i].reshape(ns * ls, PLE_DIM), nb, seq,
            state_delta[i], state_qkv_conv[i], state_glu_conv[i], wi)
        for lst, v in zip(outs, (s_p, q_p, g_p, s_s, q_s, g_s)):
            lst.append(v)
    return (xp.reshape(nb, seq, D_MODEL), xs.reshape(ns, ls, D_MODEL)) + tuple(jnp.stack(lst) for lst in outs)
```

```python
import functools

import jax
import jax.numpy as jnp
from jax import lax
from jax.experimental import pallas as pl
from jax.experimental.pallas import tpu as pltpu

f32 = jnp.float32
bf16 = jnp.bfloat16

D_MODEL = 2048
N_HEADS = 16
HEAD_K = 128
HEAD_V = 128
KEY_DIM = N_HEADS * HEAD_K
VAL_DIM = N_HEADS * HEAD_V
QKV_DIM = 2 * KEY_DIM + VAL_DIM
SHORT_CONV = 4
CONV_CH = D_MODEL
CONV_WIDTH = 31
D_FF = 5632
PLE_DIM = 256
EPS = 1e-6

O_Z = QKV_DIM
O_BETA = O_Z + VAL_DIM
O_A = O_BETA + N_HEADS
O_GLU = O_A + N_HEADS
O_GATE = O_GLU + 2 * CONV_CH

CHUNK = 64
HG = 8
NHG = N_HEADS // HG
HW = HG * HEAD_K
SEG_S = 8
LOCAL_CHUNKS = 4

VMEM_LIMIT = 56 * 1024 * 1024


def _sigmoid(x):
    return 1.0 / (1.0 + jnp.exp(-x))


def _silu(x):
    return x * _sigmoid(x)


def _rms(x, g):
    return x * lax.rsqrt(jnp.mean(x * x, axis=-1, keepdims=True) + EPS) * g


def _dot(a, b):
    return jnp.dot(a, b, preferred_element_type=f32)


def _dot_nt(a, b):
    return lax.dot_general(a, b, (((1,), (1,)), ((), ())), preferred_element_type=f32)


def _params(*sem):
    return pltpu.CompilerParams(dimension_semantics=sem, vmem_limit_bytes=VMEM_LIMIT)


def _per_source(i, na, refs, fn):
    if len(refs) == 1:
        fn(refs[0])
    else:
        pl.when(i < na)(lambda: fn(refs[0]))
        pl.when(i >= na)(lambda: fn(refs[1]))


def _split_specs(block, na, nsrc, col=0):
    if nsrc == 1:
        return [pl.BlockSpec(block, lambda i, *_: (i, col))]
    return [pl.BlockSpec(block, lambda i, *_: (jnp.minimum(i, na - 1), col)),
            pl.BlockSpec(block, lambda i, *_: (jnp.maximum(i - na, 0), col))]


def _ffn_kernel(*refs, na, nsrc):
    x_refs = refs[:nsrc]
    gpre_ref, wg_ref, wu_ref, wd_ref, gpost_ref, gnext_ref, h_ref, hn_ref, xn_scr, acc_scr = refs[nsrc:]
    i, j = pl.program_id(0), pl.program_id(1)

    @pl.when(j == 0)
    def _():
        def norm_in(x_ref):
            xn_scr[...] = _rms(x_ref[...], gpre_ref[...]).astype(bf16)

        _per_source(i, na, x_refs, norm_in)
        acc_scr[...] = jnp.zeros_like(acc_scr)

    xn = xn_scr[...]
    a = _silu(_dot(xn, wg_ref[...])) * _dot(xn, wu_ref[...])
    acc_scr[...] += _dot(a.astype(bf16), wd_ref[...])

    @pl.when(j == pl.num_programs(1) - 1)
    def _():
        def residual_out(x_ref):
            h = x_ref[...] + 0.5 * _rms(acc_scr[...], gpost_ref[...])
            h_ref[...] = h
            hn_ref[...] = _rms(h, gnext_ref[...]).astype(bf16)

        _per_source(i, na, x_refs, residual_out)


def _ffn(xs, gpre, w_gu, w_down, gpost, gnext, tm=512, tf=512):
    na = xs[0].shape[0] // tm
    nt = sum(x.shape[0] for x in xs) // tm
    nf = D_FF // tf
    row = lambda i, j: (i, 0)
    vec = pl.BlockSpec((1, D_MODEL), lambda i, j: (0, 0))
    return pl.pallas_call(
        functools.partial(_ffn_kernel, na=na, nsrc=len(xs)),
        grid=(nt, nf),
        in_specs=_split_specs((tm, D_MODEL), na, len(xs))
        + [vec,
           pl.BlockSpec((D_MODEL, tf), lambda i, j: (0, j)),
           pl.BlockSpec((D_MODEL, tf), lambda i, j: (0, j + nf)),
           pl.BlockSpec((tf, D_MODEL), lambda i, j: (j, 0)), vec, vec],
        out_specs=[pl.BlockSpec((tm, D_MODEL), row), pl.BlockSpec((tm, D_MODEL), row)],
        out_shape=[jax.ShapeDtypeStruct((nt * tm, D_MODEL), f32), jax.ShapeDtypeStruct((nt * tm, D_MODEL), bf16)],
        scratch_shapes=[pltpu.VMEM((tm, D_MODEL), bf16), pltpu.VMEM((tm, D_MODEL), f32)],
        compiler_params=_params("parallel", "arbitrary"),
        name="ffn",
    )(*xs, gpre, w_gu, w_gu, w_down, gpost, gnext)


CAST_TN = 512
W_IN_SHIFT = O_GLU - O_BETA


def _cast_w_in_kernel(a_ref, b0_ref, b1_ref, oa_ref, ob_ref, oc_ref):
    oa_ref[...] = a_ref[...].astype(bf16)

    @pl.when(pl.program_id(0) == 0)
    def _():
        oc_ref[...] = b0_ref[:, 0:128].astype(bf16)

    keep = CAST_TN - W_IN_SHIFT
    lane = lax.broadcasted_iota(jnp.int32, b0_ref.shape, 1)
    moved = jnp.where(lane < keep, pltpu.roll(b0_ref[...], keep, 1), pltpu.roll(b1_ref[...], keep, 1))
    ob_ref[...] = moved.astype(bf16)


def _cast_w_in(w_in):
    n_tail = w_in.shape[2] - O_GLU
    assert O_BETA % CAST_TN == 0 and n_tail == O_BETA and 0 < W_IN_SHIFT < CAST_TN
    nb = O_BETA // CAST_TN
    src = lambda off: pl.BlockSpec((None, D_MODEL, CAST_TN), lambda j: (0, 0, j + off))
    blk = pl.BlockSpec((D_MODEL, CAST_TN), lambda j: (0, j))
    return pl.pallas_call(
        _cast_w_in_kernel,
        grid=(nb,),
        in_specs=[src(0), src(nb), src(nb + 1)],
        out_specs=[blk, blk, pl.BlockSpec((D_MODEL, 128), lambda j: (0, 0))],
        out_shape=[jax.ShapeDtypeStruct((D_MODEL, O_BETA), bf16), jax.ShapeDtypeStruct((D_MODEL, n_tail), bf16),
                   jax.ShapeDtypeStruct((D_MODEL, 128), bf16)],
        compiler_params=_params("arbitrary"),
        name="cast_w_in",
    )(w_in, w_in, w_in)


def _proj_kernel(x_ref, w_ref, o_ref, *, act):
    y = _dot(x_ref[...], w_ref[...])
    if act == "sigmoid":
        y = _sigmoid(y)
    o_ref[...] = y.astype(o_ref.dtype)


def _proj(x, w, n0, n, act, out_dtype, r0=0, rows=None, tm=1024, tn=1024):
    t, k = x.shape
    rows = t if rows is None else rows
    i0, j0 = r0 // tm, n0 // tn
    return pl.pallas_call(
        functools.partial(_proj_kernel, act=act),
        grid=(rows // tm, n // tn),
        in_specs=[pl.BlockSpec((tm, k), lambda i, j: (i + i0, 0)),
                  pl.BlockSpec((k, tn), lambda i, j: (0, j + j0))],
        out_specs=pl.BlockSpec((tm, tn), lambda i, j: (i, j)),
        out_shape=jax.ShapeDtypeStruct((rows, n), out_dtype),
        compiler_params=_params("parallel", "arbitrary"),
        name="proj_" + act,
    )(x, w)


def _glu_kernel(x_ref, wa_ref, wb_ref, o_ref):
    x = x_ref[...]
    o_ref[...] = _dot(x, wa_ref[...]) * _sigmoid(_dot(x, wb_ref[...]))


def _proj_glu(x, w, tm=1024, tn=512):
    t, k = x.shape
    nj = CONV_CH // tn
    return pl.pallas_call(
        _glu_kernel,
        grid=(t // tm, nj),
        in_specs=[pl.BlockSpec((tm, k), lambda i, j: (i, 0)),
                  pl.BlockSpec((k, tn), lambda i, j: (0, j)),
                  pl.BlockSpec((k, tn), lambda i, j: (0, j + nj))],
        out_specs=pl.BlockSpec((tm, tn), lambda i, j: (i, j)),
        out_shape=jax.ShapeDtypeStruct((t, CONV_CH), f32),
        compiler_params=_params("parallel", "arbitrary"),
        name="proj_glu",
    )(x, w, w)


def _ba_kernel(x_ref, w_ref, alog_ref, dtb_ref, o_ref):
    y = _dot(x_ref[...], w_ref[...])
    lane = lax.broadcasted_iota(jnp.int32, y.shape, 1)
    ya = y + dtb_ref[...]
    softplus = jnp.maximum(ya, 0.0) + jnp.log(1.0 + jnp.exp(-jnp.abs(ya)))
    o_ref[...] = jnp.where(lane < HG, _sigmoid(y), -jnp.exp(alog_ref[...]) * softplus)


def _proj_ba(x, w_ba, alog_row, dtb_row, tm=1024):
    t, k = x.shape
    vec = pl.BlockSpec((1, 128), lambda i, j: (0, j))
    return pl.pallas_call(
        _ba_kernel,
        grid=(t // tm, NHG),
        in_specs=[pl.BlockSpec((tm, k), lambda i, j: (i, 0)),
                  pl.BlockSpec((k, 128), lambda i, j: (0, j)), vec, vec],
        out_specs=pl.BlockSpec((tm, 128), lambda i, j: (i, j)),
        out_shape=jax.ShapeDtypeStruct((t, NHG * 128), f32),
        compiler_params=_params("parallel", "arbitrary"),
        name="proj_ba",
    )(x, w_ba, alog_row, dtb_row)


def _group_lanes(v):
    r = jnp.zeros((NHG, 128), f32).at[:, HG:2 * HG].set(v.astype(f32).reshape(NHG, HG))
    return r.reshape(1, NHG * 128)


def _ba_weight(w_ba):
    wb = w_ba[:, :N_HEADS].reshape(D_MODEL, NHG, HG)
    wa = w_ba[:, N_HEADS:].reshape(D_MODEL, NHG, HG)
    w = jnp.zeros((D_MODEL, NHG, 128), w_ba.dtype).at[:, :, :HG].set(wb).at[:, :, HG:2 * HG].set(wa)
    return w.reshape(D_MODEL, NHG * 128)


def _seq_masks(n, seg):
    r = lax.broadcasted_iota(jnp.int32, (n, n), 0)
    c = lax.broadcasted_iota(jnp.int32, (n, n), 1)
    same = (r // seg) == (c // seg)
    return same & (r >= c), same & (r > c), same


def _mask_dot(mask, x):
    l = jnp.where(mask, 1.0, 0.0).astype(bf16)
    hi = x.astype(bf16)
    r1 = x - hi.astype(f32)
    mid = r1.astype(bf16)
    lo = (r1 - mid.astype(f32)).astype(bf16)
    return _dot(l, hi) + _dot(l, mid) + _dot(l, lo)


def _mm(a, b):
    return _dot(a.astype(bf16), b.astype(bf16))


def _unit_lower_solve(ms, rhss, seg):
    n = range(len(ms))
    r = lax.broadcasted_iota(jnp.int32, (CHUNK, CHUNK), 0)
    c = lax.broadcasted_iota(jnp.int32, (CHUNK, CHUNK), 1)
    eye = (r == c).astype(f32)
    if seg <= 8:
        d, lo = ms, None
    else:
        blk = (r // 16) == (c // 16)
        d = [jnp.where(blk, m, 0.0) for m in ms]
        lo = [ms[i] - d[i] for i in n]
    d2 = [_mm(d[i], d[i]) for i in n]
    d3 = [_mm(d[i], d2[i]) for i in n]
    d4 = [_mm(d2[i], d2[i]) for i in n]
    a = [eye - d[i] + d2[i] - d3[i] for i in n]
    td = [a[i] + _mm(a[i], d4[i]) for i in n]
    if seg > 8:
        d8 = [_mm(d4[i], d4[i]) for i in n]
        td = [td[i] + _mm(td[i], d8[i]) for i in n]
    y = [_mm(td[i], rhss[i]) for i in n]
    if lo is None:
        return y
    e = [_mm(td[i], lo[i]) for i in n]
    e2 = [_mm(e[i], e[i]) for i in n]
    w = [y[i] - _mm(e[i], y[i]) for i in n]
    return [w[i] + _mm(e2[i], w[i]) for i in n]


def _chunk_local(qs, ks, vs, betas, gcols, grows, gtots, incl, strict, seg):
    n = range(len(qs))
    qn = [q * lax.rsqrt(jnp.sum(q * q, axis=-1, keepdims=True) + EPS) * (HEAD_K ** -0.5) for q in qs]
    kn = [k * lax.rsqrt(jnp.sum(k * k, axis=-1, keepdims=True) + EPS) for k in ks]
    decay = [jnp.exp(jnp.where(incl, gcols[i] - grows[i], -jnp.inf)) for i in n]
    kb = [k.astype(bf16) for k in kn]
    kk = [_dot_nt(kb[i], kb[i]) for i in n]
    qk = [_dot_nt(qn[i].astype(bf16), kb[i]) for i in n]
    m = [jnp.where(strict, betas[i] * decay[i] * kk[i], 0.0) for i in n]
    gamma = [jnp.exp(g) for g in gcols]
    rhs = [jnp.concatenate([betas[i] * vs[i], (betas[i] * gamma[i]) * kn[i]], axis=1) for i in n]
    sol = _unit_lower_solve(m, rhs, seg)
    uv = [s[:, :HEAD_V] for s in sol]
    wk = [s[:, HEAD_V:] for s in sol]
    qkd = [qk[i] * decay[i] for i in n]
    qg = [qn[i] * gamma[i] for i in n]
    kd = [kn[i] * jnp.exp(gtots[i] - gcols[i]) for i in n]
    return uv, wk, qkd, qg, kd


def _out_gate(o, onorm, z):
    return (_rms(o, onorm) * _silu(z)).astype(bf16)


def _gdn_prompt_kernel(q_ref, k_ref, v_ref, z_ref, ba_ref, wq_ref, wk_ref, wv_ref, onorm_ref,
                       oa_ref, s_ref,
                       eq, ek, ev, s_scr, qc_s, kc_s, vc_s, gc_s, gt_s, gl_s, uv_s, wq_s, kdt_s, qkd_s,
                       *, rows):
    t = pl.program_id(2)
    nc = rows // CHUNK
    heads = range(HG)
    hs = [slice(h * HEAD_K, (h + 1) * HEAD_K) for h in heads]
    gs = [slice(HG + h, HG + h + 1) for h in heads]

    @pl.when(t == 0)
    def _():
        s_scr[...] = jnp.zeros_like(s_scr)
        for e in (eq, ek, ev):
            e[:, 0:8, :] = jnp.zeros((HG, 8, HEAD_K), f32)

    off = 8 - (SHORT_CONV - 1)
    for e, x_ref, w_ref, dst in ((eq, q_ref, wq_ref, qc_s), (ek, k_ref, wk_ref, kc_s), (ev, v_ref, wv_ref, vc_s)):
        for h in heads:
            e[h, 8:, :] = x_ref[:, hs[h]]
            for r0 in range(0, rows, CHUNK):
                acc = w_ref[0:1, hs[h]] * e[h, r0 + off:r0 + off + CHUNK, :]
                for j in range(1, SHORT_CONV):
                    acc += w_ref[j:j + 1, hs[h]] * e[h, r0 + off + j:r0 + off + j + CHUNK, :]
                dst[r0:r0 + CHUNK, hs[h]] = _silu(acc)

    incl_b, _, same_b = _seq_masks(rows, CHUNK)
    ba = ba_ref[...]
    gc_s[...] = _mask_dot(incl_b, ba)
    gtot_b = _mask_dot(same_b, ba)
    gt_s[...] = gtot_b
    gl_s[...] = jnp.exp(gtot_b)
    incl, strict, _ = _seq_masks(CHUNK, CHUNK)

    def local(cp, carry):
        cs = [cp * LOCAL_CHUNKS + i for i in range(LOCAL_CHUNKS)]
        rss = [pl.ds(pl.multiple_of(c * CHUNK, CHUNK), CHUNK) for c in cs]
        units = [(i, h) for i in range(LOCAL_CHUNKS) for h in heads]
        bac = [ba_ref[rs, :] for rs in rss]
        gc = [gc_s[rs, :] for rs in rss]
        gtot = [gt_s[rs, :] for rs in rss]
        gct = [g.T for g in gc]
        uv, wk, qkd, qg, kd = _chunk_local(
            [qc_s[rss[i], hs[h]] for i, h in units], [kc_s[rss[i], hs[h]] for i, h in units],
            [vc_s[rss[i], hs[h]] for i, h in units], [bac[i][:, h:h + 1] for i, h in units],
            [gc[i][:, gs[h]] for i, h in units], [gct[i][gs[h], :] for i, h in units],
            [gtot[i][:, gs[h]] for i, h in units], incl, strict, CHUNK)
        for n, (i, h) in enumerate(units):
            uv_s[rss[i], hs[h]] = uv[n]
            wq_s[cs[i], 0:CHUNK, hs[h]] = wk[n].astype(bf16)
            wq_s[cs[i], CHUNK:2 * CHUNK, hs[h]] = qg[n].astype(bf16)
            kdt_s[cs[i], h] = kd[n].T.astype(bf16)
            qkd_s[cs[i], h] = qkd[n].astype(bf16)
        return carry

    lax.fori_loop(0, nc // LOCAL_CHUNKS, local, 0)

    def recur(c, carry):
        r0 = pl.multiple_of(c * CHUNK, CHUNK)
        rs = pl.ds(r0, CHUNK)
        s = [s_scr[h] for h in heads]
        p = [_dot(wq_s[c, :, hs[h]], s[h].astype(bf16)) for h in heads]
        u = [(uv_s[rs, hs[h]] - p[h][:CHUNK]).astype(bf16) for h in heads]
        o = [p[h][CHUNK:] + _dot(qkd_s[c, h], u[h]) for h in heads]
        ds = [_dot(kdt_s[c, h], u[h]) for h in heads]
        for h in heads:
            s_scr[h] = gl_s[pl.ds(r0, 1), gs[h]] * s[h] + ds[h]
            oa_ref[rs, hs[h]] = _out_gate(o[h], onorm_ref[...], z_ref[rs, hs[h]])
        return carry

    lax.fori_loop(0, nc, recur, 0)

    for e in (eq, ek, ev):
        e[:, 0:8, :] = e[:, rows:rows + 8, :]

    @pl.when(t == pl.num_programs(2) - 1)
    def _():
        s_ref[0] = s_scr[...]


def _gdn_prompt(qkvz, ba, w_conv, onorm, nb, seq, rows=256):
    nt = seq // rows
    nc = rows // CHUNK
    rb = lambda b, hg, t: b * nt + t
    col = lambda off: pl.BlockSpec((rows, HW), lambda b, hg, t: (rb(b, hg, t), off * NHG + hg))
    wcol = lambda off: pl.BlockSpec((SHORT_CONV, HW), lambda b, hg, t: (0, off * NHG + hg))
    return pl.pallas_call(
        functools.partial(_gdn_prompt_kernel, rows=rows),
        grid=(nb, NHG, nt),
        in_specs=[col(0), col(1), col(2), col(3),
                  pl.BlockSpec((rows, 128), lambda b, hg, t: (rb(b, hg, t), hg)),
                  wcol(0), wcol(1), wcol(2),
                  pl.BlockSpec((1, HEAD_V), lambda b, hg, t: (0, 0))],
        out_specs=[pl.BlockSpec((rows, HW), lambda b, hg, t: (rb(b, hg, t), hg)),
                   pl.BlockSpec((1, HG, HEAD_K, HEAD_V), lambda b, hg, t: (b, hg, 0, 0))],
        out_shape=[jax.ShapeDtypeStruct((nb * seq, VAL_DIM), bf16),
                   jax.ShapeDtypeStruct((nb, N_HEADS, HEAD_K, HEAD_V), f32)],
        scratch_shapes=[pltpu.VMEM((HG, rows + 8, HEAD_K), f32)] * 3
        + [pltpu.VMEM((HG, HEAD_K, HEAD_V), f32)]
        + [pltpu.VMEM((rows, HW), f32)] * 3
        + [pltpu.VMEM((rows, 128), f32)] * 3
        + [pltpu.VMEM((rows, HW), f32),
           pltpu.VMEM((nc, 2 * CHUNK, HW), bf16),
           pltpu.VMEM((nc, HG, HEAD_K, CHUNK), bf16),
           pltpu.VMEM((nc, HG, CHUNK, CHUNK), bf16)],
        compiler_params=_params("parallel", "parallel", "arbitrary"),
        name="gdn_prompt",
    )(qkvz, qkvz, qkvz, qkvz, ba, w_conv, w_conv, w_conv, onorm)


def _gdn_sample_kernel(q_ref, k_ref, v_ref, qst_ref, kst_ref, vst_ref, z_ref, ba_ref, wq_ref, wk_ref, wv_ref,
                       onorm_ref, s0_ref, oa_ref, s_ref, e_scr):
    nseq = CHUNK // SEG_S
    heads = range(HG)
    seqs = range(nseq)
    hs = [slice(h * HEAD_K, (h + 1) * HEAD_K) for h in heads]
    gs = [slice(HG + h, HG + h + 1) for h in heads]
    js = [slice(j * SEG_S, (j + 1) * SEG_S) for j in seqs]
    incl, strict, same = _seq_masks(CHUNK, SEG_S)
    hist = SHORT_CONV - 1

    def conv(x_ref, st_ref, w_ref):
        e_scr[:, SEG_S - hist:SEG_S, :] = st_ref[...]
        e_scr[:, SEG_S:, :] = x_ref[...].reshape(nseq, SEG_S, HW)
        off = SEG_S - hist
        acc = w_ref[0:1, :] * e_scr[:, off:off + SEG_S, :].reshape(CHUNK, HW)
        for j in range(1, SHORT_CONV):
            acc += w_ref[j:j + 1, :] * e_scr[:, off + j:off + j + SEG_S, :].reshape(CHUNK, HW)
        return _silu(acc)

    qc, kc, vc = conv(q_ref, qst_ref, wq_ref), conv(k_ref, kst_ref, wk_ref), conv(v_ref, vst_ref, wv_ref)
    bac = ba_ref[...]
    gc = _mask_dot(incl, bac)
    gtot = _mask_dot(same, bac)
    gct = gc.T
    glv = jnp.exp(gtot)
    uv, wk, qkd, qg, kd = _chunk_local(
        [qc[:, hs[h]] for h in heads], [kc[:, hs[h]] for h in heads], [vc[:, hs[h]] for h in heads],
        [bac[:, h:h + 1] for h in heads], [gc[:, gs[h]] for h in heads], [gct[gs[h], :] for h in heads],
        [gtot[:, gs[h]] for h in heads], incl, strict, SEG_S)
    kdt = [kd[h].T.astype(bf16) for h in heads]
    p = [[_dot(jnp.concatenate([wk[h][js[j]], qg[h][js[j]]], axis=0).astype(bf16),
               s0_ref[j, h].astype(bf16)) for j in seqs] for h in heads]
    u = [[uv[h][js[j]] - p[h][j][:SEG_S] for j in seqs] for h in heads]
    zeros = jnp.zeros((SEG_S, HEAD_V), f32)
    for h in heads:
        for j in seqs:
            u_rows = jnp.concatenate([u[h][j] if i == j else zeros for i in seqs], axis=0).astype(bf16)
            s_ref[j, h] = glv[j * SEG_S:j * SEG_S + 1, gs[h]] * s0_ref[j, h] + _dot(kdt[h], u_rows)
    for h in heads:
        u_all = jnp.concatenate(u[h], axis=0).astype(bf16)
        o = jnp.concatenate([p[h][j][SEG_S:] for j in seqs], axis=0) + _dot(qkd[h].astype(bf16), u_all)
        oa_ref[:, hs[h]] = _out_gate(o, onorm_ref[...], z_ref[:, hs[h]])


def _gdn_sample(qkvz, qkv_state, ba, w_conv, onorm, s0, row0):
    nb = qkv_state.shape[0]
    nseq = CHUNK // SEG_S
    rb0 = row0 // CHUNK
    col = lambda off: pl.BlockSpec((CHUNK, HW), lambda g, hg: (rb0 + g, off * NHG + hg))
    st = lambda off: pl.BlockSpec((nseq, SHORT_CONV - 1, HW), lambda g, hg: (g, 0, off * NHG + hg))
    wcol = lambda off: pl.BlockSpec((SHORT_CONV, HW), lambda g, hg: (0, off * NHG + hg))
    sspec = pl.BlockSpec((nseq, HG, HEAD_K, HEAD_V), lambda g, hg: (g, hg, 0, 0))
    return pl.pallas_call(
        _gdn_sample_kernel,
        grid=(nb // nseq, NHG),
        in_specs=[col(0), col(1), col(2), st(0), st(1), st(2), col(3),
                  pl.BlockSpec((CHUNK, 128), lambda g, hg: (rb0 + g, hg)),
                  wcol(0), wcol(1), wcol(2),
                  pl.BlockSpec((1, HEAD_V), lambda g, hg: (0, 0)),
                  sspec],
        out_specs=[pl.BlockSpec((CHUNK, HW), lambda g, hg: (g, hg)), sspec],
        out_shape=[jax.ShapeDtypeStruct((nb * SEG_S, VAL_DIM), bf16), jax.ShapeDtypeStruct(s0.shape, f32)],
        scratch_shapes=[pltpu.VMEM((nseq, 2 * SEG_S, HW), f32)],
        compiler_params=_params("parallel", "parallel"),
        name="gdn_sample",
    )(qkvz, qkvz, qkvz, qkv_state, qkv_state, qkv_state, qkvz, ba, w_conv, w_conv, w_conv, onorm, s0)


CONV_LC = 128
CONV_RB = 64


def _ln_silu(c, g, b):
    mu = jnp.mean(c, axis=-1, keepdims=True)
    xc = c - mu
    y = xc * lax.rsqrt(jnp.mean(xc * xc, axis=-1, keepdims=True) + EPS) * g + b
    return _silu(y).astype(bf16)


CONV_PAD = 32


def _gates_conv_kernel(u_ref, wg_ref, x_ref, cw_ref, cb_ref, g_ref, c_ref, ext, *, tiles_per_seq):
    j, i = pl.program_id(0), pl.program_id(1)
    tm, cw = x_ref.shape

    @pl.when((i == 0) & (j == 0))
    def _():
        ext[:, tm:, :] = jnp.zeros((cw // CONV_LC, CONV_PAD, CONV_LC), f32)

    first = (i % tiles_per_seq) == 0
    off = CONV_PAD - (CONV_WIDTH - 1)
    nlc = cw // CONV_LC
    gw = g_ref.shape[1] // nlc
    for lc in range(nlc):
        gs = slice(lc * gw, (lc + 1) * gw)
        g_ref[:, gs] = _sigmoid(_dot(u_ref[...], wg_ref[:, gs])).astype(g_ref.dtype)
        ls = slice(lc * CONV_LC, (lc + 1) * CONV_LC)
        ext[lc, 0:CONV_PAD, :] = jnp.where(first, 0.0, ext[lc, tm:tm + CONV_PAD, :])
        ext[lc, CONV_PAD:, :] = x_ref[:, ls]
        for r0 in range(0, tm, CONV_RB):
            acc = jnp.broadcast_to(cb_ref[:, ls], (CONV_RB, CONV_LC))
            for t in range(CONV_WIDTH):
                acc += cw_ref[t:t + 1, ls] * ext[lc, r0 + off + t:r0 + off + t + CONV_RB, :]
            c_ref[r0:r0 + CONV_RB, ls] = acc


def _gates_conv_prompt(u, w_tail, glu, cw, cb, nb, seq, tm=1024, tn=1024):
    k = u.shape[1]
    ng = 2 * D_MODEL // tn
    cwid = CONV_CH // ng
    j0 = (O_GATE - O_GLU) // tn
    return pl.pallas_call(
        functools.partial(_gates_conv_kernel, tiles_per_seq=seq // tm),
        grid=(ng, nb * seq // tm),
        in_specs=[pl.BlockSpec((tm, k), lambda j, i: (i, 0)),
                  pl.BlockSpec((k, tn), lambda j, i: (0, j + j0)),
                  pl.BlockSpec((tm, cwid), lambda j, i: (i, j)),
                  pl.BlockSpec((CONV_WIDTH, cwid), lambda j, i: (0, j)),
                  pl.BlockSpec((1, cwid), lambda j, i: (0, j))],
        out_specs=[pl.BlockSpec((tm, tn), lambda j, i: (i, j)), pl.BlockSpec((tm, cwid), lambda j, i: (i, j))],
        out_shape=[jax.ShapeDtypeStruct((nb * seq, 2 * D_MODEL), bf16),
                   jax.ShapeDtypeStruct((nb * seq, CONV_CH), f32)],
        scratch_shapes=[pltpu.VMEM((cwid // CONV_LC, tm + CONV_PAD, CONV_LC), f32)],
        compiler_params=_params("arbitrary", "arbitrary"),
        name="gates_conv",
    )(u, w_tail, glu, cw, cb)


def _cconv_sample_kernel(x_ref, st_ref, w_ref, b_ref, g_ref, beta_ref, o_ref, e_scr, cbuf):
    nseq = CHUNK // SEG_S
    hist = CONV_WIDTH - 1
    e_scr[:, 0:hist, :] = st_ref[...]
    e_scr[:, hist:hist + SEG_S, :] = x_ref[...].reshape(nseq, SEG_S, CONV_CH)
    for lc in range(CONV_CH // CONV_LC):
        ls = slice(lc * CONV_LC, (lc + 1) * CONV_LC)
        acc = jnp.broadcast_to(b_ref[:, ls], (CHUNK, CONV_LC))
        for j in range(CONV_WIDTH):
            acc += w_ref[j:j + 1, ls] * e_scr[:, j:j + SEG_S, ls].reshape(CHUNK, CONV_LC)
        cbuf[:, ls] = acc
    o_ref[...] = _ln_silu(cbuf[...], g_ref[...], beta_ref[...])


def _cconv_sample(glu, glu_state, w, b, g, beta, row0):
    nb, hist, _ = glu_state.shape
    nseq = CHUNK // SEG_S
    rb0 = row0 // CHUNK
    vec = pl.BlockSpec((1, CONV_CH), lambda i: (0, 0))
    return pl.pallas_call(
        _cconv_sample_kernel,
        grid=(nb // nseq,),
        in_specs=[pl.BlockSpec((CHUNK, CONV_CH), lambda i: (rb0 + i, 0)),
                  pl.BlockSpec((nseq, hist, CONV_CH), lambda i: (i, 0, 0)),
                  pl.BlockSpec((CONV_WIDTH, CONV_CH), lambda i: (0, 0)), vec, vec, vec],
        out_specs=pl.BlockSpec((CHUNK, CONV_CH), lambda i: (i, 0)),
        out_shape=jax.ShapeDtypeStruct((nb * SEG_S, CONV_CH), bf16),
        scratch_shapes=[pltpu.VMEM((nseq, hist + SEG_S, CONV_CH), f32), pltpu.VMEM((CHUNK, CONV_CH), f32)],
        compiler_params=_params("parallel"),
        name="cconv_sample",
    )(glu, glu_state, w, b, g, beta)


def _merge_kernel(oap_ref, oas_ref, cp_ref, cs_ref, gap_ref, gas_ref, gbp_ref, gbs_ref, h_ref,
                  wa_ref, wb_ref, wo_ref, lng_ref, lnb_ref, gpost_ref, o_ref, *, na):
    def run(oa_ref, c, ga_ref, gb_ref):
        merged = (ga_ref[...].astype(f32) * _dot(oa_ref[...], wa_ref[...])
                  + gb_ref[...].astype(f32) * _dot(c, wb_ref[...]))
        o_ref[...] = h_ref[...] + _rms(_dot(merged.astype(bf16), wo_ref[...]), gpost_ref[...])

    i = pl.program_id(0)
    pl.when(i < na)(lambda: run(oap_ref, _ln_silu(cp_ref[...], lng_ref[...], lnb_ref[...]), gap_ref, gbp_ref))
    pl.when(i >= na)(lambda: run(oas_ref, cs_ref[...], gas_ref, gbs_ref))


def _merge(oa_p, oa_s, c_p, c_s, gates_p, gates_s, h, wa, wb, wo, ln_g, ln_b, gpost, tm=256):
    t = h.shape[0]
    na = oa_p.shape[0] // tm
    act = pl.BlockSpec((tm, D_MODEL), lambda i: (i, 0))
    vec = pl.BlockSpec((1, D_MODEL), lambda i: (0, 0))
    src = lambda col: _split_specs((tm, D_MODEL), na, 2, col)
    wsp = pl.BlockSpec((D_MODEL, D_MODEL), lambda i: (0, 0), pipeline_mode=pl.Buffered(1))
    return pl.pallas_call(
        functools.partial(_merge_kernel, na=na),
        grid=(t // tm,),
        in_specs=src(0) + src(0) + src(0) + src(1) + [act, wsp, wsp, wsp, vec, vec, vec],
        out_specs=act,
        out_shape=jax.ShapeDtypeStruct((t, D_MODEL), f32),
        compiler_params=_params("parallel"),
        name="merge",
    )(oa_p, oa_s, c_p, c_s, gates_p, gates_s, gates_p, gates_s, h, wa, wb, wo, ln_g, ln_b, gpost)


def _ple_kernel(hn_ref, pa_ref, pb_ref, h_ref, wg_ref, wp_ref, gpost_ref, oa_ref, ob_ref, *, na):
    i = pl.program_id(0)
    gate = _sigmoid(_dot(hn_ref[...], wg_ref[...]))

    def out(p_ref):
        v = gate * _dot(p_ref[...].astype(bf16), wp_ref[...])
        return h_ref[...] + _rms(v, gpost_ref[...])

    @pl.when(i < na)
    def _():
        oa_ref[...] = out(pa_ref)

    @pl.when(i >= na)
    def _():
        ob_ref[...] = out(pb_ref)


def _ple(hn, pa, pb, h, wg, wp, gpost, tm=512):
    na, nb = pa.shape[0] // tm, pb.shape[0] // tm
    row = lambda i: (i, 0)
    first = lambda i: (jnp.minimum(i, na - 1), 0)
    second = lambda i: (jnp.maximum(i - na, 0), 0)
    act = pl.BlockSpec((tm, D_MODEL), row)
    return pl.pallas_call(
        functools.partial(_ple_kernel, na=na),
        grid=(na + nb,),
        in_specs=[act, pl.BlockSpec((tm, PLE_DIM), first), pl.BlockSpec((tm, PLE_DIM), second), act,
                  pl.BlockSpec((D_MODEL, D_MODEL), lambda i: (0, 0), pipeline_mode=pl.Buffered(1)),
                  pl.BlockSpec((PLE_DIM, D_MODEL), lambda i: (0, 0)),
                  pl.BlockSpec((1, D_MODEL), lambda i: (0, 0))],
        out_specs=[pl.BlockSpec((tm, D_MODEL), first), pl.BlockSpec((tm, D_MODEL), second)],
        out_shape=[jax.ShapeDtypeStruct((na * tm, D_MODEL), f32), jax.ShapeDtypeStruct((nb * tm, D_MODEL), f32)],
        compiler_params=_params("arbitrary"),
        name="ple",
    )(hn, pa, pb, h, wg, wp, gpost)


def _layer(xp, xs, pp, ps, nb, seq, s0, qkv_buf, glu_buf, w):
    (ffn1_pre, ffn1_w_gu, ffn1_w_down, ffn1_post, mix_pre, w_in, w_short_conv, a_log, dt_bias,
     o_norm, w_dw_conv, b_dw_conv, ln_g, ln_b, w_branch_a, w_branch_b, w_out, mix_post,
     ffn2_pre, ffn2_w_gu, ffn2_w_down, ffn2_post, ple_pre, w_ple_gate, w_ple_proj,
     ple_post) = w
    tp = nb * seq
    ns = s0.shape[0]
    row = lambda v: v.astype(f32).reshape(1, -1)
    cast = lambda v: v.astype(bf16)

    h1, u = _ffn([xp, xs], row(ffn1_pre), cast(ffn1_w_gu), cast(ffn1_w_down), row(ffn1_post), row(mix_pre))

    w_qkvz, w_tail, w_mid = _cast_w_in(w_in)
    qkvz = _proj(u, w_qkvz, 0, O_BETA, "none", f32)
    ba = _proj_ba(u, _ba_weight(w_mid[:, :O_GLU - O_BETA]), _group_lanes(a_log), _group_lanes(dt_bias))
    glu = _proj_glu(u, w_tail)

    w_conv = w_short_conv.astype(f32)
    onorm = row(o_norm)
    oa_p, s_p = _gdn_prompt(qkvz, ba, w_conv, onorm, nb, seq)
    oa_s, s_s = _gdn_sample(qkvz, qkv_buf.astype(f32), ba, w_conv, onorm, s0.astype(f32), tp)

    cw, cb, lg, lb = w_dw_conv.astype(f32), row(b_dw_conv), row(ln_g), row(ln_b)
    gates_p, c_p = _gates_conv_prompt(u, w_tail, glu, cw, cb, nb, seq)
    gates_s = _proj(u, w_tail, O_GATE - O_GLU, 2 * D_MODEL, "sigmoid", bf16, r0=tp, rows=xs.shape[0])
    c_s = _cconv_sample(glu, glu_buf.astype(f32), cw, cb, lg, lb, tp)

    h2 = _merge(oa_p, oa_s, c_p, c_s, gates_p, gates_s, h1, cast(w_branch_a), cast(w_branch_b), cast(w_out),
                lg, lb, row(mix_post))
    h3, hn = _ffn([h2], row(ffn2_pre), cast(ffn2_w_gu), cast(ffn2_w_down), row(ffn2_post), row(ple_pre))
    yp, ys = _ple(hn, pp, ps, h3, cast(w_ple_gate), cast(w_ple_proj), row(ple_post))

    tail = lambda a, width, n: jnp.stack([a[(b + 1) * seq - n:(b + 1) * seq, :width] for b in range(nb)])
    qkv_p = tail(qkvz, QKV_DIM, SHORT_CONV - 1)
    glu_p = tail(glu, CONV_CH, CONV_WIDTH - 1)
    qkv_s = qkvz[tp:, :QKV_DIM].reshape(ns, SEG_S, QKV_DIM)[:, SEG_S - (SHORT_CONV - 1):]
    glu_s = jnp.concatenate([glu_buf.astype(f32)[:, SEG_S:], glu[tp:].reshape(ns, SEG_S, CONV_CH)], axis=1)
    return (yp, ys, s_p, qkv_p, glu_p, s_s, qkv_s, glu_s)


def kernel(x_prompt, x_sample, p_prompt, p_sample, state_delta, state_qkv_conv, state_glu_conv, ffn1_pre, ffn1_w_gu, ffn1_w_down, ffn1_post, mix_pre, w_in, w_short_conv, a_log, dt_bias, o_norm, w_dw_conv, b_dw_conv, ln_g, ln_b, w_branch_a, w_branch_b, w_out, mix_post, ffn2_pre, ffn2_w_gu, ffn2_w_down, ffn2_post, ple_pre, w_ple_gate, w_ple_proj, ple_post):
    weights = (ffn1_pre, ffn1_w_gu, ffn1_w_down, ffn1_post, mix_pre, w_in, w_short_conv, a_log,
               dt_bias, o_norm, w_dw_conv, b_dw_conv, ln_g, ln_b, w_branch_a, w_branch_b,
               w_out, mix_post, ffn2_pre, ffn2_w_gu, ffn2_w_down, ffn2_post, ple_pre,
               w_ple_gate, w_ple_proj, ple_post)
    nb, seq, _ = x_prompt.shape
    ns, ls, _ = x_sample.shape
    assert ls == SEG_S and seq % CHUNK == 0 and ns % (CHUNK // SEG_S) == 0
    depth = ffn1_pre.shape[0]
    tp = nb * seq
    xp, xs = x_prompt.reshape(tp, D_MODEL), x_sample.reshape(ns * ls, D_MODEL)
    outs = [[] for _ in range(6)]
    for i in range(depth):
        wi = tuple(wt[i:i + 1] if wt is w_in else wt[i] for wt in weights)
        xp, xs, s_p, q_p, g_p, s_s, q_s, g_s = _layer(
            xp, xs, p_prompt[i].reshape(tp, PLE_DIM), p_sample[i].reshape(ns * ls, PLE_DIM), nb, seq,
            state_delta[i], state_qkv_conv[i], state_glu_conv[i], wi)
        for lst, v in zip(outs, (s_p, q_p, g_p, s_s, q_s, g_s)):
            lst.append(v)
    return (xp.reshape(nb, seq, D_MODEL), xs.reshape(ns, ls, D_MODEL)) + tuple(jnp.stack(lst) for lst in outs)
```

```python
import functools

import jax
import jax.numpy as jnp
from jax import lax
from jax.experimental import pallas as pl
from jax.experimental.pallas import tpu as pltpu

f32 = jnp.float32
bf16 = jnp.bfloat16

D_MODEL = 2048
N_HEADS = 16
HEAD_K = 128
HEAD_V = 128
KEY_DIM = N_HEADS * HEAD_K
VAL_DIM = N_HEADS * HEAD_V
QKV_DIM = 2 * KEY_DIM + VAL_DIM
SHORT_CONV = 4
CONV_CH = D_MODEL
CONV_WIDTH = 31
D_FF = 5632
PLE_DIM = 256
EPS = 1e-6

O_Z = QKV_DIM
O_BETA = O_Z + VAL_DIM
O_A = O_BETA + N_HEADS
O_GLU = O_A + N_HEADS
O_GATE = O_GLU + 2 * CONV_CH

CHUNK = 64
HG = 8
NHG = N_HEADS // HG
HW = HG * HEAD_K
SEG_S = 8
LOCAL_CHUNKS = 4

VMEM_LIMIT = 56 * 1024 * 1024


def _sigmoid(x):
    return 1.0 / (1.0 + jnp.exp(-x))


def _silu(x):
    return x * _sigmoid(x)


def _rms(x, g):
    return x * lax.rsqrt(jnp.mean(x * x, axis=-1, keepdims=True) + EPS) * g


def _dot(a, b):
    return jnp.dot(a, b, preferred_element_type=f32)


def _dot_nt(a, b):
    return lax.dot_general(a, b, (((1,), (1,)), ((), ())), preferred_element_type=f32)


def _params(*sem):
    return pltpu.CompilerParams(dimension_semantics=sem, vmem_limit_bytes=VMEM_LIMIT)


def _per_source(i, na, refs, fn):
    if len(refs) == 1:
        fn(refs[0])
    else:
        pl.when(i < na)(lambda: fn(refs[0]))
        pl.when(i >= na)(lambda: fn(refs[1]))


def _split_specs(block, na, nsrc, col=0):
    if nsrc == 1:
        return [pl.BlockSpec(block, lambda i, *_: (i, col))]
    return [pl.BlockSpec(block, lambda i, *_: (jnp.minimum(i, na - 1), col)),
            pl.BlockSpec(block, lambda i, *_: (jnp.maximum(i - na, 0), col))]


def _ffn_kernel(*refs, na, nsrc):
    x_refs = refs[:nsrc]
    gpre_ref, wg_ref, wu_ref, wd_ref, gpost_ref, gnext_ref, h_ref, hn_ref, xn_scr, acc_scr = refs[nsrc:]
    i, j = pl.program_id(0), pl.program_id(1)

    @pl.when(j == 0)
    def _():
        def norm_in(x_ref):
            xn_scr[...] = _rms(x_ref[...], gpre_ref[...]).astype(bf16)

        _per_source(i, na, x_refs, norm_in)
        acc_scr[...] = jnp.zeros_like(acc_scr)

    xn = xn_scr[...]
    a = _silu(_dot(xn, wg_ref[...])) * _dot(xn, wu_ref[...])
    acc_scr[...] += _dot(a.astype(bf16), wd_ref[...])

    @pl.when(j == pl.num_programs(1) - 1)
    def _():
        def residual_out(x_ref):
            h = x_ref[...] + 0.5 * _rms(acc_scr[...], gpost_ref[...])
            h_ref[...] = h
            hn_ref[...] = _rms(h, gnext_ref[...]).astype(bf16)

        _per_source(i, na, x_refs, residual_out)


def _ffn(xs, gpre, w_gu, w_down, gpost, gnext, tm=512, tf=512):
    na = xs[0].shape[0] // tm
    nt = sum(x.shape[0] for x in xs) // tm
    nf = D_FF // tf
    row = lambda i, j: (i, 0)
    vec = pl.BlockSpec((1, D_MODEL), lambda i, j: (0, 0))
    return pl.pallas_call(
        functools.partial(_ffn_kernel, na=na, nsrc=len(xs)),
        grid=(nt, nf),
        in_specs=_split_specs((tm, D_MODEL), na, len(xs))
        + [vec,
           pl.BlockSpec((D_MODEL, tf), lambda i, j: (0, j)),
           pl.BlockSpec((D_MODEL, tf), lambda i, j: (0, j + nf)),
           pl.BlockSpec((tf, D_MODEL), lambda i, j: (j, 0)), vec, vec],
        out_specs=[pl.BlockSpec((tm, D_MODEL), row), pl.BlockSpec((tm, D_MODEL), row)],
        out_shape=[jax.ShapeDtypeStruct((nt * tm, D_MODEL), f32), jax.ShapeDtypeStruct((nt * tm, D_MODEL), bf16)],
        scratch_shapes=[pltpu.VMEM((tm, D_MODEL), bf16), pltpu.VMEM((tm, D_MODEL), f32)],
        compiler_params=_params("parallel", "arbitrary"),
        name="ffn",
    )(*xs, gpre, w_gu, w_gu, w_down, gpost, gnext)


CAST_TN = 512
W_IN_SHIFT = O_GLU - O_BETA


def _cast_w_in_kernel(a_ref, b0_ref, b1_ref, oa_ref, ob_ref, oc_ref):
    oa_ref[...] = a_ref[...].astype(bf16)

    @pl.when(pl.program_id(0) == 0)
    def _():
        oc_ref[...] = b0_ref[:, 0:128].astype(bf16)

    keep = CAST_TN - W_IN_SHIFT
    lane = lax.broadcasted_iota(jnp.int32, b0_ref.shape, 1)
    moved = jnp.where(lane < keep, pltpu.roll(b0_ref[...], keep, 1), pltpu.roll(b1_ref[...], keep, 1))
    ob_ref[...] = moved.astype(bf16)


def _cast_w_in(w_in):
    n_tail = w_in.shape[2] - O_GLU
    assert O_BETA % CAST_TN == 0 and n_tail == O_BETA and 0 < W_IN_SHIFT < CAST_TN
    nb = O_BETA // CAST_TN
    src = lambda off: pl.BlockSpec((None, D_MODEL, CAST_TN), lambda j: (0, 0, j + off))
    blk = pl.BlockSpec((D_MODEL, CAST_TN), lambda j: (0, j))
    return pl.pallas_call(
        _cast_w_in_kernel,
        grid=(nb,),
        in_specs=[src(0), src(nb), src(nb + 1)],
        out_specs=[blk, blk, pl.BlockSpec((D_MODEL, 128), lambda j: (0, 0))],
        out_shape=[jax.ShapeDtypeStruct((D_MODEL, O_BETA), bf16), jax.ShapeDtypeStruct((D_MODEL, n_tail), bf16),
                   jax.ShapeDtypeStruct((D_MODEL, 128), bf16)],
        compiler_params=_params("arbitrary"),
        name="cast_w_in",
    )(w_in, w_in, w_in)


def _proj_kernel(x_ref, wt_ref, o_ref, *, act):
    y = _dot_nt(x_ref[...], wt_ref[...])
    if act == "sigmoid":
        y = _sigmoid(y)
    o_ref[...] = y.astype(o_ref.dtype)


def _proj(x, wt, n0, n, act, out_dtype, r0=0, rows=None, tm=1024, tn=1024):
    t, k = x.shape
    rows = t if rows is None else rows
    i0, j0 = r0 // tm, n0 // tn
    return pl.pallas_call(
        functools.partial(_proj_kernel, act=act),
        grid=(rows // tm, n // tn),
        in_specs=[pl.BlockSpec((tm, k), lambda i, j: (i + i0, 0)),
                  pl.BlockSpec((tn, k), lambda i, j: (j + j0, 0))],
        out_specs=pl.BlockSpec((tm, tn), lambda i, j: (i, j)),
        out_shape=jax.ShapeDtypeStruct((rows, n), out_dtype),
        compiler_params=_params("parallel", "arbitrary"),
        name="proj_" + act,
    )(x, wt)


def _glu_kernel(x_ref, wa_ref, wb_ref, o_ref):
    x = x_ref[...]
    o_ref[...] = _dot_nt(x, wa_ref[...]) * _sigmoid(_dot_nt(x, wb_ref[...]))


def _proj_glu(x, wt, tm=1024, tn=512):
    t, k = x.shape
    nj = CONV_CH // tn
    return pl.pallas_call(
        _glu_kernel,
        grid=(t // tm, nj),
        in_specs=[pl.BlockSpec((tm, k), lambda i, j: (i, 0)),
                  pl.BlockSpec((tn, k), lambda i, j: (j, 0)),
                  pl.BlockSpec((tn, k), lambda i, j: (j + nj, 0))],
        out_specs=pl.BlockSpec((tm, tn), lambda i, j: (i, j)),
        out_shape=jax.ShapeDtypeStruct((t, CONV_CH), f32),
        compiler_params=_params("parallel", "arbitrary"),
        name="proj_glu",
    )(x, wt, wt)


def _ba_kernel(x_ref, w_ref, alog_ref, dtb_ref, o_ref):
    y = _dot(x_ref[...], w_ref[...])
    lane = lax.broadcasted_iota(jnp.int32, y.shape, 1)
    ya = y + dtb_ref[...]
    softplus = jnp.maximum(ya, 0.0) + jnp.log(1.0 + jnp.exp(-jnp.abs(ya)))
    o_ref[...] = jnp.where(lane < HG, _sigmoid(y), -jnp.exp(alog_ref[...]) * softplus)


def _proj_ba(x, w_ba, alog_row, dtb_row, tm=1024):
    t, k = x.shape
    vec = pl.BlockSpec((1, 128), lambda i, j: (0, j))
    return pl.pallas_call(
        _ba_kernel,
        grid=(t // tm, NHG),
        in_specs=[pl.BlockSpec((tm, k), lambda i, j: (i, 0)),
                  pl.BlockSpec((k, 128), lambda i, j: (0, j)), vec, vec],
        out_specs=pl.BlockSpec((tm, 128), lambda i, j: (i, j)),
        out_shape=jax.ShapeDtypeStruct((t, NHG * 128), f32),
        compiler_params=_params("parallel", "arbitrary"),
        name="proj_ba",
    )(x, w_ba, alog_row, dtb_row)


def _group_lanes(v):
    r = jnp.zeros((NHG, 128), f32).at[:, HG:2 * HG].set(v.astype(f32).reshape(NHG, HG))
    return r.reshape(1, NHG * 128)


def _ba_weight(w_ba):
    wb = w_ba[:, :N_HEADS].reshape(D_MODEL, NHG, HG)
    wa = w_ba[:, N_HEADS:].reshape(D_MODEL, NHG, HG)
    w = jnp.zeros((D_MODEL, NHG, 128), w_ba.dtype).at[:, :, :HG].set(wb).at[:, :, HG:2 * HG].set(wa)
    return w.reshape(D_MODEL, NHG * 128)


def _seq_masks(n, seg):
    r = lax.broadcasted_iota(jnp.int32, (n, n), 0)
    c = lax.broadcasted_iota(jnp.int32, (n, n), 1)
    same = (r // seg) == (c // seg)
    return same & (r >= c), same & (r > c), same


def _mask_dot(mask, x):
    l = jnp.where(mask, 1.0, 0.0).astype(bf16)
    hi = x.astype(bf16)
    r1 = x - hi.astype(f32)
    mid = r1.astype(bf16)
    lo = (r1 - mid.astype(f32)).astype(bf16)
    return _dot(l, hi) + _dot(l, mid) + _dot(l, lo)


def _mm(a, b):
    return _dot(a.astype(bf16), b.astype(bf16))


def _unit_lower_solve(ms, rhss, seg):
    n = range(len(ms))
    r = lax.broadcasted_iota(jnp.int32, (CHUNK, CHUNK), 0)
    c = lax.broadcasted_iota(jnp.int32, (CHUNK, CHUNK), 1)
    eye = (r == c).astype(f32)
    if seg <= 8:
        d, lo = ms, None
    else:
        blk = (r // 16) == (c // 16)
        d = [jnp.where(blk, m, 0.0) for m in ms]
        lo = [ms[i] - d[i] for i in n]
    d2 = [_mm(d[i], d[i]) for i in n]
    d3 = [_mm(d[i], d2[i]) for i in n]
    d4 = [_mm(d2[i], d2[i]) for i in n]
    a = [eye - d[i] + d2[i] - d3[i] for i in n]
    td = [a[i] + _mm(a[i], d4[i]) for i in n]
    if seg > 8:
        d8 = [_mm(d4[i], d4[i]) for i in n]
        td = [td[i] + _mm(td[i], d8[i]) for i in n]
    y = [_mm(td[i], rhss[i]) for i in n]
    if lo is None:
        return y
    e = [_mm(td[i], lo[i]) for i in n]
    e2 = [_mm(e[i], e[i]) for i in n]
    w = [y[i] - _mm(e[i], y[i]) for i in n]
    return [w[i] + _mm(e2[i], w[i]) for i in n]


def _chunk_local(qs, ks, vs, betas, gcols, grows, gtots, incl, strict, seg):
    n = range(len(qs))
    qn = [q * lax.rsqrt(jnp.sum(q * q, axis=-1, keepdims=True) + EPS) * (HEAD_K ** -0.5) for q in qs]
    kn = [k * lax.rsqrt(jnp.sum(k * k, axis=-1, keepdims=True) + EPS) for k in ks]
    decay = [jnp.exp(jnp.where(incl, gcols[i] - grows[i], -jnp.inf)) for i in n]
    kb = [k.astype(bf16) for k in kn]
    kk = [_dot_nt(kb[i], kb[i]) for i in n]
    qk = [_dot_nt(qn[i].astype(bf16), kb[i]) for i in n]
    m = [jnp.where(strict, betas[i] * decay[i] * kk[i], 0.0) for i in n]
    gamma = [jnp.exp(g) for g in gcols]
    rhs = [jnp.concatenate([betas[i] * vs[i], (betas[i] * gamma[i]) * kn[i]], axis=1) for i in n]
    sol = _unit_lower_solve(m, rhs, seg)
    uv = [s[:, :HEAD_V] for s in sol]
    wk = [s[:, HEAD_V:] for s in sol]
    qkd = [qk[i] * decay[i] for i in n]
    qg = [qn[i] * gamma[i] for i in n]
    kd = [kn[i] * jnp.exp(gtots[i] - gcols[i]) for i in n]
    return uv, wk, qkd, qg, kd


def _out_gate(o, onorm, z):
    return (_rms(o, onorm) * _silu(z)).astype(bf16)


def _gdn_prompt_kernel(q_ref, k_ref, v_ref, z_ref, ba_ref, wq_ref, wk_ref, wv_ref, onorm_ref,
                       oa_ref, s_ref,
                       eq, ek, ev, s_scr, qc_s, kc_s, vc_s, gc_s, gt_s, gl_s, uv_s, wq_s, kdt_s, qkd_s,
                       *, rows):
    t = pl.program_id(2)
    nc = rows // CHUNK
    heads = range(HG)
    hs = [slice(h * HEAD_K, (h + 1) * HEAD_K) for h in heads]
    gs = [slice(HG + h, HG + h + 1) for h in heads]

    @pl.when(t == 0)
    def _():
        s_scr[...] = jnp.zeros_like(s_scr)
        for e in (eq, ek, ev):
            e[:, 0:8, :] = jnp.zeros((HG, 8, HEAD_K), f32)

    off = 8 - (SHORT_CONV - 1)
    for e, x_ref, w_ref, dst in ((eq, q_ref, wq_ref, qc_s), (ek, k_ref, wk_ref, kc_s), (ev, v_ref, wv_ref, vc_s)):
        for h in heads:
            e[h, 8:, :] = x_ref[:, hs[h]]
            for r0 in range(0, rows, CHUNK):
                acc = w_ref[0:1, hs[h]] * e[h, r0 + off:r0 + off + CHUNK, :]
                for j in range(1, SHORT_CONV):
                    acc += w_ref[j:j + 1, hs[h]] * e[h, r0 + off + j:r0 + off + j + CHUNK, :]
                dst[r0:r0 + CHUNK, hs[h]] = _silu(acc)

    incl_b, _, same_b = _seq_masks(rows, CHUNK)
    ba = ba_ref[...]
    gc_s[...] = _mask_dot(incl_b, ba)
    gtot_b = _mask_dot(same_b, ba)
    gt_s[...] = gtot_b
    gl_s[...] = jnp.exp(gtot_b)
    incl, strict, _ = _seq_masks(CHUNK, CHUNK)

    def local(cp, carry):
        cs = [cp * LOCAL_CHUNKS + i for i in range(LOCAL_CHUNKS)]
        rss = [pl.ds(pl.multiple_of(c * CHUNK, CHUNK), CHUNK) for c in cs]
        units = [(i, h) for i in range(LOCAL_CHUNKS) for h in heads]
        bac = [ba_ref[rs, :] for rs in rss]
        gc = [gc_s[rs, :] for rs in rss]
        gtot = [gt_s[rs, :] for rs in rss]
        gct = [g.T for g in gc]
        uv, wk, qkd, qg, kd = _chunk_local(
            [qc_s[rss[i], hs[h]] for i, h in units], [kc_s[rss[i], hs[h]] for i, h in units],
            [vc_s[rss[i], hs[h]] for i, h in units], [bac[i][:, h:h + 1] for i, h in units],
            [gc[i][:, gs[h]] for i, h in units], [gct[i][gs[h], :] for i, h in units],
            [gtot[i][:, gs[h]] for i, h in units], incl, strict, CHUNK)
        for n, (i, h) in enumerate(units):
            uv_s[rss[i], hs[h]] = uv[n]
            wq_s[cs[i], 0:CHUNK, hs[h]] = wk[n].astype(bf16)
            wq_s[cs[i], CHUNK:2 * CHUNK, hs[h]] = qg[n].astype(bf16)
            kdt_s[cs[i], h] = kd[n].T.astype(bf16)
            qkd_s[cs[i], h] = qkd[n].astype(bf16)
        return carry

    lax.fori_loop(0, nc // LOCAL_CHUNKS, local, 0)

    def recur(c, carry):
        r0 = pl.multiple_of(c * CHUNK, CHUNK)
        rs = pl.ds(r0, CHUNK)
        s = [s_scr[h] for h in heads]
        p = [_dot(wq_s[c, :, hs[h]], s[h].astype(bf16)) for h in heads]
        u = [(uv_s[rs, hs[h]] - p[h][:CHUNK]).astype(bf16) for h in heads]
        o = [p[h][CHUNK:] + _dot(qkd_s[c, h], u[h]) for h in heads]
        ds = [_dot(kdt_s[c, h], u[h]) for h in heads]
        for h in heads:
            s_scr[h] = gl_s[pl.ds(r0, 1), gs[h]] * s[h] + ds[h]
            oa_ref[rs, hs[h]] = _out_gate(o[h], onorm_ref[...], z_ref[rs, hs[h]])
        return carry

    lax.fori_loop(0, nc, recur, 0)

    for e in (eq, ek, ev):
        e[:, 0:8, :] = e[:, rows:rows + 8, :]

    @pl.when(t == pl.num_programs(2) - 1)
    def _():
        s_ref[0] = s_scr[...]


def _gdn_prompt(qkvz, ba, w_conv, onorm, nb, seq, rows=256):
    nt = seq // rows
    nc = rows // CHUNK
    rb = lambda b, hg, t: b * nt + t
    col = lambda off: pl.BlockSpec((rows, HW), lambda b, hg, t: (rb(b, hg, t), off * NHG + hg))
    wcol = lambda off: pl.BlockSpec((SHORT_CONV, HW), lambda b, hg, t: (0, off * NHG + hg))
    return pl.pallas_call(
        functools.partial(_gdn_prompt_kernel, rows=rows),
        grid=(nb, NHG, nt),
        in_specs=[col(0), col(1), col(2), col(3),
                  pl.BlockSpec((rows, 128), lambda b, hg, t: (rb(b, hg, t), hg)),
                  wcol(0), wcol(1), wcol(2),
                  pl.BlockSpec((1, HEAD_V), lambda b, hg, t: (0, 0))],
        out_specs=[pl.BlockSpec((rows, HW), lambda b, hg, t: (rb(b, hg, t), hg)),
                   pl.BlockSpec((1, HG, HEAD_K, HEAD_V), lambda b, hg, t: (b, hg, 0, 0))],
        out_shape=[jax.ShapeDtypeStruct((nb * seq, VAL_DIM), bf16),
                   jax.ShapeDtypeStruct((nb, N_HEADS, HEAD_K, HEAD_V), f32)],
        scratch_shapes=[pltpu.VMEM((HG, rows + 8, HEAD_K), f32)] * 3
        + [pltpu.VMEM((HG, HEAD_K, HEAD_V), f32)]
        + [pltpu.VMEM((rows, HW), f32)] * 3
        + [pltpu.VMEM((rows, 128), f32)] * 3
        + [pltpu.VMEM((rows, HW), f32),
           pltpu.VMEM((nc, 2 * CHUNK, HW), bf16),
           pltpu.VMEM((nc, HG, HEAD_K, CHUNK), bf16),
           pltpu.VMEM((nc, HG, CHUNK, CHUNK), bf16)],
        compiler_params=_params("parallel", "parallel", "arbitrary"),
        name="gdn_prompt",
    )(qkvz, qkvz, qkvz, qkvz, ba, w_conv, w_conv, w_conv, onorm)


def _gdn_sample_kernel(q_ref, k_ref, v_ref, qst_ref, kst_ref, vst_ref, z_ref, ba_ref, wq_ref, wk_ref, wv_ref,
                       onorm_ref, s0_ref, oa_ref, s_ref, e_scr):
    nseq = CHUNK // SEG_S
    heads = range(HG)
    seqs = range(nseq)
    hs = [slice(h * HEAD_K, (h + 1) * HEAD_K) for h in heads]
    gs = [slice(HG + h, HG + h + 1) for h in heads]
    js = [slice(j * SEG_S, (j + 1) * SEG_S) for j in seqs]
    incl, strict, same = _seq_masks(CHUNK, SEG_S)
    hist = SHORT_CONV - 1

    def conv(x_ref, st_ref, w_ref):
        e_scr[:, SEG_S - hist:SEG_S, :] = st_ref[...]
        e_scr[:, SEG_S:, :] = x_ref[...].reshape(nseq, SEG_S, HW)
        off = SEG_S - hist
        acc = w_ref[0:1, :] * e_scr[:, off:off + SEG_S, :].reshape(CHUNK, HW)
        for j in range(1, SHORT_CONV):
            acc += w_ref[j:j + 1, :] * e_scr[:, off + j:off + j + SEG_S, :].reshape(CHUNK, HW)
        return _silu(acc)

    qc, kc, vc = conv(q_ref, qst_ref, wq_ref), conv(k_ref, kst_ref, wk_ref), conv(v_ref, vst_ref, wv_ref)
    bac = ba_ref[...]
    gc = _mask_dot(incl, bac)
    gtot = _mask_dot(same, bac)
    gct = gc.T
    glv = jnp.exp(gtot)
    uv, wk, qkd, qg, kd = _chunk_local(
        [qc[:, hs[h]] for h in heads], [kc[:, hs[h]] for h in heads], [vc[:, hs[h]] for h in heads],
        [bac[:, h:h + 1] for h in heads], [gc[:, gs[h]] for h in heads], [gct[gs[h], :] for h in heads],
        [gtot[:, gs[h]] for h in heads], incl, strict, SEG_S)
    kdt = [kd[h].T.astype(bf16) for h in heads]
    p = [[_dot(jnp.concatenate([wk[h][js[j]], qg[h][js[j]]], axis=0).astype(bf16),
               s0_ref[j, h].astype(bf16)) for j in seqs] for h in heads]
    u = [[uv[h][js[j]] - p[h][j][:SEG_S] for j in seqs] for h in heads]
    zeros = jnp.zeros((SEG_S, HEAD_V), f32)
    for h in heads:
        for j in seqs:
            u_rows = jnp.concatenate([u[h][j] if i == j else zeros for i in seqs], axis=0).astype(bf16)
            s_ref[j, h] = glv[j * SEG_S:j * SEG_S + 1, gs[h]] * s0_ref[j, h] + _dot(kdt[h], u_rows)
    for h in heads:
        u_all = jnp.concatenate(u[h], axis=0).astype(bf16)
        o = jnp.concatenate([p[h][j][SEG_S:] for j in seqs], axis=0) + _dot(qkd[h].astype(bf16), u_all)
        oa_ref[:, hs[h]] = _out_gate(o, onorm_ref[...], z_ref[:, hs[h]])


def _gdn_sample(qkvz, qkv_state, ba, w_conv, onorm, s0, row0):
    nb = qkv_state.shape[0]
    nseq = CHUNK // SEG_S
    rb0 = row0 // CHUNK
    col = lambda off: pl.BlockSpec((CHUNK, HW), lambda g, hg: (rb0 + g, off * NHG + hg))
    st = lambda off: pl.BlockSpec((nseq, SHORT_CONV - 1, HW), lambda g, hg: (g, 0, off * NHG + hg))
    wcol = lambda off: pl.BlockSpec((SHORT_CONV, HW), lambda g, hg: (0, off * NHG + hg))
    sspec = pl.BlockSpec((nseq, HG, HEAD_K, HEAD_V), lambda g, hg: (g, hg, 0, 0))
    return pl.pallas_call(
        _gdn_sample_kernel,
        grid=(nb // nseq, NHG),
        in_specs=[col(0), col(1), col(2), st(0), st(1), st(2), col(3),
                  pl.BlockSpec((CHUNK, 128), lambda g, hg: (rb0 + g, hg)),
                  wcol(0), wcol(1), wcol(2),
                  pl.BlockSpec((1, HEAD_V), lambda g, hg: (0, 0)),
                  sspec],
        out_specs=[pl.BlockSpec((CHUNK, HW), lambda g, hg: (g, hg)), sspec],
        out_shape=[jax.ShapeDtypeStruct((nb * SEG_S, VAL_DIM), bf16), jax.ShapeDtypeStruct(s0.shape, f32)],
        scratch_shapes=[pltpu.VMEM((nseq, 2 * SEG_S, HW), f32)],
        compiler_params=_params("parallel", "parallel"),
        name="gdn_sample",
    )(qkvz, qkvz, qkvz, qkv_state, qkv_state, qkv_state, qkvz, ba, w_conv, w_conv, w_conv, onorm, s0)


CONV_LC = 128
CONV_RB = 64


def _ln_silu(c, g, b):
    mu = jnp.mean(c, axis=-1, keepdims=True)
    xc = c - mu
    y = xc * lax.rsqrt(jnp.mean(xc * xc, axis=-1, keepdims=True) + EPS) * g + b
    return _silu(y).astype(bf16)


CONV_PAD = 32


def _gates_conv_kernel(u_ref, wg_ref, x_ref, cw_ref, cb_ref, g_ref, c_ref, ext, *, tiles_per_seq):
    j, i = pl.program_id(0), pl.program_id(1)
    tm, cw = x_ref.shape

    @pl.when((i == 0) & (j == 0))
    def _():
        ext[:, tm:, :] = jnp.zeros((cw // CONV_LC, CONV_PAD, CONV_LC), f32)

    first = (i % tiles_per_seq) == 0
    off = CONV_PAD - (CONV_WIDTH - 1)
    nlc = cw // CONV_LC
    gw = g_ref.shape[1] // nlc
    for lc in range(nlc):
        gs = slice(lc * gw, (lc + 1) * gw)
        g_ref[:, gs] = _sigmoid(_dot_nt(u_ref[...], wg_ref[gs, :])).astype(g_ref.dtype)
        ls = slice(lc * CONV_LC, (lc + 1) * CONV_LC)
        ext[lc, 0:CONV_PAD, :] = jnp.where(first, 0.0, ext[lc, tm:tm + CONV_PAD, :])
        ext[lc, CONV_PAD:, :] = x_ref[:, ls]
        for r0 in range(0, tm, CONV_RB):
            acc = jnp.broadcast_to(cb_ref[:, ls], (CONV_RB, CONV_LC))
            for t in range(CONV_WIDTH):
                acc += cw_ref[t:t + 1, ls] * ext[lc, r0 + off + t:r0 + off + t + CONV_RB, :]
            c_ref[r0:r0 + CONV_RB, ls] = acc


def _gates_conv_prompt(u, w_tail, glu, cw, cb, nb, seq, tm=1024, tn=1024):
    k = u.shape[1]
    ng = 2 * D_MODEL // tn
    cwid = CONV_CH // ng
    j0 = (O_GATE - O_GLU) // tn
    return pl.pallas_call(
        functools.partial(_gates_conv_kernel, tiles_per_seq=seq // tm),
        grid=(ng, nb * seq // tm),
        in_specs=[pl.BlockSpec((tm, k), lambda j, i: (i, 0)),
                  pl.BlockSpec((tn, k), lambda j, i: (j + j0, 0)),
                  pl.BlockSpec((tm, cwid), lambda j, i: (i, j)),
                  pl.BlockSpec((CONV_WIDTH, cwid), lambda j, i: (0, j)),
                  pl.BlockSpec((1, cwid), lambda j, i: (0, j))],
        out_specs=[pl.BlockSpec((tm, tn), lambda j, i: (i, j)), pl.BlockSpec((tm, cwid), lambda j, i: (i, j))],
        out_shape=[jax.ShapeDtypeStruct((nb * seq, 2 * D_MODEL), bf16),
                   jax.ShapeDtypeStruct((nb * seq, CONV_CH), f32)],
        scratch_shapes=[pltpu.VMEM((cwid // CONV_LC, tm + CONV_PAD, CONV_LC), f32)],
        compiler_params=_params("arbitrary", "arbitrary"),
        name="gates_conv",
    )(u, w_tail, glu, cw, cb)


def _cconv_sample_kernel(x_ref, st_ref, w_ref, b_ref, g_ref, beta_ref, o_ref, e_scr, cbuf):
    nseq = CHUNK // SEG_S
    hist = CONV_WIDTH - 1
    e_scr[:, 0:hist, :] = st_ref[...]
    e_scr[:, hist:hist + SEG_S, :] = x_ref[...].reshape(nseq, SEG_S, CONV_CH)
    for lc in range(CONV_CH // CONV_LC):
        ls = slice(lc * CONV_LC, (lc + 1) * CONV_LC)
        acc = jnp.broadcast_to(b_ref[:, ls], (CHUNK, CONV_LC))
        for j in range(CONV_WIDTH):
            acc += w_ref[j:j + 1, ls] * e_scr[:, j:j + SEG_S, ls].reshape(CHUNK, CONV_LC)
        cbuf[:, ls] = acc
    o_ref[...] = _ln_silu(cbuf[...], g_ref[...], beta_ref[...])


def _cconv_sample(glu, glu_state, w, b, g, beta, row0):
    nb, hist, _ = glu_state.shape
    nseq = CHUNK // SEG_S
    rb0 = row0 // CHUNK
    vec = pl.BlockSpec((1, CONV_CH), lambda i: (0, 0))
    return pl.pallas_call(
        _cconv_sample_kernel,
        grid=(nb // nseq,),
        in_specs=[pl.BlockSpec((CHUNK, CONV_CH), lambda i: (rb0 + i, 0)),
                  pl.BlockSpec((nseq, hist, CONV_CH), lambda i: (i, 0, 0)),
                  pl.BlockSpec((CONV_WIDTH, CONV_CH), lambda i: (0, 0)), vec, vec, vec],
        out_specs=pl.BlockSpec((CHUNK, CONV_CH), lambda i: (i, 0)),
        out_shape=jax.ShapeDtypeStruct((nb * SEG_S, CONV_CH), bf16),
        scratch_shapes=[pltpu.VMEM((nseq, hist + SEG_S, CONV_CH), f32), pltpu.VMEM((CHUNK, CONV_CH), f32)],
        compiler_params=_params("parallel"),
        name="cconv_sample",
    )(glu, glu_state, w, b, g, beta)


def _merge_kernel(oap_ref, oas_ref, cp_ref, cs_ref, gap_ref, gas_ref, gbp_ref, gbs_ref, h_ref,
                  wa_ref, wb_ref, wo_ref, lng_ref, lnb_ref, gpost_ref, o_ref, *, na):
    def run(oa_ref, c, ga_ref, gb_ref):
        merged = (ga_ref[...].astype(f32) * _dot(oa_ref[...], wa_ref[...])
                  + gb_ref[...].astype(f32) * _dot(c, wb_ref[...]))
        o_ref[...] = h_ref[...] + _rms(_dot(merged.astype(bf16), wo_ref[...]), gpost_ref[...])

    i = pl.program_id(0)
    pl.when(i < na)(lambda: run(oap_ref, _ln_silu(cp_ref[...], lng_ref[...], lnb_ref[...]), gap_ref, gbp_ref))
    pl.when(i >= na)(lambda: run(oas_ref, cs_ref[...], gas_ref, gbs_ref))


def _merge(oa_p, oa_s, c_p, c_s, gates_p, gates_s, h, wa, wb, wo, ln_g, ln_b, gpost, tm=256):
    t = h.shape[0]
    na = oa_p.shape[0] // tm
    act = pl.BlockSpec((tm, D_MODEL), lambda i: (i, 0))
    vec = pl.BlockSpec((1, D_MODEL), lambda i: (0, 0))
    src = lambda col: _split_specs((tm, D_MODEL), na, 2, col)
    wsp = pl.BlockSpec((D_MODEL, D_MODEL), lambda i: (0, 0), pipeline_mode=pl.Buffered(1))
    return pl.pallas_call(
        functools.partial(_merge_kernel, na=na),
        grid=(t // tm,),
        in_specs=src(0) + src(0) + src(0) + src(1) + [act, wsp, wsp, wsp, vec, vec, vec],
        out_specs=act,
        out_shape=jax.ShapeDtypeStruct((t, D_MODEL), f32),
        compiler_params=_params("parallel"),
        name="merge",
    )(oa_p, oa_s, c_p, c_s, gates_p, gates_s, gates_p, gates_s, h, wa, wb, wo, ln_g, ln_b, gpost)


def _ple_kernel(hn_ref, pa_ref, pb_ref, h_ref, wg_ref, wp_ref, gpost_ref, oa_ref, ob_ref, *, na):
    i = pl.program_id(0)
    gate = _sigmoid(_dot(hn_ref[...], wg_ref[...]))

    def out(p_ref):
        v = gate * _dot(p_ref[...].astype(bf16), wp_ref[...])
        return h_ref[...] + _rms(v, gpost_ref[...])

    @pl.when(i < na)
    def _():
        oa_ref[...] = out(pa_ref)

    @pl.when(i >= na)
    def _():
        ob_ref[...] = out(pb_ref)


def _ple(hn, pa, pb, h, wg, wp, gpost, tm=512):
    na, nb = pa.shape[0] // tm, pb.shape[0] // tm
    row = lambda i: (i, 0)
    first = lambda i: (jnp.minimum(i, na - 1), 0)
    second = lambda i: (jnp.maximum(i - na, 0), 0)
    act = pl.BlockSpec((tm, D_MODEL), row)
    return pl.pallas_call(
        functools.partial(_ple_kernel, na=na),
        grid=(na + nb,),
        in_specs=[act, pl.BlockSpec((tm, PLE_DIM), first), pl.BlockSpec((tm, PLE_DIM), second), act,
                  pl.BlockSpec((D_MODEL, D_MODEL), lambda i: (0, 0), pipeline_mode=pl.Buffered(1)),
                  pl.BlockSpec((PLE_DIM, D_MODEL), lambda i: (0, 0)),
                  pl.BlockSpec((1, D_MODEL), lambda i: (0, 0))],
        out_specs=[pl.BlockSpec((tm, D_MODEL), first), pl.BlockSpec((tm, D_MODEL), second)],
        out_shape=[jax.ShapeDtypeStruct((na * tm, D_MODEL), f32), jax.ShapeDtypeStruct((nb * tm, D_MODEL), f32)],
        compiler_params=_params("arbitrary"),
        name="ple",
    )(hn, pa, pb, h, wg, wp, gpost)


def _layer(xp, xs, pp, ps, nb, seq, s0, qkv_buf, glu_buf, w):
    (ffn1_pre, ffn1_w_gu, ffn1_w_down, ffn1_post, mix_pre, w_in, w_short_conv, a_log, dt_bias,
     o_norm, w_dw_conv, b_dw_conv, ln_g, ln_b, w_branch_a, w_branch_b, w_out, mix_post,
     ffn2_pre, ffn2_w_gu, ffn2_w_down, ffn2_post, ple_pre, w_ple_gate, w_ple_proj,
     ple_post) = w
    tp = nb * seq
    ns = s0.shape[0]
    row = lambda v: v.astype(f32).reshape(1, -1)
    cast = lambda v: v.astype(bf16)

    h1, u = _ffn([xp, xs], row(ffn1_pre), cast(ffn1_w_gu), cast(ffn1_w_down), row(ffn1_post), row(mix_pre))

    w_in_t = jnp.swapaxes(w_in[0], 0, 1).astype(bf16)
    w_qkvz, w_mid, w_tail = w_in_t[:O_BETA], w_in_t[O_BETA:O_GLU], w_in_t[O_GLU:]
    qkvz = _proj(u, w_qkvz, 0, O_BETA, "none", f32)
    ba = _proj_ba(u, _ba_weight(w_mid.T), _group_lanes(a_log), _group_lanes(dt_bias))
    glu = _proj_glu(u, w_tail)

    w_conv = w_short_conv.astype(f32)
    onorm = row(o_norm)
    oa_p, s_p = _gdn_prompt(qkvz, ba, w_conv, onorm, nb, seq)
    oa_s, s_s = _gdn_sample(qkvz, qkv_buf.astype(f32), ba, w_conv, onorm, s0.astype(f32), tp)

    cw, cb, lg, lb = w_dw_conv.astype(f32), row(b_dw_conv), row(ln_g), row(ln_b)
    gates_p, c_p = _gates_conv_prompt(u, w_tail, glu, cw, cb, nb, seq)
    gates_s = _proj(u, w_tail, O_GATE - O_GLU, 2 * D_MODEL, "sigmoid", bf16, r0=tp, rows=xs.shape[0])
    c_s = _cconv_sample(glu, glu_buf.astype(f32), cw, cb, lg, lb, tp)

    h2 = _merge(oa_p, oa_s, c_p, c_s, gates_p, gates_s, h1, cast(w_branch_a), cast(w_branch_b), cast(w_out),
                lg, lb, row(mix_post))
    h3, hn = _ffn([h2], row(ffn2_pre), cast(ffn2_w_gu), cast(ffn2_w_down), row(ffn2_post), row(ple_pre))
    yp, ys = _ple(hn, pp, ps, h3, cast(w_ple_gate), cast(w_ple_proj), row(ple_post))

    tail = lambda a, width, n: jnp.stack([a[(b + 1) * seq - n:(b + 1) * seq, :width] for b in range(nb)])
    qkv_p = tail(qkvz, QKV_DIM, SHORT_CONV - 1)
    glu_p = tail(glu, CONV_CH, CONV_WIDTH - 1)
    qkv_s = qkvz[tp:, :QKV_DIM].reshape(ns, SEG_S, QKV_DIM)[:, SEG_S - (SHORT_CONV - 1):]
    glu_s = jnp.concatenate([glu_buf.astype(f32)[:, SEG_S:], glu[tp:].reshape(ns, SEG_S, CONV_CH)], axis=1)
    return (yp, ys, s_p, qkv_p, glu_p, s_s, qkv_s, glu_s)


def kernel(x_prompt, x_sample, p_prompt, p_sample, state_delta, state_qkv_conv, state_glu_conv, ffn1_pre, ffn1_w_gu, ffn1_w_down, ffn1_post, mix_pre, w_in, w_short_conv, a_log, dt_bias, o_norm, w_dw_conv, b_dw_conv, ln_g, ln_b, w_branch_a, w_branch_b, w_out, mix_post, ffn2_pre, ffn2_w_gu, ffn2_w_down, ffn2_post, ple_pre, w_ple_gate, w_ple_proj, ple_post):
    weights = (ffn1_pre, ffn1_w_gu, ffn1_w_down, ffn1_post, mix_pre, w_in, w_short_conv, a_log,
               dt_bias, o_norm, w_dw_conv, b_dw_conv, ln_g, ln_b, w_branch_a, w_branch_b,
               w_out, mix_post, ffn2_pre, ffn2_w_gu, ffn2_w_down, ffn2_post, ple_pre,
               w_ple_gate, w_ple_proj, ple_post)
    nb, seq, _ = x_prompt.shape
    ns, ls, _ = x_sample.shape
    assert ls == SEG_S and seq % CHUNK == 0 and ns % (CHUNK // SEG_S) == 0
    depth = ffn1_pre.shape[0]
    tp = nb * seq
    xp, xs = x_prompt.reshape(tp, D_MODEL), x_sample.reshape(ns * ls, D_MODEL)
    outs = [[] for _ in range(6)]
    for i in range(depth):
        wi = tuple(wt[i:i + 1] if wt is w_in else wt[i] for wt in weights)
        xp, xs, s_p, q_p, g_p, s_s, q_s, g_s = _layer(
            xp, xs, p_prompt[i].reshape(tp, PLE_DIM), p_sample[i].reshape(ns * ls, PLE_DIM), nb, seq,
            state_delta[i], state_qkv_conv[i], state_glu_conv[i], wi)
        for lst, v in zip(outs, (s_p, q_p, g_p, s_s, q_s, g_s)):
            lst.append(v)
    return (xp.reshape(nb, seq, D_MODEL), xs.reshape(ns, ls, D_MODEL)) + tuple(jnp.stack(lst) for lst in outs)
```

```python
import functools

import jax
import jax.numpy as jnp
from jax import lax
from jax.experimental import pallas as pl
from jax.experimental.pallas import tpu as pltpu

f32 = jnp.float32
bf16 = jnp.bfloat16

D_MODEL = 2048
N_HEADS = 16
HEAD_K = 128
HEAD_V = 128
KEY_DIM = N_HEADS * HEAD_K
VAL_DIM = N_HEADS * HEAD_V
QKV_DIM = 2 * KEY_DIM + VAL_DIM
SHORT_CONV = 4
CONV_CH = D_MODEL
CONV_WIDTH = 31
D_FF = 5632
PLE_DIM = 256
EPS = 1e-6

O_Z = QKV_DIM
O_BETA = O_Z + VAL_DIM
O_A = O_BETA + N_HEADS
O_GLU = O_A + N_HEADS
O_GATE = O_GLU + 2 * CONV_CH

CHUNK = 64
HG = 8
NHG = N_HEADS // HG
HW = HG * HEAD_K
SEG_S = 8
LOCAL_CHUNKS = 4

VMEM_LIMIT = 56 * 1024 * 1024


def _sigmoid(x):
    return 1.0 / (1.0 + jnp.exp(-x))


def _silu(x):
    return x * _sigmoid(x)


def _rms(x, g):
    return x * lax.rsqrt(jnp.mean(x * x, axis=-1, keepdims=True) + EPS) * g


def _dot(a, b):
    return jnp.dot(a, b, preferred_element_type=f32)


def _dot_nt(a, b):
    return lax.dot_general(a, b, (((1,), (1,)), ((), ())), preferred_element_type=f32)


def _params(*sem):
    return pltpu.CompilerParams(dimension_semantics=sem, vmem_limit_bytes=VMEM_LIMIT)


def _per_source(i, na, refs, fn):
    if len(refs) == 1:
        fn(refs[0])
    else:
        pl.when(i < na)(lambda: fn(refs[0]))
        pl.when(i >= na)(lambda: fn(refs[1]))


def _split_specs(block, na, nsrc, col=0):
    if nsrc == 1:
        return [pl.BlockSpec(block, lambda i, *_: (i, col))]
    return [pl.BlockSpec(block, lambda i, *_: (jnp.minimum(i, na - 1), col)),
            pl.BlockSpec(block, lambda i, *_: (jnp.maximum(i - na, 0), col))]


def _ffn_kernel(*refs, na, nsrc):
    x_refs = refs[:nsrc]
    gpre_ref, wg_ref, wu_ref, wd_ref, gpost_ref, gnext_ref, h_ref, hn_ref, xn_scr, acc_scr = refs[nsrc:]
    i, j = pl.program_id(0), pl.program_id(1)

    @pl.when(j == 0)
    def _():
        def norm_in(x_ref):
            xn_scr[...] = _rms(x_ref[...], gpre_ref[...]).astype(bf16)

        _per_source(i, na, x_refs, norm_in)
        acc_scr[...] = jnp.zeros_like(acc_scr)

    xn = xn_scr[...]
    a = _silu(_dot(xn, wg_ref[...])) * _dot(xn, wu_ref[...])
    acc_scr[...] += _dot(a.astype(bf16), wd_ref[...])

    @pl.when(j == pl.num_programs(1) - 1)
    def _():
        def residual_out(x_ref):
            h = x_ref[...] + 0.5 * _rms(acc_scr[...], gpost_ref[...])
            h_ref[...] = h
            hn_ref[...] = _rms(h, gnext_ref[...]).astype(bf16)

        _per_source(i, na, x_refs, residual_out)


def _ffn(xs, gpre, w_gu, w_down, gpost, gnext, tm=512, tf=512):
    na = xs[0].shape[0] // tm
    nt = sum(x.shape[0] for x in xs) // tm
    nf = D_FF // tf
    row = lambda i, j: (i, 0)
    vec = pl.BlockSpec((1, D_MODEL), lambda i, j: (0, 0))
    return pl.pallas_call(
        functools.partial(_ffn_kernel, na=na, nsrc=len(xs)),
        grid=(nt, nf),
        in_specs=_split_specs((tm, D_MODEL), na, len(xs))
        + [vec,
           pl.BlockSpec((D_MODEL, tf), lambda i, j: (0, j)),
           pl.BlockSpec((D_MODEL, tf), lambda i, j: (0, j + nf)),
           pl.BlockSpec((tf, D_MODEL), lambda i, j: (j, 0)), vec, vec],
        out_specs=[pl.BlockSpec((tm, D_MODEL), row), pl.BlockSpec((tm, D_MODEL), row)],
        out_shape=[jax.ShapeDtypeStruct((nt * tm, D_MODEL), f32), jax.ShapeDtypeStruct((nt * tm, D_MODEL), bf16)],
        scratch_shapes=[pltpu.VMEM((tm, D_MODEL), bf16), pltpu.VMEM((tm, D_MODEL), f32)],
        compiler_params=_params("parallel", "arbitrary"),
        name="ffn",
    )(*xs, gpre, w_gu, w_gu, w_down, gpost, gnext)


def _proj_kernel(x_ref, wt_ref, o_ref, *, act):
    y = _dot_nt(x_ref[...], wt_ref[...])
    if act == "sigmoid":
        y = _sigmoid(y)
    o_ref[...] = y.astype(o_ref.dtype)


def _proj(x, wt, n0, n, act, out_dtype, r0=0, rows=None, tm=1024, tn=1024):
    t, k = x.shape
    rows = t if rows is None else rows
    i0, j0 = r0 // tm, n0 // tn
    return pl.pallas_call(
        functools.partial(_proj_kernel, act=act),
        grid=(rows // tm, n // tn),
        in_specs=[pl.BlockSpec((tm, k), lambda i, j: (i + i0, 0)),
                  pl.BlockSpec((tn, k), lambda i, j: (j + j0, 0))],
        out_specs=pl.BlockSpec((tm, tn), lambda i, j: (i, j)),
        out_shape=jax.ShapeDtypeStruct((rows, n), out_dtype),
        compiler_params=_params("parallel", "arbitrary"),
        name="proj_" + act,
    )(x, wt)


def _glu_kernel(x_ref, wa_ref, wb_ref, o_ref):
    x = x_ref[...]
    o_ref[...] = _dot_nt(x, wa_ref[...]) * _sigmoid(_dot_nt(x, wb_ref[...]))


def _proj_glu(x, wt, tm=1024, tn=512):
    t, k = x.shape
    nj = CONV_CH // tn
    return pl.pallas_call(
        _glu_kernel,
        grid=(t // tm, nj),
        in_specs=[pl.BlockSpec((tm, k), lambda i, j: (i, 0)),
                  pl.BlockSpec((tn, k), lambda i, j: (j, 0)),
                  pl.BlockSpec((tn, k), lambda i, j: (j + nj, 0))],
        out_specs=pl.BlockSpec((tm, tn), lambda i, j: (i, j)),
        out_shape=jax.ShapeDtypeStruct((t, CONV_CH), f32),
        compiler_params=_params("parallel", "arbitrary"),
        name="proj_glu",
    )(x, wt, wt)


def _ba_kernel(x_ref, w_ref, alog_ref, dtb_ref, o_ref):
    y = _dot(x_ref[...], w_ref[...])
    lane = lax.broadcasted_iota(jnp.int32, y.shape, 1)
    ya = y + dtb_ref[...]
    softplus = jnp.maximum(ya, 0.0) + jnp.log(1.0 + jnp.exp(-jnp.abs(ya)))
    o_ref[...] = jnp.where(lane < HG, _sigmoid(y), -jnp.exp(alog_ref[...]) * softplus)


def _proj_ba(x, w_ba, alog_row, dtb_row, tm=1024):
    t, k = x.shape
    vec = pl.BlockSpec((1, 128), lambda i, j: (0, j))
    return pl.pallas_call(
        _ba_kernel,
        grid=(t // tm, NHG),
        in_specs=[pl.BlockSpec((tm, k), lambda i, j: (i, 0)),
                  pl.BlockSpec((k, 128), lambda i, j: (0, j)), vec, vec],
        out_specs=pl.BlockSpec((tm, 128), lambda i, j: (i, j)),
        out_shape=jax.ShapeDtypeStruct((t, NHG * 128), f32),
        compiler_params=_params("parallel", "arbitrary"),
        name="proj_ba",
    )(x, w_ba, alog_row, dtb_row)


def _group_lanes(v):
    r = jnp.zeros((NHG, 128), f32).at[:, HG:2 * HG].set(v.astype(f32).reshape(NHG, HG))
    return r.reshape(1, NHG * 128)


def _ba_weight(w_ba):
    wb = w_ba[:, :N_HEADS].reshape(D_MODEL, NHG, HG)
    wa = w_ba[:, N_HEADS:].reshape(D_MODEL, NHG, HG)
    w = jnp.zeros((D_MODEL, NHG, 128), w_ba.dtype).at[:, :, :HG].set(wb).at[:, :, HG:2 * HG].set(wa)
    return w.reshape(D_MODEL, NHG * 128)


def _seq_masks(n, seg):
    r = lax.broadcasted_iota(jnp.int32, (n, n), 0)
    c = lax.broadcasted_iota(jnp.int32, (n, n), 1)
    same = (r // seg) == (c // seg)
    return same & (r >= c), same & (r > c), same


def _mask_dot(mask, x):
    l = jnp.where(mask, 1.0, 0.0).astype(bf16)
    hi = x.astype(bf16)
    r1 = x - hi.astype(f32)
    mid = r1.astype(bf16)
    lo = (r1 - mid.astype(f32)).astype(bf16)
    return _dot(l, hi) + _dot(l, mid) + _dot(l, lo)


def _mm(a, b):
    return _dot(a.astype(bf16), b.astype(bf16))


def _unit_lower_solve(ms, rhss, seg):
    n = range(len(ms))
    r = lax.broadcasted_iota(jnp.int32, (CHUNK, CHUNK), 0)
    c = lax.broadcasted_iota(jnp.int32, (CHUNK, CHUNK), 1)
    eye = (r == c).astype(f32)
    if seg <= 8:
        d, lo = ms, None
    else:
        blk = (r // 16) == (c // 16)
        d = [jnp.where(blk, m, 0.0) for m in ms]
        lo = [ms[i] - d[i] for i in n]
    d2 = [_mm(d[i], d[i]) for i in n]
    d3 = [_mm(d[i], d2[i]) for i in n]
    d4 = [_mm(d2[i], d2[i]) for i in n]
    a = [eye - d[i] + d2[i] - d3[i] for i in n]
    td = [a[i] + _mm(a[i], d4[i]) for i in n]
    if seg > 8:
        d8 = [_mm(d4[i], d4[i]) for i in n]
        td = [td[i] + _mm(td[i], d8[i]) for i in n]
    y = [_mm(td[i], rhss[i]) for i in n]
    if lo is None:
        return y
    e = [_mm(td[i], lo[i]) for i in n]
    e2 = [_mm(e[i], e[i]) for i in n]
    w = [y[i] - _mm(e[i], y[i]) for i in n]
    return [w[i] + _mm(e2[i], w[i]) for i in n]


def _chunk_local(qs, ks, vs, betas, gcols, grows, gtots, incl, strict, seg):
    n = range(len(qs))
    qn = [q * lax.rsqrt(jnp.sum(q * q, axis=-1, keepdims=True) + EPS) * (HEAD_K ** -0.5) for q in qs]
    kn = [k * lax.rsqrt(jnp.sum(k * k, axis=-1, keepdims=True) + EPS) for k in ks]
    decay = [jnp.exp(jnp.where(incl, gcols[i] - grows[i], -jnp.inf)) for i in n]
    kb = [k.astype(bf16) for k in kn]
    kk = [_dot_nt(kb[i], kb[i]) for i in n]
    qk = [_dot_nt(qn[i].astype(bf16), kb[i]) for i in n]
    m = [jnp.where(strict, betas[i] * decay[i] * kk[i], 0.0) for i in n]
    gamma = [jnp.exp(g) for g in gcols]
    rhs = [jnp.concatenate([betas[i] * vs[i], (betas[i] * gamma[i]) * kn[i]], axis=1) for i in n]
    sol = _unit_lower_solve(m, rhs, seg)
    uv = [s[:, :HEAD_V] for s in sol]
    wk = [s[:, HEAD_V:] for s in sol]
    qkd = [qk[i] * decay[i] for i in n]
    qg = [qn[i] * gamma[i] for i in n]
    kd = [kn[i] * jnp.exp(gtots[i] - gcols[i]) for i in n]
    return uv, wk, qkd, qg, kd


def _out_gate(o, onorm, z):
    return (_rms(o, onorm) * _silu(z)).astype(bf16)


def _gdn_prompt_kernel(q_ref, k_ref, v_ref, z_ref, ba_ref, wq_ref, wk_ref, wv_ref, onorm_ref,
                       oa_ref, s_ref,
                       eq, ek, ev, s_scr, qc_s, kc_s, vc_s, gc_s, gt_s, gl_s, uv_s, wq_s, kdt_s, qkd_s,
                       *, rows):
    t = pl.program_id(2)
    nc = rows // CHUNK
    heads = range(HG)
    hs = [slice(h * HEAD_K, (h + 1) * HEAD_K) for h in heads]
    gs = [slice(HG + h, HG + h + 1) for h in heads]

    @pl.when(t == 0)
    def _():
        s_scr[...] = jnp.zeros_like(s_scr)
        for e in (eq, ek, ev):
            e[:, 0:8, :] = jnp.zeros((HG, 8, HEAD_K), f32)

    off = 8 - (SHORT_CONV - 1)
    for e, x_ref, w_ref, dst in ((eq, q_ref, wq_ref, qc_s), (ek, k_ref, wk_ref, kc_s), (ev, v_ref, wv_ref, vc_s)):
        for h in heads:
            e[h, 8:, :] = x_ref[:, hs[h]]
            for r0 in range(0, rows, CHUNK):
                acc = w_ref[0:1, hs[h]] * e[h, r0 + off:r0 + off + CHUNK, :]
                for j in range(1, SHORT_CONV):
                    acc += w_ref[j:j + 1, hs[h]] * e[h, r0 + off + j:r0 + off + j + CHUNK, :]
                dst[r0:r0 + CHUNK, hs[h]] = _silu(acc)

    incl_b, _, same_b = _seq_masks(rows, CHUNK)
    ba = ba_ref[...]
    gc_s[...] = _mask_dot(incl_b, ba)
    gtot_b = _mask_dot(same_b, ba)
    gt_s[...] = gtot_b
    gl_s[...] = jnp.exp(gtot_b)
    incl, strict, _ = _seq_masks(CHUNK, CHUNK)

    def local(cp, carry):
        cs = [cp * LOCAL_CHUNKS + i for i in range(LOCAL_CHUNKS)]
        rss = [pl.ds(pl.multiple_of(c * CHUNK, CHUNK), CHUNK) for c in cs]
        units = [(i, h) for i in range(LOCAL_CHUNKS) for h in heads]
        bac = [ba_ref[rs, :] for rs in rss]
        gc = [gc_s[rs, :] for rs in rss]
        gtot = [gt_s[rs, :] for rs in rss]
        gct = [g.T for g in gc]
        uv, wk, qkd, qg, kd = _chunk_local(
            [qc_s[rss[i], hs[h]] for i, h in units], [kc_s[rss[i], hs[h]] for i, h in units],
            [vc_s[rss[i], hs[h]] for i, h in units], [bac[i][:, h:h + 1] for i, h in units],
            [gc[i][:, gs[h]] for i, h in units], [gct[i][gs[h], :] for i, h in units],
            [gtot[i][:, gs[h]] for i, h in units], incl, strict, CHUNK)
        for n, (i, h) in enumerate(units):
            uv_s[rss[i], hs[h]] = uv[n]
            wq_s[cs[i], 0:CHUNK, hs[h]] = wk[n].astype(bf16)
            wq_s[cs[i], CHUNK:2 * CHUNK, hs[h]] = qg[n].astype(bf16)
            kdt_s[cs[i], h] = kd[n].T.astype(bf16)
            qkd_s[cs[i], h] = qkd[n].astype(bf16)
        return carry

    lax.fori_loop(0, nc // LOCAL_CHUNKS, local, 0)

    def recur(c, carry):
        r0 = pl.multiple_of(c * CHUNK, CHUNK)
        rs = pl.ds(r0, CHUNK)
        s = [s_scr[h] for h in heads]
        p = [_dot(wq_s[c, :, hs[h]], s[h].astype(bf16)) for h in heads]
        u = [(uv_s[rs, hs[h]] - p[h][:CHUNK]).astype(bf16) for h in heads]
        o = [p[h][CHUNK:] + _dot(qkd_s[c, h], u[h]) for h in heads]
        ds = [_dot(kdt_s[c, h], u[h]) for h in heads]
        for h in heads:
            s_scr[h] = gl_s[pl.ds(r0, 1), gs[h]] * s[h] + ds[h]
            oa_ref[rs, hs[h]] = _out_gate(o[h], onorm_ref[...], z_ref[rs, hs[h]])
        return carry

    lax.fori_loop(0, nc, recur, 0)

    for e in (eq, ek, ev):
        e[:, 0:8, :] = e[:, rows:rows + 8, :]

    @pl.when(t == pl.num_programs(2) - 1)
    def _():
        s_ref[0] = s_scr[...]


def _gdn_prompt(qkvz, ba, w_conv, onorm, nb, seq, rows=256):
    nt = seq // rows
    nc = rows // CHUNK
    rb = lambda b, hg, t: b * nt + t
    col = lambda off: pl.BlockSpec((rows, HW), lambda b, hg, t: (rb(b, hg, t), off * NHG + hg))
    wcol = lambda off: pl.BlockSpec((SHORT_CONV, HW), lambda b, hg, t: (0, off * NHG + hg))
    return pl.pallas_call(
        functools.partial(_gdn_prompt_kernel, rows=rows),
        grid=(nb, NHG, nt),
        in_specs=[col(0), col(1), col(2), col(3),
                  pl.BlockSpec((rows, 128), lambda b, hg, t: (rb(b, hg, t), hg)),
                  wcol(0), wcol(1), wcol(2),
                  pl.BlockSpec((1, HEAD_V), lambda b, hg, t: (0, 0))],
        out_specs=[pl.BlockSpec((rows, HW), lambda b, hg, t: (rb(b, hg, t), hg)),
                   pl.BlockSpec((1, HG, HEAD_K, HEAD_V), lambda b, hg, t: (b, hg, 0, 0))],
        out_shape=[jax.ShapeDtypeStruct((nb * seq, VAL_DIM), bf16),
                   jax.ShapeDtypeStruct((nb, N_HEADS, HEAD_K, HEAD_V), f32)],
        scratch_shapes=[pltpu.VMEM((HG, rows + 8, HEAD_K), f32)] * 3
        + [pltpu.VMEM((HG, HEAD_K, HEAD_V), f32)]
        + [pltpu.VMEM((rows, HW), f32)] * 3
        + [pltpu.VMEM((rows, 128), f32)] * 3
        + [pltpu.VMEM((rows, HW), f32),
           pltpu.VMEM((nc, 2 * CHUNK, HW), bf16),
           pltpu.VMEM((nc, HG, HEAD_K, CHUNK), bf16),
           pltpu.VMEM((nc, HG, CHUNK, CHUNK), bf16)],
        compiler_params=_params("parallel", "parallel", "arbitrary"),
        name="gdn_prompt",
    )(qkvz, qkvz, qkvz, qkvz, ba, w_conv, w_conv, w_conv, onorm)


def _gdn_sample_kernel(q_ref, k_ref, v_ref, qst_ref, kst_ref, vst_ref, z_ref, ba_ref, wq_ref, wk_ref, wv_ref,
                       onorm_ref, s0_ref, oa_ref, s_ref, e_scr):
    nseq = CHUNK // SEG_S
    heads = range(HG)
    seqs = range(nseq)
    hs = [slice(h * HEAD_K, (h + 1) * HEAD_K) for h in heads]
    gs = [slice(HG + h, HG + h + 1) for h in heads]
    js = [slice(j * SEG_S, (j + 1) * SEG_S) for j in seqs]
    incl, strict, same = _seq_masks(CHUNK, SEG_S)
    hist = SHORT_CONV - 1

    def conv(x_ref, st_ref, w_ref):
        e_scr[:, SEG_S - hist:SEG_S, :] = st_ref[...]
        e_scr[:, SEG_S:, :] = x_ref[...].reshape(nseq, SEG_S, HW)
        off = SEG_S - hist
        acc = w_ref[0:1, :] * e_scr[:, off:off + SEG_S, :].reshape(CHUNK, HW)
        for j in range(1, SHORT_CONV):
            acc += w_ref[j:j + 1, :] * e_scr[:, off + j:off + j + SEG_S, :].reshape(CHUNK, HW)
        return _silu(acc)

    qc, kc, vc = conv(q_ref, qst_ref, wq_ref), conv(k_ref, kst_ref, wk_ref), conv(v_ref, vst_ref, wv_ref)
    bac = ba_ref[...]
    gc = _mask_dot(incl, bac)
    gtot = _mask_dot(same, bac)
    gct = gc.T
    glv = jnp.exp(gtot)
    uv, wk, qkd, qg, kd = _chunk_local(
        [qc[:, hs[h]] for h in heads], [kc[:, hs[h]] for h in heads], [vc[:, hs[h]] for h in heads],
        [bac[:, h:h + 1] for h in heads], [gc[:, gs[h]] for h in heads], [gct[gs[h], :] for h in heads],
        [gtot[:, gs[h]] for h in heads], incl, strict, SEG_S)
    kdt = [kd[h].T.astype(bf16) for h in heads]
    p = [[_dot(jnp.concatenate([wk[h][js[j]], qg[h][js[j]]], axis=0).astype(bf16),
               s0_ref[j, h].astype(bf16)) for j in seqs] for h in heads]
    u = [[uv[h][js[j]] - p[h][j][:SEG_S] for j in seqs] for h in heads]
    zeros = jnp.zeros((SEG_S, HEAD_V), f32)
    for h in heads:
        for j in seqs:
            u_rows = jnp.concatenate([u[h][j] if i == j else zeros for i in seqs], axis=0).astype(bf16)
            s_ref[j, h] = glv[j * SEG_S:j * SEG_S + 1, gs[h]] * s0_ref[j, h] + _dot(kdt[h], u_rows)
    for h in heads:
        u_all = jnp.concatenate(u[h], axis=0).astype(bf16)
        o = jnp.concatenate([p[h][j][SEG_S:] for j in seqs], axis=0) + _dot(qkd[h].astype(bf16), u_all)
        oa_ref[:, hs[h]] = _out_gate(o, onorm_ref[...], z_ref[:, hs[h]])


def _gdn_sample(qkvz, qkv_state, ba, w_conv, onorm, s0, row0):
    nb = qkv_state.shape[1]
    nseq = CHUNK // SEG_S
    rb0 = row0 // CHUNK
    col = lambda off: pl.BlockSpec((CHUNK, HW), lambda g, hg: (rb0 + g, off * NHG + hg))
    st = lambda off: pl.BlockSpec((None, nseq, SHORT_CONV - 1, HW), lambda g, hg: (0, g, 0, off * NHG + hg))
    wcol = lambda off: pl.BlockSpec((SHORT_CONV, HW), lambda g, hg: (0, off * NHG + hg))
    sspec = pl.BlockSpec((nseq, HG, HEAD_K, HEAD_V), lambda g, hg: (g, hg, 0, 0))
    return pl.pallas_call(
        _gdn_sample_kernel,
        grid=(nb // nseq, NHG),
        in_specs=[col(0), col(1), col(2), st(0), st(1), st(2), col(3),
                  pl.BlockSpec((CHUNK, 128), lambda g, hg: (rb0 + g, hg)),
                  wcol(0), wcol(1), wcol(2),
                  pl.BlockSpec((1, HEAD_V), lambda g, hg: (0, 0)),
                  pl.BlockSpec((None, nseq, HG, HEAD_K, HEAD_V), lambda g, hg: (0, g, hg, 0, 0))],
        out_specs=[pl.BlockSpec((CHUNK, HW), lambda g, hg: (g, hg)), sspec],
        out_shape=[jax.ShapeDtypeStruct((nb * SEG_S, VAL_DIM), bf16), jax.ShapeDtypeStruct(s0.shape[1:], f32)],
        scratch_shapes=[pltpu.VMEM((nseq, 2 * SEG_S, HW), f32)],
        compiler_params=_params("parallel", "parallel"),
        name="gdn_sample",
    )(qkvz, qkvz, qkvz, qkv_state, qkv_state, qkv_state, qkvz, ba, w_conv, w_conv, w_conv, onorm, s0)


CONV_LC = 128
CONV_RB = 64


def _ln_silu(c, g, b):
    mu = jnp.mean(c, axis=-1, keepdims=True)
    xc = c - mu
    y = xc * lax.rsqrt(jnp.mean(xc * xc, axis=-1, keepdims=True) + EPS) * g + b
    return _silu(y).astype(bf16)


CONV_PAD = 32


def _gates_conv_kernel(u_ref, wg_ref, x_ref, cw_ref, cb_ref, g_ref, c_ref, ext, *, tiles_per_seq):
    j, i = pl.program_id(0), pl.program_id(1)
    tm, cw = x_ref.shape

    @pl.when((i == 0) & (j == 0))
    def _():
        ext[:, tm:, :] = jnp.zeros((cw // CONV_LC, CONV_PAD, CONV_LC), f32)

    first = (i % tiles_per_seq) == 0
    off = CONV_PAD - (CONV_WIDTH - 1)
    nlc = cw // CONV_LC
    gw = g_ref.shape[1] // nlc
    for lc in range(nlc):
        gs = slice(lc * gw, (lc + 1) * gw)
        g_ref[:, gs] = _sigmoid(_dot_nt(u_ref[...], wg_ref[gs, :])).astype(g_ref.dtype)
        ls = slice(lc * CONV_LC, (lc + 1) * CONV_LC)
        ext[lc, 0:CONV_PAD, :] = jnp.where(first, 0.0, ext[lc, tm:tm + CONV_PAD, :])
        ext[lc, CONV_PAD:, :] = x_ref[:, ls]
        for r0 in range(0, tm, CONV_RB):
            acc = jnp.broadcast_to(cb_ref[:, ls], (CONV_RB, CONV_LC))
            for t in range(CONV_WIDTH):
                acc += cw_ref[t:t + 1, ls] * ext[lc, r0 + off + t:r0 + off + t + CONV_RB, :]
            c_ref[r0:r0 + CONV_RB, ls] = acc


def _gates_conv_prompt(u, w_tail, glu, cw, cb, nb, seq, tm=1024, tn=1024):
    k = u.shape[1]
    ng = 2 * D_MODEL // tn
    cwid = CONV_CH // ng
    j0 = (O_GATE - O_GLU) // tn
    return pl.pallas_call(
        functools.partial(_gates_conv_kernel, tiles_per_seq=seq // tm),
        grid=(ng, nb * seq // tm),
        in_specs=[pl.BlockSpec((tm, k), lambda j, i: (i, 0)),
                  pl.BlockSpec((tn, k), lambda j, i: (j + j0, 0)),
                  pl.BlockSpec((tm, cwid), lambda j, i: (i, j)),
                  pl.BlockSpec((CONV_WIDTH, cwid), lambda j, i: (0, j)),
                  pl.BlockSpec((1, cwid), lambda j, i: (0, j))],
        out_specs=[pl.BlockSpec((tm, tn), lambda j, i: (i, j)), pl.BlockSpec((tm, cwid), lambda j, i: (i, j))],
        out_shape=[jax.ShapeDtypeStruct((nb * seq, 2 * D_MODEL), bf16),
                   jax.ShapeDtypeStruct((nb * seq, CONV_CH), f32)],
        scratch_shapes=[pltpu.VMEM((cwid // CONV_LC, tm + CONV_PAD, CONV_LC), f32)],
        compiler_params=_params("arbitrary", "arbitrary"),
        name="gates_conv",
    )(u, w_tail, glu, cw, cb)


def _cconv_sample_kernel(x_ref, st_ref, w_ref, b_ref, g_ref, beta_ref, o_ref, e_scr, cbuf):
    nseq = CHUNK // SEG_S
    hist = CONV_WIDTH - 1
    e_scr[:, 0:hist, :] = st_ref[...]
    e_scr[:, hist:hist + SEG_S, :] = x_ref[...].reshape(nseq, SEG_S, CONV_CH)
    for lc in range(CONV_CH // CONV_LC):
        ls = slice(lc * CONV_LC, (lc + 1) * CONV_LC)
        acc = jnp.broadcast_to(b_ref[:, ls], (CHUNK, CONV_LC))
        for j in range(CONV_WIDTH):
            acc += w_ref[j:j + 1, ls] * e_scr[:, j:j + SEG_S, ls].reshape(CHUNK, CONV_LC)
        cbuf[:, ls] = acc
    o_ref[...] = _ln_silu(cbuf[...], g_ref[...], beta_ref[...])


def _cconv_sample(glu, glu_state, w, b, g, beta, row0):
    _, nb, hist, _ = glu_state.shape
    nseq = CHUNK // SEG_S
    rb0 = row0 // CHUNK
    vec = pl.BlockSpec((1, CONV_CH), lambda i: (0, 0))
    return pl.pallas_call(
        _cconv_sample_kernel,
        grid=(nb // nseq,),
        in_specs=[pl.BlockSpec((CHUNK, CONV_CH), lambda i: (rb0 + i, 0)),
                  pl.BlockSpec((None, nseq, hist, CONV_CH), lambda i: (0, i, 0, 0)),
                  pl.BlockSpec((CONV_WIDTH, CONV_CH), lambda i: (0, 0)), vec, vec, vec],
        out_specs=pl.BlockSpec((CHUNK, CONV_CH), lambda i: (i, 0)),
        out_shape=jax.ShapeDtypeStruct((nb * SEG_S, CONV_CH), bf16),
        scratch_shapes=[pltpu.VMEM((nseq, hist + SEG_S, CONV_CH), f32), pltpu.VMEM((CHUNK, CONV_CH), f32)],
        compiler_params=_params("parallel"),
        name="cconv_sample",
    )(glu, glu_state, w, b, g, beta)


def _merge_kernel(oap_ref, oas_ref, cp_ref, cs_ref, gap_ref, gas_ref, gbp_ref, gbs_ref, h_ref,
                  wa_ref, wb_ref, wo_ref, lng_ref, lnb_ref, gpost_ref, o_ref, *, na):
    def run(oa_ref, c, ga_ref, gb_ref):
        merged = (ga_ref[...].astype(f32) * _dot(oa_ref[...], wa_ref[...])
                  + gb_ref[...].astype(f32) * _dot(c, wb_ref[...]))
        o_ref[...] = h_ref[...] + _rms(_dot(merged.astype(bf16), wo_ref[...]), gpost_ref[...])

    i = pl.program_id(0)
    pl.when(i < na)(lambda: run(oap_ref, _ln_silu(cp_ref[...], lng_ref[...], lnb_ref[...]), gap_ref, gbp_ref))
    pl.when(i >= na)(lambda: run(oas_ref, cs_ref[...], gas_ref, gbs_ref))


def _merge(oa_p, oa_s, c_p, c_s, gates_p, gates_s, h, wa, wb, wo, ln_g, ln_b, gpost, tm=256):
    t = h.shape[0]
    na = oa_p.shape[0] // tm
    act = pl.BlockSpec((tm, D_MODEL), lambda i: (i, 0))
    vec = pl.BlockSpec((1, D_MODEL), lambda i: (0, 0))
    src = lambda col: _split_specs((tm, D_MODEL), na, 2, col)
    wsp = pl.BlockSpec((D_MODEL, D_MODEL), lambda i: (0, 0), pipeline_mode=pl.Buffered(1))
    return pl.pallas_call(
        functools.partial(_merge_kernel, na=na),
        grid=(t // tm,),
        in_specs=src(0) + src(0) + src(0) + src(1) + [act, wsp, wsp, wsp, vec, vec, vec],
        out_specs=act,
        out_shape=jax.ShapeDtypeStruct((t, D_MODEL), f32),
        compiler_params=_params("parallel"),
        name="merge",
    )(oa_p, oa_s, c_p, c_s, gates_p, gates_s, gates_p, gates_s, h, wa, wb, wo, ln_g, ln_b, gpost)


def _ple_kernel(hn_ref, pa_ref, pb_ref, h_ref, wg_ref, wp_ref, gpost_ref, oa_ref, ob_ref, *, na):
    i = pl.program_id(0)
    gate = _sigmoid(_dot(hn_ref[...], wg_ref[...]))

    def out(p_ref):
        v = gate * _dot(p_ref[...].astype(bf16), wp_ref[...])
        return h_ref[...] + _rms(v, gpost_ref[...])

    @pl.when(i < na)
    def _():
        oa_ref[...] = out(pa_ref)

    @pl.when(i >= na)
    def _():
        ob_ref[...] = out(pb_ref)


def _ple(hn, pa, pb, h, wg, wp, gpost, tm=512):
    na, nb = pa.shape[0] // tm, pb.shape[0] // tm
    row = lambda i: (i, 0)
    first = lambda i: (jnp.minimum(i, na - 1), 0)
    second = lambda i: (jnp.maximum(i - na, 0), 0)
    act = pl.BlockSpec((tm, D_MODEL), row)
    return pl.pallas_call(
        functools.partial(_ple_kernel, na=na),
        grid=(na + nb,),
        in_specs=[act, pl.BlockSpec((tm, PLE_DIM), first), pl.BlockSpec((tm, PLE_DIM), second), act,
                  pl.BlockSpec((D_MODEL, D_MODEL), lambda i: (0, 0), pipeline_mode=pl.Buffered(1)),
                  pl.BlockSpec((PLE_DIM, D_MODEL), lambda i: (0, 0)),
                  pl.BlockSpec((1, D_MODEL), lambda i: (0, 0))],
        out_specs=[pl.BlockSpec((tm, D_MODEL), first), pl.BlockSpec((tm, D_MODEL), second)],
        out_shape=[jax.ShapeDtypeStruct((na * tm, D_MODEL), f32), jax.ShapeDtypeStruct((nb * tm, D_MODEL), f32)],
        compiler_params=_params("arbitrary"),
        name="ple",
    )(hn, pa, pb, h, wg, wp, gpost)


def _layer(xp, xs, pp, ps, nb, seq, s0, qkv_buf, glu_buf, w):
    (ffn1_pre, ffn1_w_gu, ffn1_w_down, ffn1_post, mix_pre, w_in, w_short_conv, a_log, dt_bias,
     o_norm, w_dw_conv, b_dw_conv, ln_g, ln_b, w_branch_a, w_branch_b, w_out, mix_post,
     ffn2_pre, ffn2_w_gu, ffn2_w_down, ffn2_post, ple_pre, w_ple_gate, w_ple_proj,
     ple_post) = w
    tp = nb * seq
    ns = s0.shape[1]
    row = lambda v: v.astype(f32).reshape(1, -1)
    cast = lambda v: v.astype(bf16)

    h1, u = _ffn([xp, xs], row(ffn1_pre), cast(ffn1_w_gu), cast(ffn1_w_down), row(ffn1_post), row(mix_pre))

    cast_t = lambda lo, hi: jnp.swapaxes(w_in[0, :, lo:hi], 0, 1).astype(bf16)
    w_qkvz, w_mid, w_tail = cast_t(0, O_BETA), cast_t(O_BETA, O_GLU), cast_t(O_GLU, w_in.shape[2])
    qkvz = _proj(u, w_qkvz, 0, O_BETA, "none", f32)
    ba = _proj_ba(u, _ba_weight(w_mid.T), _group_lanes(a_log), _group_lanes(dt_bias))
    glu = _proj_glu(u, w_tail)

    w_conv = w_short_conv.astype(f32)
    onorm = row(o_norm)
    oa_p, s_p = _gdn_prompt(qkvz, ba, w_conv, onorm, nb, seq)
    oa_s, s_s = _gdn_sample(qkvz, qkv_buf.astype(f32), ba, w_conv, onorm, s0.astype(f32), tp)

    cw, cb, lg, lb = w_dw_conv.astype(f32), row(b_dw_conv), row(ln_g), row(ln_b)
    gates_p, c_p = _gates_conv_prompt(u, w_tail, glu, cw, cb, nb, seq)
    gates_s = _proj(u, w_tail, O_GATE - O_GLU, 2 * D_MODEL, "sigmoid", bf16, r0=tp, rows=xs.shape[0])
    c_s = _cconv_sample(glu, glu_buf.astype(f32), cw, cb, lg, lb, tp)

    h2 = _merge(oa_p, oa_s, c_p, c_s, gates_p, gates_s, h1, cast(w_branch_a), cast(w_branch_b), cast(w_out),
                lg, lb, row(mix_post))
    h3, hn = _ffn([h2], row(ffn2_pre), cast(ffn2_w_gu), cast(ffn2_w_down), row(ffn2_post), row(ple_pre))
    yp, ys = _ple(hn, pp, ps, h3, cast(w_ple_gate), cast(w_ple_proj), row(ple_post))

    tail = lambda a, width, n: jnp.stack([a[(b + 1) * seq - n:(b + 1) * seq, :width] for b in range(nb)])
    qkv_p = tail(qkvz, QKV_DIM, SHORT_CONV - 1)
    glu_p = tail(glu, CONV_CH, CONV_WIDTH - 1)
    qkv_s = qkvz[tp:, :QKV_DIM].reshape(ns, SEG_S, QKV_DIM)[:, SEG_S - (SHORT_CONV - 1):]
    glu_s = jnp.concatenate([glu_buf[0].astype(f32)[:, SEG_S:], glu[tp:].reshape(ns, SEG_S, CONV_CH)], axis=1)
    return (yp, ys, s_p, qkv_p, glu_p, s_s, qkv_s, glu_s)


def kernel(x_prompt, x_sample, p_prompt, p_sample, state_delta, state_qkv_conv, state_glu_conv, ffn1_pre, ffn1_w_gu, ffn1_w_down, ffn1_post, mix_pre, w_in, w_short_conv, a_log, dt_bias, o_norm, w_dw_conv, b_dw_conv, ln_g, ln_b, w_branch_a, w_branch_b, w_out, mix_post, ffn2_pre, ffn2_w_gu, ffn2_w_down, ffn2_post, ple_pre, w_ple_gate, w_ple_proj, ple_post):
    weights = (ffn1_pre, ffn1_w_gu, ffn1_w_down, ffn1_post, mix_pre, w_in, w_short_conv, a_log,
               dt_bias, o_norm, w_dw_conv, b_dw_conv, ln_g, ln_b, w_branch_a, w_branch_b,
               w_out, mix_post, ffn2_pre, ffn2_w_gu, ffn2_w_down, ffn2_post, ple_pre,
               w_ple_gate, w_ple_proj, ple_post)
    nb, seq, _ = x_prompt.shape
    ns, ls, _ = x_sample.shape
    assert ls == SEG_S and seq % CHUNK == 0 and ns % (CHUNK // SEG_S) == 0
    depth = ffn1_pre.shape[0]
    tp = nb * seq
    xp, xs = x_prompt.reshape(tp, D_MODEL), x_sample.reshape(ns * ls, D_MODEL)
    outs = [[] for _ in range(6)]
    for i in range(depth):
        wi = tuple(wt[i:i + 1] if wt is w_in else wt[i] for wt in weights)
        xp, xs, s_p, q_p, g_p, s_s, q_s, g_s = _layer(
            xp, xs, p_prompt[i].reshape(tp, PLE_DIM), p_sample[i].reshape(ns * ls, PLE_DIM), nb, seq,
            state_delta[i:i + 1], state_qkv_conv[i:i + 1], state_glu_conv[i:i + 1], wi)
        for lst, v in zip(outs, (s_p, q_p, g_p, s_s, q_s, g_s)):
            lst.append(v)
    return (xp.reshape(nb, seq, D_MODEL), xs.reshape(ns, ls, D_MODEL)) + tuple(jnp.stack(lst) for lst in outs)
```

```python
import functools

import jax
import jax.numpy as jnp
from jax import lax
from jax.experimental import pallas as pl
from jax.experimental.pallas import tpu as pltpu

f32 = jnp.float32
bf16 = jnp.bfloat16

D_MODEL = 2048
N_HEADS = 16
HEAD_K = 128
HEAD_V = 128
KEY_DIM = N_HEADS * HEAD_K
VAL_DIM = N_HEADS * HEAD_V
QKV_DIM = 2 * KEY_DIM + VAL_DIM
SHORT_CONV = 4
CONV_CH = D_MODEL
CONV_WIDTH = 31
D_FF = 5632
PLE_DIM = 256
EPS = 1e-6

O_Z = QKV_DIM
O_BETA = O_Z + VAL_DIM
O_A = O_BETA + N_HEADS
O_GLU = O_A + N_HEADS
O_GATE = O_GLU + 2 * CONV_CH

CHUNK = 64
HG = 8
NHG = N_HEADS // HG
HW = HG * HEAD_K
SEG_S = 8
LOCAL_CHUNKS = 4

VMEM_LIMIT = 56 * 1024 * 1024


def _sigmoid(x):
    return 1.0 / (1.0 + jnp.exp(-x))


def _silu(x):
    return x * _sigmoid(x)


def _rms(x, g):
    return x * lax.rsqrt(jnp.mean(x * x, axis=-1, keepdims=True) + EPS) * g


def _dot(a, b):
    return jnp.dot(a, b, preferred_element_type=f32)


def _dot_nt(a, b):
    return lax.dot_general(a, b, (((1,), (1,)), ((), ())), preferred_element_type=f32)


def _params(*sem):
    return pltpu.CompilerParams(dimension_semantics=sem, vmem_limit_bytes=VMEM_LIMIT)


def _per_source(i, na, refs, fn):
    if len(refs) == 1:
        fn(refs[0])
    else:
        pl.when(i < na)(lambda: fn(refs[0]))
        pl.when(i >= na)(lambda: fn(refs[1]))


def _split_specs(block, na, nsrc, col=0):
    if nsrc == 1:
        return [pl.BlockSpec(block, lambda i, *_: (i, col))]
    return [pl.BlockSpec(block, lambda i, *_: (jnp.minimum(i, na - 1), col)),
            pl.BlockSpec(block, lambda i, *_: (jnp.maximum(i - na, 0), col))]


def _ffn_kernel(*refs, na, nsrc):
    x_refs = refs[:nsrc]
    gpre_ref, wg_ref, wu_ref, wd_ref, gpost_ref, gnext_ref, h_ref, hn_ref, xn_scr, acc_scr = refs[nsrc:]
    i, j = pl.program_id(0), pl.program_id(1)

    @pl.when(j == 0)
    def _():
        def norm_in(x_ref):
            xn_scr[...] = _rms(x_ref[...], gpre_ref[...]).astype(bf16)

        _per_source(i, na, x_refs, norm_in)
        acc_scr[...] = jnp.zeros_like(acc_scr)

    xn = xn_scr[...]
    a = _silu(_dot(xn, wg_ref[...])) * _dot(xn, wu_ref[...])
    acc_scr[...] += _dot(a.astype(bf16), wd_ref[...])

    @pl.when(j == pl.num_programs(1) - 1)
    def _():
        def residual_out(x_ref):
            h = x_ref[...] + 0.5 * _rms(acc_scr[...], gpost_ref[...])
            h_ref[...] = h
            hn_ref[...] = _rms(h, gnext_ref[...]).astype(bf16)

        _per_source(i, na, x_refs, residual_out)


def _ffn(xs, gpre, w_gu, w_down, gpost, gnext, tm=512, tf=512):
    na = xs[0].shape[0] // tm
    nt = sum(x.shape[0] for x in xs) // tm
    nf = D_FF // tf
    row = lambda i, j: (i, 0)
    vec = pl.BlockSpec((1, D_MODEL), lambda i, j: (0, 0))
    return pl.pallas_call(
        functools.partial(_ffn_kernel, na=na, nsrc=len(xs)),
        grid=(nt, nf),
        in_specs=_split_specs((tm, D_MODEL), na, len(xs))
        + [vec,
           pl.BlockSpec((D_MODEL, tf), lambda i, j: (0, j)),
           pl.BlockSpec((D_MODEL, tf), lambda i, j: (0, j + nf)),
           pl.BlockSpec((tf, D_MODEL), lambda i, j: (j, 0)), vec, vec],
        out_specs=[pl.BlockSpec((tm, D_MODEL), row), pl.BlockSpec((tm, D_MODEL), row)],
        out_shape=[jax.ShapeDtypeStruct((nt * tm, D_MODEL), f32), jax.ShapeDtypeStruct((nt * tm, D_MODEL), bf16)],
        scratch_shapes=[pltpu.VMEM((tm, D_MODEL), bf16), pltpu.VMEM((tm, D_MODEL), f32)],
        compiler_params=_params("parallel", "arbitrary"),
        name="ffn",
    )(*xs, gpre, w_gu, w_gu, w_down, gpost, gnext)


def _proj_kernel(x_ref, wt_ref, o_ref, *, act):
    y = _dot_nt(x_ref[...], wt_ref[...])
    if act == "sigmoid":
        y = _sigmoid(y)
    o_ref[...] = y.astype(o_ref.dtype)


def _proj(x, wt, n0, n, act, out_dtype, r0=0, rows=None, tm=1024, tn=1024):
    t, k = x.shape
    rows = t if rows is None else rows
    i0, j0 = r0 // tm, n0 // tn
    return pl.pallas_call(
        functools.partial(_proj_kernel, act=act),
        grid=(rows // tm, n // tn),
        in_specs=[pl.BlockSpec((tm, k), lambda i, j: (i + i0, 0)),
                  pl.BlockSpec((tn, k), lambda i, j: (j + j0, 0))],
        out_specs=pl.BlockSpec((tm, tn), lambda i, j: (i, j)),
        out_shape=jax.ShapeDtypeStruct((rows, n), out_dtype),
        compiler_params=_params("parallel", "arbitrary"),
        name="proj_" + act,
    )(x, wt)


def _glu_kernel(x_ref, wa_ref, wb_ref, o_ref):
    x = x_ref[...]
    o_ref[...] = _dot_nt(x, wa_ref[...]) * _sigmoid(_dot_nt(x, wb_ref[...]))


def _proj_glu(x, wt, tm=1024, tn=512):
    t, k = x.shape
    nj = CONV_CH // tn
    return pl.pallas_call(
        _glu_kernel,
        grid=(t // tm, nj),
        in_specs=[pl.BlockSpec((tm, k), lambda i, j: (i, 0)),
                  pl.BlockSpec((tn, k), lambda i, j: (j, 0)),
                  pl.BlockSpec((tn, k), lambda i, j: (j + nj, 0))],
        out_specs=pl.BlockSpec((tm, tn), lambda i, j: (i, j)),
        out_shape=jax.ShapeDtypeStruct((t, CONV_CH), f32),
        compiler_params=_params("parallel", "arbitrary"),
        name="proj_glu",
    )(x, wt, wt)


def _ba_kernel(x_ref, w_ref, alog_ref, dtb_ref, o_ref):
    y = _dot(x_ref[...], w_ref[...])
    lane = lax.broadcasted_iota(jnp.int32, y.shape, 1)
    ya = y + dtb_ref[...]
    softplus = jnp.maximum(ya, 0.0) + jnp.log(1.0 + jnp.exp(-jnp.abs(ya)))
    o_ref[...] = jnp.where(lane < HG, _sigmoid(y), -jnp.exp(alog_ref[...]) * softplus)


def _proj_ba(x, w_ba, alog_row, dtb_row, tm=1024):
    t, k = x.shape
    vec = pl.BlockSpec((1, 128), lambda i, j: (0, j))
    return pl.pallas_call(
        _ba_kernel,
        grid=(t // tm, NHG),
        in_specs=[pl.BlockSpec((tm, k), lambda i, j: (i, 0)),
                  pl.BlockSpec((k, 128), lambda i, j: (0, j)), vec, vec],
        out_specs=pl.BlockSpec((tm, 128), lambda i, j: (i, j)),
        out_shape=jax.ShapeDtypeStruct((t, NHG * 128), f32),
        compiler_params=_params("parallel", "arbitrary"),
        name="proj_ba",
    )(x, w_ba, alog_row, dtb_row)


def _group_lanes(v):
    r = jnp.zeros((NHG, 128), f32).at[:, HG:2 * HG].set(v.astype(f32).reshape(NHG, HG))
    return r.reshape(1, NHG * 128)


def _ba_weight(w_ba):
    wb = w_ba[:, :N_HEADS].reshape(D_MODEL, NHG, HG)
    wa = w_ba[:, N_HEADS:].reshape(D_MODEL, NHG, HG)
    w = jnp.zeros((D_MODEL, NHG, 128), w_ba.dtype).at[:, :, :HG].set(wb).at[:, :, HG:2 * HG].set(wa)
    return w.reshape(D_MODEL, NHG * 128)


def _seq_masks(n, seg):
    r = lax.broadcasted_iota(jnp.int32, (n, n), 0)
    c = lax.broadcasted_iota(jnp.int32, (n, n), 1)
    same = (r // seg) == (c // seg)
    return same & (r >= c), same & (r > c), same


def _mask_dot(mask, x):
    l = jnp.where(mask, 1.0, 0.0).astype(bf16)
    hi = x.astype(bf16)
    r1 = x - hi.astype(f32)
    mid = r1.astype(bf16)
    lo = (r1 - mid.astype(f32)).astype(bf16)
    return _dot(l, hi) + _dot(l, mid) + _dot(l, lo)


def _mm(a, b):
    return _dot(a.astype(bf16), b.astype(bf16))


def _unit_lower_solve(ms, rhss, seg):
    n = range(len(ms))
    r = lax.broadcasted_iota(jnp.int32, (CHUNK, CHUNK), 0)
    c = lax.broadcasted_iota(jnp.int32, (CHUNK, CHUNK), 1)
    eye = (r == c).astype(f32)
    if seg <= 8:
        d, lo = ms, None
    else:
        blk = (r // 16) == (c // 16)
        d = [jnp.where(blk, m, 0.0) for m in ms]
        lo = [ms[i] - d[i] for i in n]
    d2 = [_mm(d[i], d[i]) for i in n]
    d3 = [_mm(d[i], d2[i]) for i in n]
    d4 = [_mm(d2[i], d2[i]) for i in n]
    a = [eye - d[i] + d2[i] - d3[i] for i in n]
    td = [a[i] + _mm(a[i], d4[i]) for i in n]
    if seg > 8:
        d8 = [_mm(d4[i], d4[i]) for i in n]
        td = [td[i] + _mm(td[i], d8[i]) for i in n]
    y = [_mm(td[i], rhss[i]) for i in n]
    if lo is None:
        return y
    e = [_mm(td[i], lo[i]) for i in n]
    e2 = [_mm(e[i], e[i]) for i in n]
    w = [y[i] - _mm(e[i], y[i]) for i in n]
    return [w[i] + _mm(e2[i], w[i]) for i in n]


def _chunk_local(qs, ks, vs, betas, gcols, grows, gtots, incl, strict, seg):
    n = range(len(qs))
    qn = [q * lax.rsqrt(jnp.sum(q * q, axis=-1, keepdims=True) + EPS) * (HEAD_K ** -0.5) for q in qs]
    kn = [k * lax.rsqrt(jnp.sum(k * k, axis=-1, keepdims=True) + EPS) for k in ks]
    decay = [jnp.exp(jnp.where(incl, gcols[i] - grows[i], -jnp.inf)) for i in n]
    kb = [k.astype(bf16) for k in kn]
    kk = [_dot_nt(kb[i], kb[i]) for i in n]
    qk = [_dot_nt(qn[i].astype(bf16), kb[i]) for i in n]
    m = [jnp.where(strict, betas[i] * decay[i] * kk[i], 0.0) for i in n]
    gamma = [jnp.exp(g) for g in gcols]
    rhs = [jnp.concatenate([betas[i] * vs[i], (betas[i] * gamma[i]) * kn[i]], axis=1) for i in n]
    sol = _unit_lower_solve(m, rhs, seg)
    uv = [s[:, :HEAD_V] for s in sol]
    wk = [s[:, HEAD_V:] for s in sol]
    qkd = [qk[i] * decay[i] for i in n]
    qg = [qn[i] * gamma[i] for i in n]
    kd = [kn[i] * jnp.exp(gtots[i] - gcols[i]) for i in n]
    return uv, wk, qkd, qg, kd


def _out_gate(o, onorm, z):
    return (_rms(o, onorm) * _silu(z)).astype(bf16)


def _gdn_prompt_kernel(q_ref, k_ref, v_ref, z_ref, ba_ref, wq_ref, wk_ref, wv_ref, onorm_ref,
                       oa_ref, s_ref,
                       eq, ek, ev, s_scr, qc_s, kc_s, vc_s, gc_s, gt_s, gl_s, uv_s, wq_s, kdt_s, qkd_s,
                       *, rows):
    t = pl.program_id(2)
    nc = rows // CHUNK
    heads = range(HG)
    hs = [slice(h * HEAD_K, (h + 1) * HEAD_K) for h in heads]
    gs = [slice(HG + h, HG + h + 1) for h in heads]

    @pl.when(t == 0)
    def _():
        s_scr[...] = jnp.zeros_like(s_scr)
        for e in (eq, ek, ev):
            e[:, 0:8, :] = jnp.zeros((HG, 8, HEAD_K), f32)

    off = 8 - (SHORT_CONV - 1)
    for e, x_ref, w_ref, dst in ((eq, q_ref, wq_ref, qc_s), (ek, k_ref, wk_ref, kc_s), (ev, v_ref, wv_ref, vc_s)):
        for h in heads:
            e[h, 8:, :] = x_ref[:, hs[h]]
            for r0 in range(0, rows, CHUNK):
                acc = w_ref[0:1, hs[h]] * e[h, r0 + off:r0 + off + CHUNK, :]
                for j in range(1, SHORT_CONV):
                    acc += w_ref[j:j + 1, hs[h]] * e[h, r0 + off + j:r0 + off + j + CHUNK, :]
                dst[r0:r0 + CHUNK, hs[h]] = _silu(acc)

    incl_b, _, same_b = _seq_masks(rows, CHUNK)
    ba = ba_ref[...]
    gc_s[...] = _mask_dot(incl_b, ba)
    gtot_b = _mask_dot(same_b, ba)
    gt_s[...] = gtot_b
    gl_s[...] = jnp.exp(gtot_b)
    incl, strict, _ = _seq_masks(CHUNK, CHUNK)

    def local(cp, carry):
        cs = [cp * LOCAL_CHUNKS + i for i in range(LOCAL_CHUNKS)]
        rss = [pl.ds(pl.multiple_of(c * CHUNK, CHUNK), CHUNK) for c in cs]
        units = [(i, h) for i in range(LOCAL_CHUNKS) for h in heads]
        bac = [ba_ref[rs, :] for rs in rss]
        gc = [gc_s[rs, :] for rs in rss]
        gtot = [gt_s[rs, :] for rs in rss]
        gct = [g.T for g in gc]
        uv, wk, qkd, qg, kd = _chunk_local(
            [qc_s[rss[i], hs[h]] for i, h in units], [kc_s[rss[i], hs[h]] for i, h in units],
            [vc_s[rss[i], hs[h]] for i, h in units], [bac[i][:, h:h + 1] for i, h in units],
            [gc[i][:, gs[h]] for i, h in units], [gct[i][gs[h], :] for i, h in units],
            [gtot[i][:, gs[h]] for i, h in units], incl, strict, CHUNK)
        for n, (i, h) in enumerate(units):
            uv_s[rss[i], hs[h]] = uv[n]
            wq_s[cs[i], 0:CHUNK, hs[h]] = wk[n].astype(bf16)
            wq_s[cs[i], CHUNK:2 * CHUNK, hs[h]] = qg[n].astype(bf16)
            kdt_s[cs[i], h] = kd[n].T.astype(bf16)
            qkd_s[cs[i], h] = qkd[n].astype(bf16)
        return carry

    lax.fori_loop(0, nc // LOCAL_CHUNKS, local, 0)

    for c in range(nc):
        rs = slice(c * CHUNK, (c + 1) * CHUNK)
        s = [s_scr[h] for h in heads]
        p = [_dot(wq_s[c, :, hs[h]], s[h].astype(bf16)) for h in heads]
        u = [(uv_s[rs, hs[h]] - p[h][:CHUNK]).astype(bf16) for h in heads]
        o = [p[h][CHUNK:] + _dot(qkd_s[c, h], u[h]) for h in heads]
        ds = [_dot(kdt_s[c, h], u[h]) for h in heads]
        for h in heads:
            s_scr[h] = gl_s[c * CHUNK:c * CHUNK + 1, gs[h]] * s[h] + ds[h]
            oa_ref[rs, hs[h]] = _out_gate(o[h], onorm_ref[...], z_ref[rs, hs[h]])

    for e in (eq, ek, ev):
        e[:, 0:8, :] = e[:, rows:rows + 8, :]

    @pl.when(t == pl.num_programs(2) - 1)
    def _():
        s_ref[0] = s_scr[...]


def _gdn_prompt(qkvz, ba, w_conv, onorm, nb, seq, rows=256):
    nt = seq // rows
    nc = rows // CHUNK
    rb = lambda b, hg, t: b * nt + t
    col = lambda off: pl.BlockSpec((rows, HW), lambda b, hg, t: (rb(b, hg, t), off * NHG + hg))
    wcol = lambda off: pl.BlockSpec((SHORT_CONV, HW), lambda b, hg, t: (0, off * NHG + hg))
    return pl.pallas_call(
        functools.partial(_gdn_prompt_kernel, rows=rows),
        grid=(nb, NHG, nt),
        in_specs=[col(0), col(1), col(2), col(3),
                  pl.BlockSpec((rows, 128), lambda b, hg, t: (rb(b, hg, t), hg)),
                  wcol(0), wcol(1), wcol(2),
                  pl.BlockSpec((1, HEAD_V), lambda b, hg, t: (0, 0))],
        out_specs=[pl.BlockSpec((rows, HW), lambda b, hg, t: (rb(b, hg, t), hg)),
                   pl.BlockSpec((1, HG, HEAD_K, HEAD_V), lambda b, hg, t: (b, hg, 0, 0))],
        out_shape=[jax.ShapeDtypeStruct((nb * seq, VAL_DIM), bf16),
                   jax.ShapeDtypeStruct((nb, N_HEADS, HEAD_K, HEAD_V), f32)],
        scratch_shapes=[pltpu.VMEM((HG, rows + 8, HEAD_K), f32)] * 3
        + [pltpu.VMEM((HG, HEAD_K, HEAD_V), f32)]
        + [pltpu.VMEM((rows, HW), f32)] * 3
        + [pltpu.VMEM((rows, 128), f32)] * 3
        + [pltpu.VMEM((rows, HW), f32),
           pltpu.VMEM((nc, 2 * CHUNK, HW), bf16),
           pltpu.VMEM((nc, HG, HEAD_K, CHUNK), bf16),
           pltpu.VMEM((nc, HG, CHUNK, CHUNK), bf16)],
        compiler_params=_params("parallel", "parallel", "arbitrary"),
        name="gdn_prompt",
    )(qkvz, qkvz, qkvz, qkvz, ba, w_conv, w_conv, w_conv, onorm)


def _gdn_sample_kernel(q_ref, k_ref, v_ref, qst_ref, kst_ref, vst_ref, z_ref, ba_ref, wq_ref, wk_ref, wv_ref,
                       onorm_ref, s0_ref, oa_ref, s_ref, e_scr):
    nseq = CHUNK // SEG_S
    heads = range(HG)
    seqs = range(nseq)
    hs = [slice(h * HEAD_K, (h + 1) * HEAD_K) for h in heads]
    gs = [slice(HG + h, HG + h + 1) for h in heads]
    js = [slice(j * SEG_S, (j + 1) * SEG_S) for j in seqs]
    incl, strict, same = _seq_masks(CHUNK, SEG_S)
    hist = SHORT_CONV - 1

    def conv(x_ref, st_ref, w_ref):
        e_scr[:, SEG_S - hist:SEG_S, :] = st_ref[...]
        e_scr[:, SEG_S:, :] = x_ref[...].reshape(nseq, SEG_S, HW)
        off = SEG_S - hist
        acc = w_ref[0:1, :] * e_scr[:, off:off + SEG_S, :].reshape(CHUNK, HW)
        for j in range(1, SHORT_CONV):
            acc += w_ref[j:j + 1, :] * e_scr[:, off + j:off + j + SEG_S, :].reshape(CHUNK, HW)
        return _silu(acc)

    qc, kc, vc = conv(q_ref, qst_ref, wq_ref), conv(k_ref, kst_ref, wk_ref), conv(v_ref, vst_ref, wv_ref)
    bac = ba_ref[...]
    gc = _mask_dot(incl, bac)
    gtot = _mask_dot(same, bac)
    gct = gc.T
    glv = jnp.exp(gtot)
    uv, wk, qkd, qg, kd = _chunk_local(
        [qc[:, hs[h]] for h in heads], [kc[:, hs[h]] for h in heads], [vc[:, hs[h]] for h in heads],
        [bac[:, h:h + 1] for h in heads], [gc[:, gs[h]] for h in heads], [gct[gs[h], :] for h in heads],
        [gtot[:, gs[h]] for h in heads], incl, strict, SEG_S)
    kdt = [kd[h].T.astype(bf16) for h in heads]
    p = [[_dot(jnp.concatenate([wk[h][js[j]], qg[h][js[j]]], axis=0).astype(bf16),
               s0_ref[j, h].astype(bf16)) for j in seqs] for h in heads]
    u = [[uv[h][js[j]] - p[h][j][:SEG_S] for j in seqs] for h in heads]
    zeros = jnp.zeros((SEG_S, HEAD_V), f32)
    for h in heads:
        for j in seqs:
            u_rows = jnp.concatenate([u[h][j] if i == j else zeros for i in seqs], axis=0).astype(bf16)
            s_ref[j, h] = glv[j * SEG_S:j * SEG_S + 1, gs[h]] * s0_ref[j, h] + _dot(kdt[h], u_rows)
    for h in heads:
        u_all = jnp.concatenate(u[h], axis=0).astype(bf16)
        o = jnp.concatenate([p[h][j][SEG_S:] for j in seqs], axis=0) + _dot(qkd[h].astype(bf16), u_all)
        oa_ref[:, hs[h]] = _out_gate(o, onorm_ref[...], z_ref[:, hs[h]])


def _gdn_sample(qkvz, qkv_state, ba, w_conv, onorm, s0, row0):
    nb = qkv_state.shape[1]
    nseq = CHUNK // SEG_S
    rb0 = row0 // CHUNK
    col = lambda off: pl.BlockSpec((CHUNK, HW), lambda g, hg: (rb0 + g, off * NHG + hg))
    st = lambda off: pl.BlockSpec((None, nseq, SHORT_CONV - 1, HW), lambda g, hg: (0, g, 0, off * NHG + hg))
    wcol = lambda off: pl.BlockSpec((SHORT_CONV, HW), lambda g, hg: (0, off * NHG + hg))
    sspec = pl.BlockSpec((nseq, HG, HEAD_K, HEAD_V), lambda g, hg: (g, hg, 0, 0))
    return pl.pallas_call(
        _gdn_sample_kernel,
        grid=(nb // nseq, NHG),
        in_specs=[col(0), col(1), col(2), st(0), st(1), st(2), col(3),
                  pl.BlockSpec((CHUNK, 128), lambda g, hg: (rb0 + g, hg)),
                  wcol(0), wcol(1), wcol(2),
                  pl.BlockSpec((1, HEAD_V), lambda g, hg: (0, 0)),
                  pl.BlockSpec((None, nseq, HG, HEAD_K, HEAD_V), lambda g, hg: (0, g, hg, 0, 0))],
        out_specs=[pl.BlockSpec((CHUNK, HW), lambda g, hg: (g, hg)), sspec],
        out_shape=[jax.ShapeDtypeStruct((nb * SEG_S, VAL_DIM), bf16), jax.ShapeDtypeStruct(s0.shape[1:], f32)],
        scratch_shapes=[pltpu.VMEM((nseq, 2 * SEG_S, HW), f32)],
        compiler_params=_params("parallel", "parallel"),
        name="gdn_sample",
    )(qkvz, qkvz, qkvz, qkv_state, qkv_state, qkv_state, qkvz, ba, w_conv, w_conv, w_conv, onorm, s0)


CONV_LC = 128
CONV_RB = 64


def _ln_silu(c, g, b):
    mu = jnp.mean(c, axis=-1, keepdims=True)
    xc = c - mu
    y = xc * lax.rsqrt(jnp.mean(xc * xc, axis=-1, keepdims=True) + EPS) * g + b
    return _silu(y).astype(bf16)


CONV_PAD = 32


def _gates_conv_kernel(u_ref, wg_ref, x_ref, cw_ref, cb_ref, g_ref, c_ref, ext, *, tiles_per_seq):
    j, i = pl.program_id(0), pl.program_id(1)
    tm, cw = x_ref.shape

    @pl.when((i == 0) & (j == 0))
    def _():
        ext[:, tm:, :] = jnp.zeros((cw // CONV_LC, CONV_PAD, CONV_LC), f32)

    first = (i % tiles_per_seq) == 0
    off = CONV_PAD - (CONV_WIDTH - 1)
    nlc = cw // CONV_LC
    gw = g_ref.shape[1] // nlc
    for lc in range(nlc):
        gs = slice(lc * gw, (lc + 1) * gw)
        g_ref[:, gs] = _sigmoid(_dot_nt(u_ref[...], wg_ref[gs, :])).astype(g_ref.dtype)
        ls = slice(lc * CONV_LC, (lc + 1) * CONV_LC)
        ext[lc, 0:CONV_PAD, :] = jnp.where(first, 0.0, ext[lc, tm:tm + CONV_PAD, :])
        ext[lc, CONV_PAD:, :] = x_ref[:, ls]
        for r0 in range(0, tm, CONV_RB):
            acc = jnp.broadcast_to(cb_ref[:, ls], (CONV_RB, CONV_LC))
            for t in range(CONV_WIDTH):
                acc += cw_ref[t:t + 1, ls] * ext[lc, r0 + off + t:r0 + off + t + CONV_RB, :]
            c_ref[r0:r0 + CONV_RB, ls] = acc


def _gates_conv_prompt(u, w_tail, glu, cw, cb, nb, seq, tm=1024, tn=1024):
    k = u.shape[1]
    ng = 2 * D_MODEL // tn
    cwid = CONV_CH // ng
    j0 = (O_GATE - O_GLU) // tn
    return pl.pallas_call(
        functools.partial(_gates_conv_kernel, tiles_per_seq=seq // tm),
        grid=(ng, nb * seq // tm),
        in_specs=[pl.BlockSpec((tm, k), lambda j, i: (i, 0)),
                  pl.BlockSpec((tn, k), lambda j, i: (j + j0, 0)),
                  pl.BlockSpec((tm, cwid), lambda j, i: (i, j)),
                  pl.BlockSpec((CONV_WIDTH, cwid), lambda j, i: (0, j)),
                  pl.BlockSpec((1, cwid), lambda j, i: (0, j))],
        out_specs=[pl.BlockSpec((tm, tn), lambda j, i: (i, j)), pl.BlockSpec((tm, cwid), lambda j, i: (i, j))],
        out_shape=[jax.ShapeDtypeStruct((nb * seq, 2 * D_MODEL), bf16),
                   jax.ShapeDtypeStruct((nb * seq, CONV_CH), f32)],
        scratch_shapes=[pltpu.VMEM((cwid // CONV_LC, tm + CONV_PAD, CONV_LC), f32)],
        compiler_params=_params("arbitrary", "arbitrary"),
        name="gates_conv",
    )(u, w_tail, glu, cw, cb)


def _cconv_sample_kernel(x_ref, st_ref, w_ref, b_ref, g_ref, beta_ref, o_ref, e_scr, cbuf):
    nseq = CHUNK // SEG_S
    hist = CONV_WIDTH - 1
    for lc in range(CONV_CH // CONV_LC):
        ls = slice(lc * CONV_LC, (lc + 1) * CONV_LC)
        e_scr[lc, :, 0:hist, :] = st_ref[:, :, ls]
        e_scr[lc, :, hist:hist + SEG_S, :] = x_ref[:, ls].reshape(nseq, SEG_S, CONV_LC)
        acc = jnp.broadcast_to(b_ref[:, ls], (CHUNK, CONV_LC))
        for j in range(CONV_WIDTH):
            acc += w_ref[j:j + 1, ls] * e_scr[lc, :, j:j + SEG_S, :].reshape(CHUNK, CONV_LC)
        cbuf[:, ls] = acc
    o_ref[...] = _ln_silu(cbuf[...], g_ref[...], beta_ref[...])


def _cconv_sample(glu, glu_state, w, b, g, beta, row0):
    _, nb, hist, _ = glu_state.shape
    nseq = CHUNK // SEG_S
    rb0 = row0 // CHUNK
    vec = pl.BlockSpec((1, CONV_CH), lambda i: (0, 0))
    return pl.pallas_call(
        _cconv_sample_kernel,
        grid=(nb // nseq,),
        in_specs=[pl.BlockSpec((CHUNK, CONV_CH), lambda i: (rb0 + i, 0)),
                  pl.BlockSpec((None, nseq, hist, CONV_CH), lambda i: (0, i, 0, 0)),
                  pl.BlockSpec((CONV_WIDTH, CONV_CH), lambda i: (0, 0)), vec, vec, vec],
        out_specs=pl.BlockSpec((CHUNK, CONV_CH), lambda i: (i, 0)),
        out_shape=jax.ShapeDtypeStruct((nb * SEG_S, CONV_CH), bf16),
        scratch_shapes=[pltpu.VMEM((CONV_CH // CONV_LC, nseq, hist + SEG_S, CONV_LC), f32),
                        pltpu.VMEM((CHUNK, CONV_CH), f32)],
        compiler_params=_params("parallel"),
        name="cconv_sample",
    )(glu, glu_state, w, b, g, beta)


def _merge_kernel(oap_ref, oas_ref, cp_ref, cs_ref, gap_ref, gas_ref, gbp_ref, gbs_ref, h_ref,
                  wa_ref, wb_ref, wo_ref, lng_ref, lnb_ref, gpost_ref, o_ref, *, na):
    def run(oa_ref, c, ga_ref, gb_ref):
        merged = (ga_ref[...].astype(f32) * _dot(oa_ref[...], wa_ref[...])
                  + gb_ref[...].astype(f32) * _dot(c, wb_ref[...]))
        o_ref[...] = h_ref[...] + _rms(_dot(merged.astype(bf16), wo_ref[...]), gpost_ref[...])

    i = pl.program_id(0)
    pl.when(i < na)(lambda: run(oap_ref, _ln_silu(cp_ref[...], lng_ref[...], lnb_ref[...]), gap_ref, gbp_ref))
    pl.when(i >= na)(lambda: run(oas_ref, cs_ref[...], gas_ref, gbs_ref))


def _merge(oa_p, oa_s, c_p, c_s, gates_p, gates_s, h, wa, wb, wo, ln_g, ln_b, gpost, tm=256):
    t = h.shape[0]
    na = oa_p.shape[0] // tm
    act = pl.BlockSpec((tm, D_MODEL), lambda i: (i, 0))
    vec = pl.BlockSpec((1, D_MODEL), lambda i: (0, 0))
    src = lambda col: _split_specs((tm, D_MODEL), na, 2, col)
    wsp = pl.BlockSpec((D_MODEL, D_MODEL), lambda i: (0, 0), pipeline_mode=pl.Buffered(1))
    return pl.pallas_call(
        functools.partial(_merge_kernel, na=na),
        grid=(t // tm,),
        in_specs=src(0) + src(0) + src(0) + src(1) + [act, wsp, wsp, wsp, vec, vec, vec],
        out_specs=act,
        out_shape=jax.ShapeDtypeStruct((t, D_MODEL), f32),
        compiler_params=_params("parallel"),
        name="merge",
    )(oa_p, oa_s, c_p, c_s, gates_p, gates_s, gates_p, gates_s, h, wa, wb, wo, ln_g, ln_b, gpost)


def _ple_kernel(hn_ref, pa_ref, pb_ref, h_ref, wg_ref, wp_ref, gpost_ref, oa_ref, ob_ref, *, na):
    i = pl.program_id(0)
    gate = _sigmoid(_dot(hn_ref[...], wg_ref[...]))

    def out(p_ref):
        v = gate * _dot(p_ref[...].astype(bf16), wp_ref[...])
        return h_ref[...] + _rms(v, gpost_ref[...])

    @pl.when(i < na)
    def _():
        oa_ref[...] = out(pa_ref)

    @pl.when(i >= na)
    def _():
        ob_ref[...] = out(pb_ref)


def _ple(hn, pa, pb, h, wg, wp, gpost, tm=512):
    na, nb = pa.shape[0] // tm, pb.shape[0] // tm
    row = lambda i: (i, 0)
    first = lambda i: (jnp.minimum(i, na - 1), 0)
    second = lambda i: (jnp.maximum(i - na, 0), 0)
    act = pl.BlockSpec((tm, D_MODEL), row)
    return pl.pallas_call(
        functools.partial(_ple_kernel, na=na),
        grid=(na + nb,),
        in_specs=[act, pl.BlockSpec((tm, PLE_DIM), first), pl.BlockSpec((tm, PLE_DIM), second), act,
                  pl.BlockSpec((D_MODEL, D_MODEL), lambda i: (0, 0), pipeline_mode=pl.Buffered(1)),
                  pl.BlockSpec((PLE_DIM, D_MODEL), lambda i: (0, 0)),
                  pl.BlockSpec((1, D_MODEL), lambda i: (0, 0))],
        out_specs=[pl.BlockSpec((tm, D_MODEL), first), pl.BlockSpec((tm, D_MODEL), second)],
        out_shape=[jax.ShapeDtypeStruct((na * tm, D_MODEL), f32), jax.ShapeDtypeStruct((nb * tm, D_MODEL), f32)],
        compiler_params=_params("arbitrary"),
        name="ple",
    )(hn, pa, pb, h, wg, wp, gpost)


def _layer(xp, xs, pp, ps, nb, seq, s0, qkv_buf, glu_buf, w):
    (ffn1_pre, ffn1_w_gu, ffn1_w_down, ffn1_post, mix_pre, w_in, w_short_conv, a_log, dt_bias,
     o_norm, w_dw_conv, b_dw_conv, ln_g, ln_b, w_branch_a, w_branch_b, w_out, mix_post,
     ffn2_pre, ffn2_w_gu, ffn2_w_down, ffn2_post, ple_pre, w_ple_gate, w_ple_proj,
     ple_post) = w
    tp = nb * seq
    ns = s0.shape[1]
    row = lambda v: v.astype(f32).reshape(1, -1)
    cast = lambda v: v.astype(bf16)

    h1, u = _ffn([xp, xs], row(ffn1_pre), cast(ffn1_w_gu), cast(ffn1_w_down), row(ffn1_post), row(mix_pre))

    cast_t = lambda lo, hi: jnp.swapaxes(w_in[0, :, lo:hi], 0, 1).astype(bf16)
    w_qkvz, w_mid, w_tail = cast_t(0, O_BETA), cast_t(O_BETA, O_GLU), cast_t(O_GLU, w_in.shape[2])
    qkvz = _proj(u, w_qkvz, 0, O_BETA, "none", f32)
    ba = _proj_ba(u, _ba_weight(w_mid.T), _group_lanes(a_log), _group_lanes(dt_bias))
    glu = _proj_glu(u, w_tail)

    w_conv = w_short_conv.astype(f32)
    onorm = row(o_norm)
    oa_p, s_p = _gdn_prompt(qkvz, ba, w_conv, onorm, nb, seq)
    oa_s, s_s = _gdn_sample(qkvz, qkv_buf.astype(f32), ba, w_conv, onorm, s0.astype(f32), tp)

    cw, cb, lg, lb = w_dw_conv.astype(f32), row(b_dw_conv), row(ln_g), row(ln_b)
    gates_p, c_p = _gates_conv_prompt(u, w_tail, glu, cw, cb, nb, seq)
    gates_s = _proj(u, w_tail, O_GATE - O_GLU, 2 * D_MODEL, "sigmoid", bf16, r0=tp, rows=xs.shape[0])
    c_s = _cconv_sample(glu, glu_buf.astype(f32), cw, cb, lg, lb, tp)

    h2 = _merge(oa_p, oa_s, c_p, c_s, gates_p, gates_s, h1, cast(w_branch_a), cast(w_branch_b), cast(w_out),
                lg, lb, row(mix_post))
    h3, hn = _ffn([h2], row(ffn2_pre), cast(ffn2_w_gu), cast(ffn2_w_down), row(ffn2_post), row(ple_pre))
    yp, ys = _ple(hn, pp, ps, h3, cast(w_ple_gate), cast(w_ple_proj), row(ple_post))

    tail = lambda a, width, n: jnp.stack([a[(b + 1) * seq - n:(b + 1) * seq, :width] for b in range(nb)])
    qkv_p = tail(qkvz, QKV_DIM, SHORT_CONV - 1)
    glu_p = tail(glu, CONV_CH, CONV_WIDTH - 1)
    qkv_s = qkvz[tp:, :QKV_DIM].reshape(ns, SEG_S, QKV_DIM)[:, SEG_S - (SHORT_CONV - 1):]
    glu_s = jnp.concatenate([glu_buf[0].astype(f32)[:, SEG_S:], glu[tp:].reshape(ns, SEG_S, CONV_CH)], axis=1)
    return (yp, ys, s_p, qkv_p, glu_p, s_s, qkv_s, glu_s)


def kernel(x_prompt, x_sample, p_prompt, p_sample, state_delta, state_qkv_conv, state_glu_conv, ffn1_pre, ffn1_w_gu, ffn1_w_down, ffn1_post, mix_pre, w_in, w_short_conv, a_log, dt_bias, o_norm, w_dw_conv, b_dw_conv, ln_g, ln_b, w_branch_a, w_branch_b, w_out, mix_post, ffn2_pre, ffn2_w_gu, ffn2_w_down, ffn2_post, ple_pre, w_ple_gate, w_ple_proj, ple_post):
    weights = (ffn1_pre, ffn1_w_gu, ffn1_w_down, ffn1_post, mix_pre, w_in, w_short_conv, a_log,
               dt_bias, o_norm, w_dw_conv, b_dw_conv, ln_g, ln_b, w_branch_a, w_branch_b,
               w_out, mix_post, ffn2_pre, ffn2_w_gu, ffn2_w_down, ffn2_post, ple_pre,
               w_ple_gate, w_ple_proj, ple_post)
    nb, seq, _ = x_prompt.shape
    ns, ls, _ = x_sample.shape
    assert ls == SEG_S and seq % CHUNK == 0 and ns % (CHUNK // SEG_S) == 0
    depth = ffn1_pre.shape[0]
    tp = nb * seq
    xp, xs = x_prompt.reshape(tp, D_MODEL), x_sample.reshape(ns * ls, D_MODEL)
    outs = [[] for _ in range(6)]
    for i in range(depth):
        wi = tuple(wt[i:i + 1] if wt is w_in else wt[i] for wt in weights)
        xp, xs, s_p, q_p, g_p, s_s, q_s, g_s = _layer(
            xp, xs, p_prompt[i].reshape(tp, PLE_DIM), p_sample[i].reshape(ns * ls, PLE_DIM), nb, seq,
            state_delta[i:i + 1], state_qkv_conv[i:i + 1], state_glu_conv[i:i + 1], wi)
        for lst, v in zip(outs, (s_p, q_p, g_p, s_s, q_s, g_s)):
            lst.append(v)
    return (xp.reshape(nb, seq, D_MODEL), xs.reshape(ns, ls, D_MODEL)) + tuple(jnp.stack(lst) for lst in outs)
```

```python
import functools

import jax
import jax.numpy as jnp
from jax import lax
from jax.experimental import pallas as pl
from jax.experimental.pallas import tpu as pltpu

f32 = jnp.float32
bf16 = jnp.bfloat16

D_MODEL = 2048
N_HEADS = 16
HEAD_K = 128
HEAD_V = 128
KEY_DIM = N_HEADS * HEAD_K
VAL_DIM = N_HEADS * HEAD_V
QKV_DIM = 2 * KEY_DIM + VAL_DIM
SHORT_CONV = 4
CONV_CH = D_MODEL
CONV_WIDTH = 31
D_FF = 5632
PLE_DIM = 256
EPS = 1e-6

O_Z = QKV_DIM
O_BETA = O_Z + VAL_DIM
O_A = O_BETA + N_HEADS
O_GLU = O_A + N_HEADS
O_GATE = O_GLU + 2 * CONV_CH

CHUNK = 64
HG = 8
NHG = N_HEADS // HG
HW = HG * HEAD_K
SEG_S = 8
LOCAL_CHUNKS = 4

VMEM_LIMIT = 56 * 1024 * 1024


def _sigmoid(x):
    return 1.0 / (1.0 + jnp.exp(-x))


def _silu(x):
    return x * _sigmoid(x)


def _rms(x, g):
    return x * lax.rsqrt(jnp.mean(x * x, axis=-1, keepdims=True) + EPS) * g


def _dot(a, b):
    return jnp.dot(a, b, preferred_element_type=f32)


def _dot_nt(a, b):
    return lax.dot_general(a, b, (((1,), (1,)), ((), ())), preferred_element_type=f32)


def _params(*sem):
    return pltpu.CompilerParams(dimension_semantics=sem, vmem_limit_bytes=VMEM_LIMIT)


def _per_source(i, na, refs, fn):
    if len(refs) == 1:
        fn(refs[0])
    else:
        pl.when(i < na)(lambda: fn(refs[0]))
        pl.when(i >= na)(lambda: fn(refs[1]))


def _split_specs(block, na, nsrc, col=0):
    if nsrc == 1:
        return [pl.BlockSpec(block, lambda i, *_: (i, col))]
    return [pl.BlockSpec(block, lambda i, *_: (jnp.minimum(i, na - 1), col)),
            pl.BlockSpec(block, lambda i, *_: (jnp.maximum(i - na, 0), col))]


def _ffn_kernel(*refs, na, nsrc):
    x_refs = refs[:nsrc]
    gpre_ref, wg_ref, wu_ref, wd_ref, gpost_ref, gnext_ref, h_ref, hn_ref, xn_scr, acc_scr = refs[nsrc:]
    i, j = pl.program_id(0), pl.program_id(1)

    @pl.when(j == 0)
    def _():
        def norm_in(x_ref):
            xn_scr[...] = _rms(x_ref[...], gpre_ref[...]).astype(bf16)

        _per_source(i, na, x_refs, norm_in)
        acc_scr[...] = jnp.zeros_like(acc_scr)

    xn = xn_scr[...]
    a = _silu(_dot(xn, wg_ref[...])) * _dot(xn, wu_ref[...])
    acc_scr[...] += _dot(a.astype(bf16), wd_ref[...])

    @pl.when(j == pl.num_programs(1) - 1)
    def _():
        def residual_out(x_ref):
            h = x_ref[...] + 0.5 * _rms(acc_scr[...], gpost_ref[...])
            h_ref[...] = h
            hn_ref[...] = _rms(h, gnext_ref[...]).astype(bf16)

        _per_source(i, na, x_refs, residual_out)


def _ffn(xs, gpre, w_gu, w_down, gpost, gnext, tm=512, tf=512):
    na = xs[0].shape[0] // tm
    nt = sum(x.shape[0] for x in xs) // tm
    nf = D_FF // tf
    row = lambda i, j: (i, 0)
    vec = pl.BlockSpec((1, D_MODEL), lambda i, j: (0, 0))
    return pl.pallas_call(
        functools.partial(_ffn_kernel, na=na, nsrc=len(xs)),
        grid=(nt, nf),
        in_specs=_split_specs((tm, D_MODEL), na, len(xs))
        + [vec,
           pl.BlockSpec((D_MODEL, tf), lambda i, j: (0, j)),
           pl.BlockSpec((D_MODEL, tf), lambda i, j: (0, j + nf)),
           pl.BlockSpec((tf, D_MODEL), lambda i, j: (j, 0)), vec, vec],
        out_specs=[pl.BlockSpec((tm, D_MODEL), row), pl.BlockSpec((tm, D_MODEL), row)],
        out_shape=[jax.ShapeDtypeStruct((nt * tm, D_MODEL), f32), jax.ShapeDtypeStruct((nt * tm, D_MODEL), bf16)],
        scratch_shapes=[pltpu.VMEM((tm, D_MODEL), bf16), pltpu.VMEM((tm, D_MODEL), f32)],
        compiler_params=_params("parallel", "arbitrary"),
        name="ffn",
    )(*xs, gpre, w_gu, w_gu, w_down, gpost, gnext)


def _proj_kernel(x_ref, wt_ref, o_ref, *, act):
    y = _dot_nt(x_ref[...], wt_ref[...])
    if act == "sigmoid":
        y = _sigmoid(y)
    o_ref[...] = y.astype(o_ref.dtype)


def _proj(x, wt, n0, n, act, out_dtype, r0=0, rows=None, tm=1024, tn=1024):
    t, k = x.shape
    rows = t if rows is None else rows
    i0, j0 = r0 // tm, n0 // tn
    return pl.pallas_call(
        functools.partial(_proj_kernel, act=act),
        grid=(rows // tm, n // tn),
        in_specs=[pl.BlockSpec((tm, k), lambda i, j: (i + i0, 0)),
                  pl.BlockSpec((tn, k), lambda i, j: (j + j0, 0))],
        out_specs=pl.BlockSpec((tm, tn), lambda i, j: (i, j)),
        out_shape=jax.ShapeDtypeStruct((rows, n), out_dtype),
        compiler_params=_params("parallel", "arbitrary"),
        name="proj_" + act,
    )(x, wt)


def _glu_ba_kernel(x_ref, wa_ref, wb_ref, wba_ref, alog_ref, dtb_ref, o_ref, ba_ref):
    x = x_ref[...]
    o_ref[...] = _dot_nt(x, wa_ref[...]) * _sigmoid(_dot_nt(x, wb_ref[...]))

    @pl.when(pl.program_id(1) == 0)
    def _():
        y = _dot(x, wba_ref[...])
        lane = lax.broadcasted_iota(jnp.int32, y.shape, 1) % 128
        ya = y + dtb_ref[...]
        softplus = jnp.maximum(ya, 0.0) + jnp.log(1.0 + jnp.exp(-jnp.abs(ya)))
        ba_ref[...] = jnp.where(lane < HG, _sigmoid(y), -jnp.exp(alog_ref[...]) * softplus)


def _proj_glu_ba(x, wt, w_ba, alog_row, dtb_row, tm=1024, tn=1024):
    t, k = x.shape
    nj = CONV_CH // tn
    nba = NHG * 128
    vec = pl.BlockSpec((1, nba), lambda i, j: (0, 0))
    return pl.pallas_call(
        _glu_ba_kernel,
        grid=(t // tm, nj),
        in_specs=[pl.BlockSpec((tm, k), lambda i, j: (i, 0)),
                  pl.BlockSpec((tn, k), lambda i, j: (j, 0)),
                  pl.BlockSpec((tn, k), lambda i, j: (j + nj, 0)),
                  pl.BlockSpec((k, nba), lambda i, j: (0, 0)), vec, vec],
        out_specs=[pl.BlockSpec((tm, tn), lambda i, j: (i, j)), pl.BlockSpec((tm, nba), lambda i, j: (i, 0))],
        out_shape=[jax.ShapeDtypeStruct((t, CONV_CH), f32), jax.ShapeDtypeStruct((t, nba), f32)],
        compiler_params=_params("parallel", "arbitrary"),
        name="proj_glu_ba",
    )(x, wt, wt, w_ba, alog_row, dtb_row)


def _group_lanes(v):
    r = jnp.zeros((NHG, 128), f32).at[:, HG:2 * HG].set(v.astype(f32).reshape(NHG, HG))
    return r.reshape(1, NHG * 128)


def _ba_weight(w_ba):
    wb = w_ba[:, :N_HEADS].reshape(D_MODEL, NHG, HG)
    wa = w_ba[:, N_HEADS:].reshape(D_MODEL, NHG, HG)
    w = jnp.zeros((D_MODEL, NHG, 128), w_ba.dtype).at[:, :, :HG].set(wb).at[:, :, HG:2 * HG].set(wa)
    return w.reshape(D_MODEL, NHG * 128)


def _seq_masks(n, seg):
    r = lax.broadcasted_iota(jnp.int32, (n, n), 0)
    c = lax.broadcasted_iota(jnp.int32, (n, n), 1)
    same = (r // seg) == (c // seg)
    return same & (r >= c), same & (r > c), same


def _mask_dot(mask, x):
    l = jnp.where(mask, 1.0, 0.0).astype(bf16)
    hi = x.astype(bf16)
    r1 = x - hi.astype(f32)
    mid = r1.astype(bf16)
    lo = (r1 - mid.astype(f32)).astype(bf16)
    return _dot(l, hi) + _dot(l, mid) + _dot(l, lo)


def _mm(a, b):
    return _dot(a.astype(bf16), b.astype(bf16))


def _unit_lower_solve(ms, rhss, seg):
    n = range(len(ms))
    r = lax.broadcasted_iota(jnp.int32, (CHUNK, CHUNK), 0)
    c = lax.broadcasted_iota(jnp.int32, (CHUNK, CHUNK), 1)
    eye = (r == c).astype(f32)
    if seg <= 8:
        d, lo = ms, None
    else:
        blk = (r // 16) == (c // 16)
        d = [jnp.where(blk, m, 0.0) for m in ms]
        lo = [ms[i] - d[i] for i in n]
    d2 = [_mm(d[i], d[i]) for i in n]
    d3 = [_mm(d[i], d2[i]) for i in n]
    d4 = [_mm(d2[i], d2[i]) for i in n]
    a = [eye - d[i] + d2[i] - d3[i] for i in n]
    td = [a[i] + _mm(a[i], d4[i]) for i in n]
    if seg > 8:
        d8 = [_mm(d4[i], d4[i]) for i in n]
        td = [td[i] + _mm(td[i], d8[i]) for i in n]
    y = [_mm(td[i], rhss[i]) for i in n]
    if lo is None:
        return y
    e = [_mm(td[i], lo[i]) for i in n]
    e2 = [_mm(e[i], e[i]) for i in n]
    w = [y[i] - _mm(e[i], y[i]) for i in n]
    return [w[i] + _mm(e2[i], w[i]) for i in n]


def _chunk_local(qs, ks, vs, betas, gcols, grows, gtots, incl, strict, seg):
    n = range(len(qs))
    qn = [q * lax.rsqrt(jnp.sum(q * q, axis=-1, keepdims=True) + EPS) * (HEAD_K ** -0.5) for q in qs]
    kn = [k * lax.rsqrt(jnp.sum(k * k, axis=-1, keepdims=True) + EPS) for k in ks]
    decay = [jnp.exp(jnp.where(incl, gcols[i] - grows[i], -jnp.inf)) for i in n]
    kb = [k.astype(bf16) for k in kn]
    kk = [_dot_nt(kb[i], kb[i]) for i in n]
    qk = [_dot_nt(qn[i].astype(bf16), kb[i]) for i in n]
    m = [jnp.where(strict, betas[i] * decay[i] * kk[i], 0.0) for i in n]
    gamma = [jnp.exp(g) for g in gcols]
    rhs = [jnp.concatenate([betas[i] * vs[i], (betas[i] * gamma[i]) * kn[i]], axis=1) for i in n]
    sol = _unit_lower_solve(m, rhs, seg)
    uv = [s[:, :HEAD_V] for s in sol]
    wk = [s[:, HEAD_V:] for s in sol]
    qkd = [qk[i] * decay[i] for i in n]
    qg = [qn[i] * gamma[i] for i in n]
    kd = [kn[i] * jnp.exp(gtots[i] - gcols[i]) for i in n]
    return uv, wk, qkd, qg, kd


def _out_gate(o, onorm, z):
    return (_rms(o, onorm) * _silu(z)).astype(bf16)


def _gdn_prompt_kernel(q_ref, k_ref, v_ref, z_ref, ba_ref, wq_ref, wk_ref, wv_ref, onorm_ref,
                       oa_ref, s_ref,
                       eq, ek, ev, s_scr, qc_s, kc_s, vc_s, gc_s, gt_s, gl_s, uv_s, wq_s, kdt_s, qkd_s,
                       *, rows):
    t = pl.program_id(2)
    nc = rows // CHUNK
    heads = range(HG)
    hs = [slice(h * HEAD_K, (h + 1) * HEAD_K) for h in heads]
    gs = [slice(HG + h, HG + h + 1) for h in heads]

    @pl.when(t == 0)
    def _():
        s_scr[...] = jnp.zeros_like(s_scr)
        for e in (eq, ek, ev):
            e[:, 0:8, :] = jnp.zeros((HG, 8, HEAD_K), f32)

    off = 8 - (SHORT_CONV - 1)
    for e, x_ref, w_ref, dst in ((eq, q_ref, wq_ref, qc_s), (ek, k_ref, wk_ref, kc_s), (ev, v_ref, wv_ref, vc_s)):
        for h in heads:
            e[h, 8:, :] = x_ref[:, hs[h]]
            for r0 in range(0, rows, CHUNK):
                acc = w_ref[0:1, hs[h]] * e[h, r0 + off:r0 + off + CHUNK, :]
                for j in range(1, SHORT_CONV):
                    acc += w_ref[j:j + 1, hs[h]] * e[h, r0 + off + j:r0 + off + j + CHUNK, :]
                dst[r0:r0 + CHUNK, hs[h]] = _silu(acc)

    incl_b, _, same_b = _seq_masks(rows, CHUNK)
    ba = ba_ref[...]
    gc_s[...] = _mask_dot(incl_b, ba)
    gtot_b = _mask_dot(same_b, ba)
    gt_s[...] = gtot_b
    gl_s[...] = jnp.exp(gtot_b)
    incl, strict, _ = _seq_masks(CHUNK, CHUNK)

    def local(cp, carry):
        cs = [cp * LOCAL_CHUNKS + i for i in range(LOCAL_CHUNKS)]
        rss = [pl.ds(pl.multiple_of(c * CHUNK, CHUNK), CHUNK) for c in cs]
        units = [(i, h) for i in range(LOCAL_CHUNKS) for h in heads]
        bac = [ba_ref[rs, :] for rs in rss]
        gc = [gc_s[rs, :] for rs in rss]
        gtot = [gt_s[rs, :] for rs in rss]
        gct = [g.T for g in gc]
        uv, wk, qkd, qg, kd = _chunk_local(
            [qc_s[rss[i], hs[h]] for i, h in units], [kc_s[rss[i], hs[h]] for i, h in units],
            [vc_s[rss[i], hs[h]] for i, h in units], [bac[i][:, h:h + 1] for i, h in units],
            [gc[i][:, gs[h]] for i, h in units], [gct[i][gs[h], :] for i, h in units],
            [gtot[i][:, gs[h]] for i, h in units], incl, strict, CHUNK)
        for n, (i, h) in enumerate(units):
            uv_s[rss[i], hs[h]] = uv[n]
            wq_s[cs[i], 0:CHUNK, hs[h]] = wk[n].astype(bf16)
            wq_s[cs[i], CHUNK:2 * CHUNK, hs[h]] = qg[n].astype(bf16)
            kdt_s[cs[i], h] = kd[n].T.astype(bf16)
            qkd_s[cs[i], h] = qkd[n].astype(bf16)
        return carry

    lax.fori_loop(0, nc // LOCAL_CHUNKS, local, 0)

    for c in range(nc):
        rs = slice(c * CHUNK, (c + 1) * CHUNK)
        s = [s_scr[h] for h in heads]
        p = [_dot(wq_s[c, :, hs[h]], s[h].astype(bf16)) for h in heads]
        u = [(uv_s[rs, hs[h]] - p[h][:CHUNK]).astype(bf16) for h in heads]
        o = [p[h][CHUNK:] + _dot(qkd_s[c, h], u[h]) for h in heads]
        ds = [_dot(kdt_s[c, h], u[h]) for h in heads]
        for h in heads:
            s_scr[h] = gl_s[c * CHUNK:c * CHUNK + 1, gs[h]] * s[h] + ds[h]
            oa_ref[rs, hs[h]] = _out_gate(o[h], onorm_ref[...], z_ref[rs, hs[h]])

    for e in (eq, ek, ev):
        e[:, 0:8, :] = e[:, rows:rows + 8, :]

    @pl.when(t == pl.num_programs(2) - 1)
    def _():
        s_ref[0] = s_scr[...]


def _gdn_prompt(qkvz, ba, w_conv, onorm, nb, seq, rows=256):
    nt = seq // rows
    nc = rows // CHUNK
    rb = lambda b, hg, t: b * nt + t
    col = lambda off: pl.BlockSpec((rows, HW), lambda b, hg, t: (rb(b, hg, t), off * NHG + hg))
    wcol = lambda off: pl.BlockSpec((SHORT_CONV, HW), lambda b, hg, t: (0, off * NHG + hg))
    return pl.pallas_call(
        functools.partial(_gdn_prompt_kernel, rows=rows),
        grid=(nb, NHG, nt),
        in_specs=[col(0), col(1), col(2), col(3),
                  pl.BlockSpec((rows, 128), lambda b, hg, t: (rb(b, hg, t), hg)),
                  wcol(0), wcol(1), wcol(2),
                  pl.BlockSpec((1, HEAD_V), lambda b, hg, t: (0, 0))],
        out_specs=[pl.BlockSpec((rows, HW), lambda b, hg, t: (rb(b, hg, t), hg)),
                   pl.BlockSpec((1, HG, HEAD_K, HEAD_V), lambda b, hg, t: (b, hg, 0, 0))],
        out_shape=[jax.ShapeDtypeStruct((nb * seq, VAL_DIM), bf16),
                   jax.ShapeDtypeStruct((nb, N_HEADS, HEAD_K, HEAD_V), f32)],
        scratch_shapes=[pltpu.VMEM((HG, rows + 8, HEAD_K), f32)] * 3
        + [pltpu.VMEM((HG, HEAD_K, HEAD_V), f32)]
        + [pltpu.VMEM((rows, HW), f32)] * 3
        + [pltpu.VMEM((rows, 128), f32)] * 3
        + [pltpu.VMEM((rows, HW), f32),
           pltpu.VMEM((nc, 2 * CHUNK, HW), bf16),
           pltpu.VMEM((nc, HG, HEAD_K, CHUNK), bf16),
           pltpu.VMEM((nc, HG, CHUNK, CHUNK), bf16)],
        compiler_params=_params("parallel", "parallel", "arbitrary"),
        name="gdn_prompt",
    )(qkvz, qkvz, qkvz, qkvz, ba, w_conv, w_conv, w_conv, onorm)


def _gdn_sample_kernel(q_ref, k_ref, v_ref, qst_ref, kst_ref, vst_ref, z_ref, ba_ref, wq_ref, wk_ref, wv_ref,
                       onorm_ref, s0_ref, oa_ref, s_ref, e_scr):
    nseq = CHUNK // SEG_S
    heads = range(HG)
    seqs = range(nseq)
    hs = [slice(h * HEAD_K, (h + 1) * HEAD_K) for h in heads]
    gs = [slice(HG + h, HG + h + 1) for h in heads]
    js = [slice(j * SEG_S, (j + 1) * SEG_S) for j in seqs]
    incl, strict, same = _seq_masks(CHUNK, SEG_S)
    hist = SHORT_CONV - 1

    def conv(x_ref, st_ref, w_ref):
        e_scr[:, SEG_S - hist:SEG_S, :] = st_ref[...]
        e_scr[:, SEG_S:, :] = x_ref[...].reshape(nseq, SEG_S, HW)
        off = SEG_S - hist
        acc = w_ref[0:1, :] * e_scr[:, off:off + SEG_S, :].reshape(CHUNK, HW)
        for j in range(1, SHORT_CONV):
            acc += w_ref[j:j + 1, :] * e_scr[:, off + j:off + j + SEG_S, :].reshape(CHUNK, HW)
        return _silu(acc)

    qc, kc, vc = conv(q_ref, qst_ref, wq_ref), conv(k_ref, kst_ref, wk_ref), conv(v_ref, vst_ref, wv_ref)
    bac = ba_ref[...]
    gc = _mask_dot(incl, bac)
    gtot = _mask_dot(same, bac)
    gct = gc.T
    glv = jnp.exp(gtot)
    uv, wk, qkd, qg, kd = _chunk_local(
        [qc[:, hs[h]] for h in heads], [kc[:, hs[h]] for h in heads], [vc[:, hs[h]] for h in heads],
        [bac[:, h:h + 1] for h in heads], [gc[:, gs[h]] for h in heads], [gct[gs[h], :] for h in heads],
        [gtot[:, gs[h]] for h in heads], incl, strict, SEG_S)
    kdt = [kd[h].T.astype(bf16) for h in heads]
    p = [[_dot(jnp.concatenate([wk[h][js[j]], qg[h][js[j]]], axis=0).astype(bf16),
               s0_ref[j, h].astype(bf16)) for j in seqs] for h in heads]
    u = [[uv[h][js[j]] - p[h][j][:SEG_S] for j in seqs] for h in heads]
    zeros = jnp.zeros((SEG_S, HEAD_V), f32)
    for h in heads:
        for j in seqs:
            u_rows = jnp.concatenate([u[h][j] if i == j else zeros for i in seqs], axis=0).astype(bf16)
            s_ref[j, h] = glv[j * SEG_S:j * SEG_S + 1, gs[h]] * s0_ref[j, h] + _dot(kdt[h], u_rows)
    for h in heads:
        u_all = jnp.concatenate(u[h], axis=0).astype(bf16)
        o = jnp.concatenate([p[h][j][SEG_S:] for j in seqs], axis=0) + _dot(qkd[h].astype(bf16), u_all)
        oa_ref[:, hs[h]] = _out_gate(o, onorm_ref[...], z_ref[:, hs[h]])


def _gdn_sample(qkvz, qkv_state, ba, w_conv, onorm, s0, row0):
    nb = qkv_state.shape[1]
    nseq = CHUNK // SEG_S
    rb0 = row0 // CHUNK
    col = lambda off: pl.BlockSpec((CHUNK, HW), lambda g, hg: (rb0 + g, off * NHG + hg))
    st = lambda off: pl.BlockSpec((None, nseq, SHORT_CONV - 1, HW), lambda g, hg: (0, g, 0, off * NHG + hg))
    wcol = lambda off: pl.BlockSpec((SHORT_CONV, HW), lambda g, hg: (0, off * NHG + hg))
    sspec = pl.BlockSpec((nseq, HG, HEAD_K, HEAD_V), lambda g, hg: (g, hg, 0, 0))
    return pl.pallas_call(
        _gdn_sample_kernel,
        grid=(nb // nseq, NHG),
        in_specs=[col(0), col(1), col(2), st(0), st(1), st(2), col(3),
                  pl.BlockSpec((CHUNK, 128), lambda g, hg: (rb0 + g, hg)),
                  wcol(0), wcol(1), wcol(2),
                  pl.BlockSpec((1, HEAD_V), lambda g, hg: (0, 0)),
                  pl.BlockSpec((None, nseq, HG, HEAD_K, HEAD_V), lambda g, hg: (0, g, hg, 0, 0))],
        out_specs=[pl.BlockSpec((CHUNK, HW), lambda g, hg: (g, hg)), sspec],
        out_shape=[jax.ShapeDtypeStruct((nb * SEG_S, VAL_DIM), bf16), jax.ShapeDtypeStruct(s0.shape[1:], f32)],
        scratch_shapes=[pltpu.VMEM((nseq, 2 * SEG_S, HW), f32)],
        compiler_params=_params("parallel", "parallel"),
        name="gdn_sample",
    )(qkvz, qkvz, qkvz, qkv_state, qkv_state, qkv_state, qkvz, ba, w_conv, w_conv, w_conv, onorm, s0)


CONV_LC = 128
CONV_RB = 64


def _ln_silu(c, g, b):
    mu = jnp.mean(c, axis=-1, keepdims=True)
    xc = c - mu
    y = xc * lax.rsqrt(jnp.mean(xc * xc, axis=-1, keepdims=True) + EPS) * g + b
    return _silu(y).astype(bf16)


CONV_PAD = 32


def _gates_conv_kernel(u_ref, wg_ref, x_ref, cw_ref, cb_ref, g_ref, c_ref, ext, *, tiles_per_seq):
    j, i = pl.program_id(0), pl.program_id(1)
    tm, cw = x_ref.shape

    @pl.when((i == 0) & (j == 0))
    def _():
        ext[:, tm:, :] = jnp.zeros((cw // CONV_LC, CONV_PAD, CONV_LC), f32)

    first = (i % tiles_per_seq) == 0
    off = CONV_PAD - (CONV_WIDTH - 1)
    nlc = cw // CONV_LC
    gw = g_ref.shape[1] // nlc
    for lc in range(nlc):
        gs = slice(lc * gw, (lc + 1) * gw)
        g_ref[:, gs] = _sigmoid(_dot_nt(u_ref[...], wg_ref[gs, :])).astype(g_ref.dtype)
        ls = slice(lc * CONV_LC, (lc + 1) * CONV_LC)
        ext[lc, 0:CONV_PAD, :] = jnp.where(first, 0.0, ext[lc, tm:tm + CONV_PAD, :])
        ext[lc, CONV_PAD:, :] = x_ref[:, ls]
        for r0 in range(0, tm, CONV_RB):
            acc = jnp.broadcast_to(cb_ref[:, ls], (CONV_RB, CONV_LC))
            for t in range(CONV_WIDTH):
                acc += cw_ref[t:t + 1, ls] * ext[lc, r0 + off + t:r0 + off + t + CONV_RB, :]
            c_ref[r0:r0 + CONV_RB, ls] = acc


def _gates_conv_prompt(u, w_tail, glu, cw, cb, nb, seq, tm=1024, tn=1024):
    k = u.shape[1]
    ng = 2 * D_MODEL // tn
    cwid = CONV_CH // ng
    j0 = (O_GATE - O_GLU) // tn
    return pl.pallas_call(
        functools.partial(_gates_conv_kernel, tiles_per_seq=seq // tm),
        grid=(ng, nb * seq // tm),
        in_specs=[pl.BlockSpec((tm, k), lambda j, i: (i, 0)),
                  pl.BlockSpec((tn, k), lambda j, i: (j + j0, 0)),
                  pl.BlockSpec((tm, cwid), lambda j, i: (i, j)),
                  pl.BlockSpec((CONV_WIDTH, cwid), lambda j, i: (0, j)),
                  pl.BlockSpec((1, cwid), lambda j, i: (0, j))],
        out_specs=[pl.BlockSpec((tm, tn), lambda j, i: (i, j)), pl.BlockSpec((tm, cwid), lambda j, i: (i, j))],
        out_shape=[jax.ShapeDtypeStruct((nb * seq, 2 * D_MODEL), bf16),
                   jax.ShapeDtypeStruct((nb * seq, CONV_CH), f32)],
        scratch_shapes=[pltpu.VMEM((cwid // CONV_LC, tm + CONV_PAD, CONV_LC), f32)],
        compiler_params=_params("arbitrary", "arbitrary"),
        name="gates_conv",
    )(u, w_tail, glu, cw, cb)


def _cconv_sample_kernel(x_ref, st_ref, w_ref, b_ref, g_ref, beta_ref, o_ref, e_scr, cbuf):
    nseq = CHUNK // SEG_S
    hist = CONV_WIDTH - 1
    for lc in range(CONV_CH // CONV_LC):
        ls = slice(lc * CONV_LC, (lc + 1) * CONV_LC)
        e_scr[lc, :, 0:hist, :] = st_ref[:, :, ls]
        e_scr[lc, :, hist:hist + SEG_S, :] = x_ref[:, ls].reshape(nseq, SEG_S, CONV_LC)
        acc = jnp.broadcast_to(b_ref[:, ls], (CHUNK, CONV_LC))
        for j in range(CONV_WIDTH):
            acc += w_ref[j:j + 1, ls] * e_scr[lc, :, j:j + SEG_S, :].reshape(CHUNK, CONV_LC)
        cbuf[:, ls] = acc
    o_ref[...] = _ln_silu(cbuf[...], g_ref[...], beta_ref[...])


def _cconv_sample(glu, glu_state, w, b, g, beta, row0):
    _, nb, hist, _ = glu_state.shape
    nseq = CHUNK // SEG_S
    rb0 = row0 // CHUNK
    vec = pl.BlockSpec((1, CONV_CH), lambda i: (0, 0))
    return pl.pallas_call(
        _cconv_sample_kernel,
        grid=(nb // nseq,),
        in_specs=[pl.BlockSpec((CHUNK, CONV_CH), lambda i: (rb0 + i, 0)),
                  pl.BlockSpec((None, nseq, hist, CONV_CH), lambda i: (0, i, 0, 0)),
                  pl.BlockSpec((CONV_WIDTH, CONV_CH), lambda i: (0, 0)), vec, vec, vec],
        out_specs=pl.BlockSpec((CHUNK, CONV_CH), lambda i: (i, 0)),
        out_shape=jax.ShapeDtypeStruct((nb * SEG_S, CONV_CH), bf16),
        scratch_shapes=[pltpu.VMEM((CONV_CH // CONV_LC, nseq, hist + SEG_S, CONV_LC), f32),
                        pltpu.VMEM((CHUNK, CONV_CH), f32)],
        compiler_params=_params("parallel"),
        name="cconv_sample",
    )(glu, glu_state, w, b, g, beta)


def _merge_kernel(oap_ref, oas_ref, cp_ref, cs_ref, gap_ref, gas_ref, gbp_ref, gbs_ref, h_ref,
                  wa_ref, wb_ref, wo_ref, lng_ref, lnb_ref, gpost_ref, o_ref, *, na):
    def run(oa_ref, c, ga_ref, gb_ref):
        merged = (ga_ref[...].astype(f32) * _dot(oa_ref[...], wa_ref[...])
                  + gb_ref[...].astype(f32) * _dot(c, wb_ref[...]))
        o_ref[...] = h_ref[...] + _rms(_dot(merged.astype(bf16), wo_ref[...]), gpost_ref[...])

    i = pl.program_id(0)
    pl.when(i < na)(lambda: run(oap_ref, _ln_silu(cp_ref[...], lng_ref[...], lnb_ref[...]), gap_ref, gbp_ref))
    pl.when(i >= na)(lambda: run(oas_ref, cs_ref[...], gas_ref, gbs_ref))


def _merge(oa_p, oa_s, c_p, c_s, gates_p, gates_s, h, wa, wb, wo, ln_g, ln_b, gpost, tm=256):
    t = h.shape[0]
    na = oa_p.shape[0] // tm
    act = pl.BlockSpec((tm, D_MODEL), lambda i: (i, 0))
    vec = pl.BlockSpec((1, D_MODEL), lambda i: (0, 0))
    src = lambda col: _split_specs((tm, D_MODEL), na, 2, col)
    wsp = pl.BlockSpec((D_MODEL, D_MODEL), lambda i: (0, 0), pipeline_mode=pl.Buffered(1))
    return pl.pallas_call(
        functools.partial(_merge_kernel, na=na),
        grid=(t // tm,),
        in_specs=src(0) + src(0) + src(0) + src(1) + [act, wsp, wsp, wsp, vec, vec, vec],
        out_specs=act,
        out_shape=jax.ShapeDtypeStruct((t, D_MODEL), f32),
        compiler_params=_params("parallel"),
        name="merge",
    )(oa_p, oa_s, c_p, c_s, gates_p, gates_s, gates_p, gates_s, h, wa, wb, wo, ln_g, ln_b, gpost)


def _ple_kernel(hn_ref, pa_ref, pb_ref, h_ref, wg_ref, wp_ref, gpost_ref, oa_ref, ob_ref, *, na):
    i = pl.program_id(0)
    gate = _sigmoid(_dot(hn_ref[...], wg_ref[...]))

    def out(p_ref):
        v = gate * _dot(p_ref[...].astype(bf16), wp_ref[...])
        return h_ref[...] + _rms(v, gpost_ref[...])

    @pl.when(i < na)
    def _():
        oa_ref[...] = out(pa_ref)

    @pl.when(i >= na)
    def _():
        ob_ref[...] = out(pb_ref)


def _ple(hn, pa, pb, h, wg, wp, gpost, tm=512):
    na, nb = pa.shape[0] // tm, pb.shape[0] // tm
    row = lambda i: (i, 0)
    first = lambda i: (jnp.minimum(i, na - 1), 0)
    second = lambda i: (jnp.maximum(i - na, 0), 0)
    act = pl.BlockSpec((tm, D_MODEL), row)
    return pl.pallas_call(
        functools.partial(_ple_kernel, na=na),
        grid=(na + nb,),
        in_specs=[act, pl.BlockSpec((tm, PLE_DIM), first), pl.BlockSpec((tm, PLE_DIM), second), act,
                  pl.BlockSpec((D_MODEL, D_MODEL), lambda i: (0, 0), pipeline_mode=pl.Buffered(1)),
                  pl.BlockSpec((PLE_DIM, D_MODEL), lambda i: (0, 0)),
                  pl.BlockSpec((1, D_MODEL), lambda i: (0, 0))],
        out_specs=[pl.BlockSpec((tm, D_MODEL), first), pl.BlockSpec((tm, D_MODEL), second)],
        out_shape=[jax.ShapeDtypeStruct((na * tm, D_MODEL), f32), jax.ShapeDtypeStruct((nb * tm, D_MODEL), f32)],
        compiler_params=_params("arbitrary"),
        name="ple",
    )(hn, pa, pb, h, wg, wp, gpost)


def _layer(xp, xs, pp, ps, nb, seq, s0, qkv_buf, glu_buf, w):
    (ffn1_pre, ffn1_w_gu, ffn1_w_down, ffn1_post, mix_pre, w_in, w_short_conv, a_log, dt_bias,
     o_norm, w_dw_conv, b_dw_conv, ln_g, ln_b, w_branch_a, w_branch_b, w_out, mix_post,
     ffn2_pre, ffn2_w_gu, ffn2_w_down, ffn2_post, ple_pre, w_ple_gate, w_ple_proj,
     ple_post) = w
    tp = nb * seq
    ns = s0.shape[1]
    row = lambda v: v.astype(f32).reshape(1, -1)
    cast = lambda v: v.astype(bf16)

    h1, u = _ffn([xp, xs], row(ffn1_pre), cast(ffn1_w_gu), cast(ffn1_w_down), row(ffn1_post), row(mix_pre))

    cast_t = lambda lo, hi: jnp.swapaxes(w_in[0, :, lo:hi], 0, 1).astype(bf16)
    w_qkvz, w_mid, w_tail = cast_t(0, O_BETA), cast_t(O_BETA, O_GLU), cast_t(O_GLU, w_in.shape[2])
    qkvz = _proj(u, w_qkvz, 0, O_BETA, "none", f32)
    glu, ba = _proj_glu_ba(u, w_tail, _ba_weight(w_mid.T), _group_lanes(a_log), _group_lanes(dt_bias))

    w_conv = w_short_conv.astype(f32)
    onorm = row(o_norm)
    oa_p, s_p = _gdn_prompt(qkvz, ba, w_conv, onorm, nb, seq)
    oa_s, s_s = _gdn_sample(qkvz, qkv_buf.astype(f32), ba, w_conv, onorm, s0.astype(f32), tp)

    cw, cb, lg, lb = w_dw_conv.astype(f32), row(b_dw_conv), row(ln_g), row(ln_b)
    gates_p, c_p = _gates_conv_prompt(u, w_tail, glu, cw, cb, nb, seq)
    gates_s = _proj(u, w_tail, O_GATE - O_GLU, 2 * D_MODEL, "sigmoid", bf16, r0=tp, rows=xs.shape[0])
    c_s = _cconv_sample(glu, glu_buf.astype(f32), cw, cb, lg, lb, tp)

    h2 = _merge(oa_p, oa_s, c_p, c_s, gates_p, gates_s, h1, cast(w_branch_a), cast(w_branch_b), cast(w_out),
                lg, lb, row(mix_post))
    h3, hn = _ffn([h2], row(ffn2_pre), cast(ffn2_w_gu), cast(ffn2_w_down), row(ffn2_post), row(ple_pre))
    yp, ys = _ple(hn, pp, ps, h3, cast(w_ple_gate), cast(w_ple_proj), row(ple_post))

    tail = lambda a, width, n: jnp.stack([a[(b + 1) * seq - n:(b + 1) * seq, :width] for b in range(nb)])
    qkv_p = tail(qkvz, QKV_DIM, SHORT_CONV - 1)
    glu_p = tail(glu, CONV_CH, CONV_WIDTH - 1)
    qkv_s = qkvz[tp:, :QKV_DIM].reshape(ns, SEG_S, QKV_DIM)[:, SEG_S - (SHORT_CONV - 1):]
    glu_s = jnp.concatenate([glu_buf[0].astype(f32)[:, SEG_S:], glu[tp:].reshape(ns, SEG_S, CONV_CH)], axis=1)
    return (yp, ys, s_p, qkv_p, glu_p, s_s, qkv_s, glu_s)


def kernel(x_prompt, x_sample, p_prompt, p_sample, state_delta, state_qkv_conv, state_glu_conv, ffn1_pre, ffn1_w_gu, ffn1_w_down, ffn1_post, mix_pre, w_in, w_short_conv, a_log, dt_bias, o_norm, w_dw_conv, b_dw_conv, ln_g, ln_b, w_branch_a, w_branch_b, w_out, mix_post, ffn2_pre, ffn2_w_gu, ffn2_w_down, ffn2_post, ple_pre, w_ple_gate, w_ple_proj, ple_post):
    weights = (ffn1_pre, ffn1_w_gu, ffn1_w_down, ffn1_post, mix_pre, w_in, w_short_conv, a_log,
               dt_bias, o_norm, w_dw_conv, b_dw_conv, ln_g, ln_b, w_branch_a, w_branch_b,
               w_out, mix_post, ffn2_pre, ffn2_w_gu, ffn2_w_down, ffn2_post, ple_pre,
               w_ple_gate, w_ple_proj, ple_post)
    nb, seq, _ = x_prompt.shape
    ns, ls, _ = x_sample.shape
    assert ls == SEG_S and seq % CHUNK == 0 and ns % (CHUNK // SEG_S) == 0
    depth = ffn1_pre.shape[0]
    tp = nb * seq
    xp, xs = x_prompt.reshape(tp, D_MODEL), x_sample.reshape(ns * ls, D_MODEL)
    outs = [[] for _ in range(6)]
    for i in range(depth):
        wi = tuple(wt[i:i + 1] if wt is w_in else wt[i] for wt in weights)
        xp, xs, s_p, q_p, g_p, s_s, q_s, g_s = _layer(
            xp, xs, p_prompt[i].reshape(tp, PLE_DIM), p_sample[i].reshape(ns * ls, PLE_DIM), nb, seq,
            state_delta[i:i + 1], state_qkv_conv[i:i + 1], state_glu_conv[i:i + 1], wi)
        for lst, v in zip(outs, (s_p, q_p, g_p, s_s, q_s, g_s)):
            lst.append(v)
    return (xp.reshape(nb, seq, D_MODEL), xs.reshape(ns, ls, D_MODEL)) + tuple(jnp.stack(lst) for lst in outs)
```

```python
import functools

import jax
import jax.numpy as jnp
from jax import lax
from jax.experimental import pallas as pl
from jax.experimental.pallas import tpu as pltpu

f32 = jnp.float32
bf16 = jnp.bfloat16

D_MODEL = 2048
N_HEADS = 16
HEAD_K = 128
HEAD_V = 128
KEY_DIM = N_HEADS * HEAD_K
VAL_DIM = N_HEADS * HEAD_V
QKV_DIM = 2 * KEY_DIM + VAL_DIM
SHORT_CONV = 4
CONV_CH = D_MODEL
CONV_WIDTH = 31
D_FF = 5632
PLE_DIM = 256
EPS = 1e-6

O_Z = QKV_DIM
O_BETA = O_Z + VAL_DIM
O_A = O_BETA + N_HEADS
O_GLU = O_A + N_HEADS
O_GATE = O_GLU + 2 * CONV_CH

CHUNK = 64
HG = 8
NHG = N_HEADS // HG
HW = HG * HEAD_K
SEG_S = 8
LOCAL_CHUNKS = 4

VMEM_LIMIT = 56 * 1024 * 1024


def _sigmoid(x):
    return 1.0 / (1.0 + jnp.exp(-x))


def _silu(x):
    return x * _sigmoid(x)


def _rms(x, g):
    return x * lax.rsqrt(jnp.mean(x * x, axis=-1, keepdims=True) + EPS) * g


def _dot(a, b):
    return jnp.dot(a, b, preferred_element_type=f32)


def _dot_nt(a, b):
    return lax.dot_general(a, b, (((1,), (1,)), ((), ())), preferred_element_type=f32)


def _params(*sem):
    return pltpu.CompilerParams(dimension_semantics=sem, vmem_limit_bytes=VMEM_LIMIT)


def _per_source(i, na, refs, fn):
    if len(refs) == 1:
        fn(refs[0])
    else:
        pl.when(i < na)(lambda: fn(refs[0]))
        pl.when(i >= na)(lambda: fn(refs[1]))


def _split_specs(block, na, nsrc, col=0):
    if nsrc == 1:
        return [pl.BlockSpec(block, lambda i, *_: (i, col))]
    return [pl.BlockSpec(block, lambda i, *_: (jnp.minimum(i, na - 1), col)),
            pl.BlockSpec(block, lambda i, *_: (jnp.maximum(i - na, 0), col))]


def _ffn_kernel(*refs, na, nsrc):
    x_refs = refs[:nsrc]
    gpre_ref, wg_ref, wu_ref, wd_ref, gpost_ref, gnext_ref, h_ref, hn_ref, xn_scr, acc_scr = refs[nsrc:]
    i, j = pl.program_id(0), pl.program_id(1)

    @pl.when(j == 0)
    def _():
        def norm_in(x_ref):
            xn_scr[...] = _rms(x_ref[...], gpre_ref[...]).astype(bf16)

        _per_source(i, na, x_refs, norm_in)
        acc_scr[...] = jnp.zeros_like(acc_scr)

    xn = xn_scr[...]
    a = _silu(_dot(xn, wg_ref[...])) * _dot(xn, wu_ref[...])
    acc_scr[...] += _dot(a.astype(bf16), wd_ref[...])

    @pl.when(j == pl.num_programs(1) - 1)
    def _():
        def residual_out(x_ref):
            h = x_ref[...] + 0.5 * _rms(acc_scr[...], gpost_ref[...])
            h_ref[...] = h
            hn_ref[...] = _rms(h, gnext_ref[...]).astype(bf16)

        _per_source(i, na, x_refs, residual_out)


def _ffn(xs, gpre, w_gu, w_down, gpost, gnext, tm=512, tf=512):
    na = xs[0].shape[0] // tm
    nt = sum(x.shape[0] for x in xs) // tm
    nf = D_FF // tf
    row = lambda i, j: (i, 0)
    vec = pl.BlockSpec((1, D_MODEL), lambda i, j: (0, 0))
    return pl.pallas_call(
        functools.partial(_ffn_kernel, na=na, nsrc=len(xs)),
        grid=(nt, nf),
        in_specs=_split_specs((tm, D_MODEL), na, len(xs))
        + [vec,
           pl.BlockSpec((D_MODEL, tf), lambda i, j: (0, j)),
           pl.BlockSpec((D_MODEL, tf), lambda i, j: (0, j + nf)),
           pl.BlockSpec((tf, D_MODEL), lambda i, j: (j, 0)), vec, vec],
        out_specs=[pl.BlockSpec((tm, D_MODEL), row), pl.BlockSpec((tm, D_MODEL), row)],
        out_shape=[jax.ShapeDtypeStruct((nt * tm, D_MODEL), f32), jax.ShapeDtypeStruct((nt * tm, D_MODEL), bf16)],
        scratch_shapes=[pltpu.VMEM((tm, D_MODEL), bf16), pltpu.VMEM((tm, D_MODEL), f32)],
        compiler_params=_params("parallel", "arbitrary"),
        name="ffn",
    )(*xs, gpre, w_gu, w_gu, w_down, gpost, gnext)


def _proj_kernel(x_ref, wt_ref, o_ref, *, act):
    y = _dot_nt(x_ref[...], wt_ref[...])
    if act == "sigmoid":
        y = _sigmoid(y)
    o_ref[...] = y.astype(o_ref.dtype)


def _proj(x, wt, n0, n, act, out_dtype, r0=0, rows=None, tm=1024, tn=1024):
    t, k = x.shape
    rows = t if rows is None else rows
    i0, j0 = r0 // tm, n0 // tn
    return pl.pallas_call(
        functools.partial(_proj_kernel, act=act),
        grid=(rows // tm, n // tn),
        in_specs=[pl.BlockSpec((tm, k), lambda i, j: (i + i0, 0)),
                  pl.BlockSpec((tn, k), lambda i, j: (j + j0, 0))],
        out_specs=pl.BlockSpec((tm, tn), lambda i, j: (i, j)),
        out_shape=jax.ShapeDtypeStruct((rows, n), out_dtype),
        compiler_params=_params("parallel", "arbitrary"),
        name="proj_" + act,
    )(x, wt)


def _glu_ba_kernel(x_ref, wa_ref, wb_ref, wba_ref, alog_ref, dtb_ref, o_ref, ba_ref):
    x = x_ref[...]
    o_ref[...] = _dot_nt(x, wa_ref[...]) * _sigmoid(_dot_nt(x, wb_ref[...]))

    @pl.when(pl.program_id(1) == 0)
    def _():
        y = _dot(x, wba_ref[...])
        lane = lax.broadcasted_iota(jnp.int32, y.shape, 1) % 128
        ya = y + dtb_ref[...]
        softplus = jnp.maximum(ya, 0.0) + jnp.log(1.0 + jnp.exp(-jnp.abs(ya)))
        ba_ref[...] = jnp.where(lane < HG, _sigmoid(y), -jnp.exp(alog_ref[...]) * softplus)


def _proj_glu_ba(x, wt, w_ba, alog_row, dtb_row, tm=1024, tn=1024):
    t, k = x.shape
    nj = CONV_CH // tn
    nba = NHG * 128
    vec = pl.BlockSpec((1, nba), lambda i, j: (0, 0))
    return pl.pallas_call(
        _glu_ba_kernel,
        grid=(t // tm, nj),
        in_specs=[pl.BlockSpec((tm, k), lambda i, j: (i, 0)),
                  pl.BlockSpec((tn, k), lambda i, j: (j, 0)),
                  pl.BlockSpec((tn, k), lambda i, j: (j + nj, 0)),
                  pl.BlockSpec((k, nba), lambda i, j: (0, 0)), vec, vec],
        out_specs=[pl.BlockSpec((tm, tn), lambda i, j: (i, j)), pl.BlockSpec((tm, nba), lambda i, j: (i, 0))],
        out_shape=[jax.ShapeDtypeStruct((t, CONV_CH), f32), jax.ShapeDtypeStruct((t, nba), f32)],
        compiler_params=_params("parallel", "arbitrary"),
        name="proj_glu_ba",
    )(x, wt, wt, w_ba, alog_row, dtb_row)


def _group_lanes(v):
    r = jnp.zeros((NHG, 128), f32).at[:, HG:2 * HG].set(v.astype(f32).reshape(NHG, HG))
    return r.reshape(1, NHG * 128)


def _ba_weight(w_ba):
    wb = w_ba[:, :N_HEADS].reshape(D_MODEL, NHG, HG)
    wa = w_ba[:, N_HEADS:].reshape(D_MODEL, NHG, HG)
    w = jnp.zeros((D_MODEL, NHG, 128), w_ba.dtype).at[:, :, :HG].set(wb).at[:, :, HG:2 * HG].set(wa)
    return w.reshape(D_MODEL, NHG * 128)


def _seq_masks(n, seg):
    r = lax.broadcasted_iota(jnp.int32, (n, n), 0)
    c = lax.broadcasted_iota(jnp.int32, (n, n), 1)
    same = (r // seg) == (c // seg)
    return same & (r >= c), same & (r > c), same


def _mask_dot(mask, x):
    l = jnp.where(mask, 1.0, 0.0).astype(bf16)
    hi = x.astype(bf16)
    r1 = x - hi.astype(f32)
    mid = r1.astype(bf16)
    lo = (r1 - mid.astype(f32)).astype(bf16)
    return _dot(l, hi) + _dot(l, mid) + _dot(l, lo)


def _mm(a, b):
    return _dot(a.astype(bf16), b.astype(bf16))


def _unit_lower_solve(ms, rhss, seg):
    n = range(len(ms))
    r = lax.broadcasted_iota(jnp.int32, (CHUNK, CHUNK), 0)
    c = lax.broadcasted_iota(jnp.int32, (CHUNK, CHUNK), 1)
    eye = (r == c).astype(f32)
    if seg <= 8:
        d, lo = ms, None
    else:
        blk = (r // 16) == (c // 16)
        d = [jnp.where(blk, m, 0.0) for m in ms]
        lo = [ms[i] - d[i] for i in n]
    d2 = [_mm(d[i], d[i]) for i in n]
    d3 = [_mm(d[i], d2[i]) for i in n]
    d4 = [_mm(d2[i], d2[i]) for i in n]
    a = [eye - d[i] + d2[i] - d3[i] for i in n]
    td = [a[i] + _mm(a[i], d4[i]) for i in n]
    if seg > 8:
        d8 = [_mm(d4[i], d4[i]) for i in n]
        td = [td[i] + _mm(td[i], d8[i]) for i in n]
    y = [_mm(td[i], rhss[i]) for i in n]
    if lo is None:
        return y
    e = [_mm(td[i], lo[i]) for i in n]
    e2 = [_mm(e[i], e[i]) for i in n]
    w = [y[i] - _mm(e[i], y[i]) for i in n]
    return [w[i] + _mm(e2[i], w[i]) for i in n]


def _chunk_local(qs, ks, vs, betas, gcols, grows, gtots, incl, strict, seg):
    n = range(len(qs))
    qn = [q * lax.rsqrt(jnp.sum(q * q, axis=-1, keepdims=True) + EPS) * (HEAD_K ** -0.5) for q in qs]
    kn = [k * lax.rsqrt(jnp.sum(k * k, axis=-1, keepdims=True) + EPS) for k in ks]
    decay = [jnp.exp(jnp.where(incl, gcols[i] - grows[i], -jnp.inf)) for i in n]
    kb = [k.astype(bf16) for k in kn]
    kk = [_dot_nt(kb[i], kb[i]) for i in n]
    qk = [_dot_nt(qn[i].astype(bf16), kb[i]) for i in n]
    m = [jnp.where(strict, betas[i] * decay[i] * kk[i], 0.0) for i in n]
    gamma = [jnp.exp(g) for g in gcols]
    rhs = [jnp.concatenate([betas[i] * vs[i], (betas[i] * gamma[i]) * kn[i]], axis=1) for i in n]
    sol = _unit_lower_solve(m, rhs, seg)
    uv = [s[:, :HEAD_V] for s in sol]
    wk = [s[:, HEAD_V:] for s in sol]
    qkd = [qk[i] * decay[i] for i in n]
    qg = [qn[i] * gamma[i] for i in n]
    kd = [kn[i] * jnp.exp(gtots[i] - gcols[i]) for i in n]
    return uv, wk, qkd, qg, kd


def _out_gate(o, onorm, z):
    return (_rms(o, onorm) * _silu(z)).astype(bf16)


def _gdn_prompt_kernel(q_ref, k_ref, v_ref, z_ref, ba_ref, wq_ref, wk_ref, wv_ref, onorm_ref,
                       oa_ref, s_ref,
                       eq, ek, ev, s_scr, qc_s, kc_s, vc_s, gc_s, gt_s, gl_s, uv_s, wq_s, kdt_s, qkd_s,
                       *, rows):
    t = pl.program_id(2)
    nc = rows // CHUNK
    heads = range(HG)
    hs = [slice(h * HEAD_K, (h + 1) * HEAD_K) for h in heads]
    gs = [slice(HG + h, HG + h + 1) for h in heads]

    @pl.when(t == 0)
    def _():
        s_scr[...] = jnp.zeros_like(s_scr)
        for e in (eq, ek, ev):
            e[:, 0:8, :] = jnp.zeros((HG, 8, HEAD_K), f32)

    off = 8 - (SHORT_CONV - 1)
    for e, x_ref, w_ref, dst in ((eq, q_ref, wq_ref, qc_s), (ek, k_ref, wk_ref, kc_s), (ev, v_ref, wv_ref, vc_s)):
        for h in heads:
            e[h, 8:, :] = x_ref[:, hs[h]]
            for r0 in range(0, rows, CHUNK):
                acc = w_ref[0:1, hs[h]] * e[h, r0 + off:r0 + off + CHUNK, :]
                for j in range(1, SHORT_CONV):
                    acc += w_ref[j:j + 1, hs[h]] * e[h, r0 + off + j:r0 + off + j + CHUNK, :]
                dst[r0:r0 + CHUNK, hs[h]] = _silu(acc)

    incl_b, _, same_b = _seq_masks(rows, CHUNK)
    ba = ba_ref[...]
    gc_s[...] = _mask_dot(incl_b, ba)
    gtot_b = _mask_dot(same_b, ba)
    gt_s[...] = gtot_b
    gl_s[...] = jnp.exp(gtot_b)
    incl, strict, _ = _seq_masks(CHUNK, CHUNK)

    def local(cp, carry):
        cs = [cp * LOCAL_CHUNKS + i for i in range(LOCAL_CHUNKS)]
        rss = [pl.ds(pl.multiple_of(c * CHUNK, CHUNK), CHUNK) for c in cs]
        units = [(i, h) for i in range(LOCAL_CHUNKS) for h in heads]
        bac = [ba_ref[rs, :] for rs in rss]
        gc = [gc_s[rs, :] for rs in rss]
        gtot = [gt_s[rs, :] for rs in rss]
        gct = [g.T for g in gc]
        uv, wk, qkd, qg, kd = _chunk_local(
            [qc_s[rss[i], hs[h]] for i, h in units], [kc_s[rss[i], hs[h]] for i, h in units],
            [vc_s[rss[i], hs[h]] for i, h in units], [bac[i][:, h:h + 1] for i, h in units],
            [gc[i][:, gs[h]] for i, h in units], [gct[i][gs[h], :] for i, h in units],
            [gtot[i][:, gs[h]] for i, h in units], incl, strict, CHUNK)
        for n, (i, h) in enumerate(units):
            uv_s[rss[i], hs[h]] = uv[n]
            wq_s[cs[i], 0:CHUNK, hs[h]] = wk[n].astype(bf16)
            wq_s[cs[i], CHUNK:2 * CHUNK, hs[h]] = qg[n].astype(bf16)
            kdt_s[cs[i], h] = kd[n].T.astype(bf16)
            qkd_s[cs[i], h] = qkd[n].astype(bf16)
        return carry

    lax.fori_loop(0, nc // LOCAL_CHUNKS, local, 0)

    for c in range(nc):
        rs = slice(c * CHUNK, (c + 1) * CHUNK)
        s = [s_scr[h] for h in heads]
        p = [_dot(wq_s[c, :, hs[h]], s[h].astype(bf16)) for h in heads]
        u = [(uv_s[rs, hs[h]] - p[h][:CHUNK]).astype(bf16) for h in heads]
        o = [p[h][CHUNK:] + _dot(qkd_s[c, h], u[h]) for h in heads]
        ds = [_dot(kdt_s[c, h], u[h]) for h in heads]
        for h in heads:
            s_scr[h] = gl_s[c * CHUNK:c * CHUNK + 1, gs[h]] * s[h] + ds[h]
            oa_ref[rs, hs[h]] = _out_gate(o[h], onorm_ref[...], z_ref[rs, hs[h]])

    for e in (eq, ek, ev):
        e[:, 0:8, :] = e[:, rows:rows + 8, :]

    @pl.when(t == pl.num_programs(2) - 1)
    def _():
        s_ref[0] = s_scr[...]


def _gdn_prompt(qkvz, ba, w_conv, onorm, nb, seq, rows=256):
    nt = seq // rows
    nc = rows // CHUNK
    rb = lambda b, hg, t: b * nt + t
    col = lambda off: pl.BlockSpec((rows, HW), lambda b, hg, t: (rb(b, hg, t), off * NHG + hg))
    wcol = lambda off: pl.BlockSpec((SHORT_CONV, HW), lambda b, hg, t: (0, off * NHG + hg))
    return pl.pallas_call(
        functools.partial(_gdn_prompt_kernel, rows=rows),
        grid=(nb, NHG, nt),
        in_specs=[col(0), col(1), col(2), col(3),
                  pl.BlockSpec((rows, 128), lambda b, hg, t: (rb(b, hg, t), hg)),
                  wcol(0), wcol(1), wcol(2),
                  pl.BlockSpec((1, HEAD_V), lambda b, hg, t: (0, 0))],
        out_specs=[pl.BlockSpec((rows, HW), lambda b, hg, t: (rb(b, hg, t), hg)),
                   pl.BlockSpec((1, HG, HEAD_K, HEAD_V), lambda b, hg, t: (b, hg, 0, 0))],
        out_shape=[jax.ShapeDtypeStruct((nb * seq, VAL_DIM), bf16),
                   jax.ShapeDtypeStruct((nb, N_HEADS, HEAD_K, HEAD_V), f32)],
        scratch_shapes=[pltpu.VMEM((HG, rows + 8, HEAD_K), f32)] * 3
        + [pltpu.VMEM((HG, HEAD_K, HEAD_V), f32)]
        + [pltpu.VMEM((rows, HW), f32)] * 3
        + [pltpu.VMEM((rows, 128), f32)] * 3
        + [pltpu.VMEM((rows, HW), f32),
           pltpu.VMEM((nc, 2 * CHUNK, HW), bf16),
           pltpu.VMEM((nc, HG, HEAD_K, CHUNK), bf16),
           pltpu.VMEM((nc, HG, CHUNK, CHUNK), bf16)],
        compiler_params=_params("parallel", "parallel", "arbitrary"),
        name="gdn_prompt",
    )(qkvz, qkvz, qkvz, qkvz, ba, w_conv, w_conv, w_conv, onorm)


def _gdn_sample_kernel(q_ref, k_ref, v_ref, qst_ref, kst_ref, vst_ref, z_ref, ba_ref, wq_ref, wk_ref, wv_ref,
                       onorm_ref, s0_ref, oa_ref, s_ref, e_scr):
    nseq = CHUNK // SEG_S
    heads = range(HG)
    seqs = range(nseq)
    hs = [slice(h * HEAD_K, (h + 1) * HEAD_K) for h in heads]
    gs = [slice(HG + h, HG + h + 1) for h in heads]
    js = [slice(j * SEG_S, (j + 1) * SEG_S) for j in seqs]
    incl, strict, same = _seq_masks(CHUNK, SEG_S)
    hist = SHORT_CONV - 1

    def conv(x_ref, st_ref, w_ref):
        e_scr[:, SEG_S - hist:SEG_S, :] = st_ref[...]
        e_scr[:, SEG_S:, :] = x_ref[...].reshape(nseq, SEG_S, HW)
        off = SEG_S - hist
        acc = w_ref[0:1, :] * e_scr[:, off:off + SEG_S, :].reshape(CHUNK, HW)
        for j in range(1, SHORT_CONV):
            acc += w_ref[j:j + 1, :] * e_scr[:, off + j:off + j + SEG_S, :].reshape(CHUNK, HW)
        return _silu(acc)

    qc, kc, vc = conv(q_ref, qst_ref, wq_ref), conv(k_ref, kst_ref, wk_ref), conv(v_ref, vst_ref, wv_ref)
    bac = ba_ref[...]
    gc = _mask_dot(incl, bac)
    gtot = _mask_dot(same, bac)
    gct = gc.T
    glv = jnp.exp(gtot)
    uv, wk, qkd, qg, kd = _chunk_local(
        [qc[:, hs[h]] for h in heads], [kc[:, hs[h]] for h in heads], [vc[:, hs[h]] for h in heads],
        [bac[:, h:h + 1] for h in heads], [gc[:, gs[h]] for h in heads], [gct[gs[h], :] for h in heads],
        [gtot[:, gs[h]] for h in heads], incl, strict, SEG_S)
    kdt = [kd[h].T.astype(bf16) for h in heads]
    p = [[_dot(jnp.concatenate([wk[h][js[j]], qg[h][js[j]]], axis=0).astype(bf16),
               s0_ref[j, h].astype(bf16)) for j in seqs] for h in heads]
    u = [[uv[h][js[j]] - p[h][j][:SEG_S] for j in seqs] for h in heads]
    zeros = jnp.zeros((SEG_S, HEAD_V), f32)
    for h in heads:
        for j in seqs:
            u_rows = jnp.concatenate([u[h][j] if i == j else zeros for i in seqs], axis=0).astype(bf16)
            s_ref[j, h] = glv[j * SEG_S:j * SEG_S + 1, gs[h]] * s0_ref[j, h] + _dot(kdt[h], u_rows)
    for h in heads:
        u_all = jnp.concatenate(u[h], axis=0).astype(bf16)
        o = jnp.concatenate([p[h][j][SEG_S:] for j in seqs], axis=0) + _dot(qkd[h].astype(bf16), u_all)
        oa_ref[:, hs[h]] = _out_gate(o, onorm_ref[...], z_ref[:, hs[h]])


def _gdn_sample(qkvz, qkv_state, ba, w_conv, onorm, s0, row0):
    nb = qkv_state.shape[1]
    nseq = CHUNK // SEG_S
    rb0 = row0 // CHUNK
    col = lambda off: pl.BlockSpec((CHUNK, HW), lambda g, hg: (rb0 + g, off * NHG + hg))
    st = lambda off: pl.BlockSpec((None, nseq, SHORT_CONV - 1, HW), lambda g, hg: (0, g, 0, off * NHG + hg))
    wcol = lambda off: pl.BlockSpec((SHORT_CONV, HW), lambda g, hg: (0, off * NHG + hg))
    sspec = pl.BlockSpec((nseq, HG, HEAD_K, HEAD_V), lambda g, hg: (g, hg, 0, 0))
    return pl.pallas_call(
        _gdn_sample_kernel,
        grid=(nb // nseq, NHG),
        in_specs=[col(0), col(1), col(2), st(0), st(1), st(2), col(3),
                  pl.BlockSpec((CHUNK, 128), lambda g, hg: (rb0 + g, hg)),
                  wcol(0), wcol(1), wcol(2),
                  pl.BlockSpec((1, HEAD_V), lambda g, hg: (0, 0)),
                  pl.BlockSpec((None, nseq, HG, HEAD_K, HEAD_V), lambda g, hg: (0, g, hg, 0, 0))],
        out_specs=[pl.BlockSpec((CHUNK, HW), lambda g, hg: (g, hg)), sspec],
        out_shape=[jax.ShapeDtypeStruct((nb * SEG_S, VAL_DIM), bf16), jax.ShapeDtypeStruct(s0.shape[1:], f32)],
        scratch_shapes=[pltpu.VMEM((nseq, 2 * SEG_S, HW), f32)],
        compiler_params=_params("parallel", "parallel"),
        name="gdn_sample",
    )(qkvz, qkvz, qkvz, qkv_state, qkv_state, qkv_state, qkvz, ba, w_conv, w_conv, w_conv, onorm, s0)


CONV_LC = 128
CONV_RB = 64


def _ln_silu(c, g, b):
    mu = jnp.mean(c, axis=-1, keepdims=True)
    xc = c - mu
    y = xc * lax.rsqrt(jnp.mean(xc * xc, axis=-1, keepdims=True) + EPS) * g + b
    return _silu(y).astype(bf16)


CONV_PAD = 32


def _gates_conv_kernel(u_ref, wg_ref, x_ref, cw_ref, cb_ref, g_ref, c_ref, ext, *, tiles_per_seq):
    j, i = pl.program_id(0), pl.program_id(1)
    tm, cw = x_ref.shape

    @pl.when((i == 0) & (j == 0))
    def _():
        ext[:, tm:, :] = jnp.zeros((cw // CONV_LC, CONV_PAD, CONV_LC), f32)

    first = (i % tiles_per_seq) == 0
    off = CONV_PAD - (CONV_WIDTH - 1)
    nlc = cw // CONV_LC
    gw = g_ref.shape[1] // nlc
    for lc in range(nlc):
        gs = slice(lc * gw, (lc + 1) * gw)
        g_ref[:, gs] = _sigmoid(_dot_nt(u_ref[...], wg_ref[gs, :])).astype(g_ref.dtype)
        ls = slice(lc * CONV_LC, (lc + 1) * CONV_LC)
        ext[lc, 0:CONV_PAD, :] = jnp.where(first, 0.0, ext[lc, tm:tm + CONV_PAD, :])
        ext[lc, CONV_PAD:, :] = x_ref[:, ls]
        for r0 in range(0, tm, CONV_RB):
            acc = jnp.broadcast_to(cb_ref[:, ls], (CONV_RB, CONV_LC))
            for t in range(CONV_WIDTH):
                acc += cw_ref[t:t + 1, ls] * ext[lc, r0 + off + t:r0 + off + t + CONV_RB, :]
            c_ref[r0:r0 + CONV_RB, ls] = acc


def _gates_conv_prompt(u, w_tail, glu, cw, cb, nb, seq, tm=1024, tn=1024):
    k = u.shape[1]
    ng = 2 * D_MODEL // tn
    cwid = CONV_CH // ng
    j0 = (O_GATE - O_GLU) // tn
    return pl.pallas_call(
        functools.partial(_gates_conv_kernel, tiles_per_seq=seq // tm),
        grid=(ng, nb * seq // tm),
        in_specs=[pl.BlockSpec((tm, k), lambda j, i: (i, 0)),
                  pl.BlockSpec((tn, k), lambda j, i: (j + j0, 0)),
                  pl.BlockSpec((tm, cwid), lambda j, i: (i, j)),
                  pl.BlockSpec((CONV_WIDTH, cwid), lambda j, i: (0, j)),
                  pl.BlockSpec((1, cwid), lambda j, i: (0, j))],
        out_specs=[pl.BlockSpec((tm, tn), lambda j, i: (i, j)), pl.BlockSpec((tm, cwid), lambda j, i: (i, j))],
        out_shape=[jax.ShapeDtypeStruct((nb * seq, 2 * D_MODEL), bf16),
                   jax.ShapeDtypeStruct((nb * seq, CONV_CH), f32)],
        scratch_shapes=[pltpu.VMEM((cwid // CONV_LC, tm + CONV_PAD, CONV_LC), f32)],
        compiler_params=_params("arbitrary", "arbitrary"),
        name="gates_conv",
    )(u, w_tail, glu, cw, cb)


def _cconv_sample_kernel(x_ref, st_ref, w_ref, b_ref, g_ref, beta_ref, o_ref, e_scr, cbuf):
    nseq = CHUNK // SEG_S
    hist = CONV_WIDTH - 1
    for lc in range(CONV_CH // CONV_LC):
        ls = slice(lc * CONV_LC, (lc + 1) * CONV_LC)
        e_scr[lc, :, 0:hist, :] = st_ref[:, :, ls]
        e_scr[lc, :, hist:hist + SEG_S, :] = x_ref[:, ls].reshape(nseq, SEG_S, CONV_LC)
        acc = jnp.broadcast_to(b_ref[:, ls], (CHUNK, CONV_LC))
        for j in range(CONV_WIDTH):
            acc += w_ref[j:j + 1, ls] * e_scr[lc, :, j:j + SEG_S, :].reshape(CHUNK, CONV_LC)
        cbuf[:, ls] = acc
    o_ref[...] = _ln_silu(cbuf[...], g_ref[...], beta_ref[...])


def _cconv_sample(glu, glu_state, w, b, g, beta, row0):
    _, nb, hist, _ = glu_state.shape
    nseq = CHUNK // SEG_S
    rb0 = row0 // CHUNK
    vec = pl.BlockSpec((1, CONV_CH), lambda i: (0, 0))
    return pl.pallas_call(
        _cconv_sample_kernel,
        grid=(nb // nseq,),
        in_specs=[pl.BlockSpec((CHUNK, CONV_CH), lambda i: (rb0 + i, 0)),
                  pl.BlockSpec((None, nseq, hist, CONV_CH), lambda i: (0, i, 0, 0)),
                  pl.BlockSpec((CONV_WIDTH, CONV_CH), lambda i: (0, 0)), vec, vec, vec],
        out_specs=pl.BlockSpec((CHUNK, CONV_CH), lambda i: (i, 0)),
        out_shape=jax.ShapeDtypeStruct((nb * SEG_S, CONV_CH), bf16),
        scratch_shapes=[pltpu.VMEM((CONV_CH // CONV_LC, nseq, hist + SEG_S, CONV_LC), f32),
                        pltpu.VMEM((CHUNK, CONV_CH), f32)],
        compiler_params=_params("parallel"),
        name="cconv_sample",
    )(glu, glu_state, w, b, g, beta)


def _merge_kernel(oap_ref, oas_ref, cp_ref, cs_ref, gap_ref, gas_ref, gbp_ref, gbs_ref, h_ref,
                  wa_ref, wb_ref, wo_ref, lng_ref, lnb_ref, gpost_ref, o_ref, *, na):
    def run(oa_ref, c, ga_ref, gb_ref):
        merged = (ga_ref[...].astype(f32) * _dot(oa_ref[...], wa_ref[...])
                  + gb_ref[...].astype(f32) * _dot(c, wb_ref[...]))
        o_ref[...] = h_ref[...] + _rms(_dot(merged.astype(bf16), wo_ref[...]), gpost_ref[...])

    i = pl.program_id(0)
    pl.when(i < na)(lambda: run(oap_ref, _ln_silu(cp_ref[...], lng_ref[...], lnb_ref[...]), gap_ref, gbp_ref))
    pl.when(i >= na)(lambda: run(oas_ref, cs_ref[...], gas_ref, gbs_ref))


def _merge(oa_p, oa_s, c_p, c_s, gates_p, gates_s, h, wa, wb, wo, ln_g, ln_b, gpost, tm=256):
    t = h.shape[0]
    na = oa_p.shape[0] // tm
    act = pl.BlockSpec((tm, D_MODEL), lambda i: (i, 0))
    vec = pl.BlockSpec((1, D_MODEL), lambda i: (0, 0))
    src = lambda col: _split_specs((tm, D_MODEL), na, 2, col)
    wsp = pl.BlockSpec((D_MODEL, D_MODEL), lambda i: (0, 0), pipeline_mode=pl.Buffered(1))
    return pl.pallas_call(
        functools.partial(_merge_kernel, na=na),
        grid=(t // tm,),
        in_specs=src(0) + src(0) + src(0) + src(1) + [act, wsp, wsp, wsp, vec, vec, vec],
        out_specs=act,
        out_shape=jax.ShapeDtypeStruct((t, D_MODEL), f32),
        compiler_params=_params("parallel"),
        name="merge",
    )(oa_p, oa_s, c_p, c_s, gates_p, gates_s, gates_p, gates_s, h, wa, wb, wo, ln_g, ln_b, gpost)


def _ple_kernel(hn_ref, pa_ref, pb_ref, h_ref, wg_ref, wp_ref, gpost_ref, oa_ref, ob_ref, *, na):
    i = pl.program_id(0)
    gate = _sigmoid(_dot(hn_ref[...], wg_ref[...]))

    def out(p_ref):
        v = gate * _dot(p_ref[...].astype(bf16), wp_ref[...])
        return h_ref[...] + _rms(v, gpost_ref[...])

    @pl.when(i < na)
    def _():
        oa_ref[...] = out(pa_ref)

    @pl.when(i >= na)
    def _():
        ob_ref[...] = out(pb_ref)


def _ple(hn, pa, pb, h, wg, wp, gpost, tm=512):
    na, nb = pa.shape[0] // tm, pb.shape[0] // tm
    row = lambda i: (i, 0)
    first = lambda i: (jnp.minimum(i, na - 1), 0)
    second = lambda i: (jnp.maximum(i - na, 0), 0)
    act = pl.BlockSpec((tm, D_MODEL), row)
    return pl.pallas_call(
        functools.partial(_ple_kernel, na=na),
        grid=(na + nb,),
        in_specs=[act, pl.BlockSpec((tm, PLE_DIM), first), pl.BlockSpec((tm, PLE_DIM), second), act,
                  pl.BlockSpec((D_MODEL, D_MODEL), lambda i: (0, 0), pipeline_mode=pl.Buffered(1)),
                  pl.BlockSpec((PLE_DIM, D_MODEL), lambda i: (0, 0)),
                  pl.BlockSpec((1, D_MODEL), lambda i: (0, 0))],
        out_specs=[pl.BlockSpec((tm, D_MODEL), first), pl.BlockSpec((tm, D_MODEL), second)],
        out_shape=[jax.ShapeDtypeStruct((na * tm, D_MODEL), f32), jax.ShapeDtypeStruct((nb * tm, D_MODEL), f32)],
        compiler_params=_params("arbitrary"),
        name="ple",
    )(hn, pa, pb, h, wg, wp, gpost)


def _layer(xp, xs, pp, ps, nb, seq, s0, qkv_buf, glu_buf, w):
    (ffn1_pre, ffn1_w_gu, ffn1_w_down, ffn1_post, mix_pre, w_in, w_short_conv, a_log, dt_bias,
     o_norm, w_dw_conv, b_dw_conv, ln_g, ln_b, w_branch_a, w_branch_b, w_out, mix_post,
     ffn2_pre, ffn2_w_gu, ffn2_w_down, ffn2_post, ple_pre, w_ple_gate, w_ple_proj,
     ple_post) = w
    tp = nb * seq
    ns = s0.shape[1]
    row = lambda v: v.astype(f32).reshape(1, -1)
    cast = lambda v: v.astype(bf16)

    h1, u = _ffn([xp, xs], row(ffn1_pre), cast(ffn1_w_gu), cast(ffn1_w_down), row(ffn1_post), row(mix_pre))

    cast_t = lambda lo, hi: jnp.swapaxes(w_in[0, :, lo:hi], 0, 1).astype(bf16)
    w_qkvz, w_mid, w_tail = cast_t(0, O_BETA), cast_t(O_BETA, O_GLU), cast_t(O_GLU, w_in.shape[2])
    qkvz = _proj(u, w_qkvz, 0, O_BETA, "none", f32, tn=2048)
    glu, ba = _proj_glu_ba(u, w_tail, _ba_weight(w_mid.T), _group_lanes(a_log), _group_lanes(dt_bias))

    w_conv = w_short_conv.astype(f32)
    onorm = row(o_norm)
    oa_p, s_p = _gdn_prompt(qkvz, ba, w_conv, onorm, nb, seq)
    oa_s, s_s = _gdn_sample(qkvz, qkv_buf.astype(f32), ba, w_conv, onorm, s0.astype(f32), tp)

    cw, cb, lg, lb = w_dw_conv.astype(f32), row(b_dw_conv), row(ln_g), row(ln_b)
    gates_p, c_p = _gates_conv_prompt(u, w_tail, glu, cw, cb, nb, seq)
    gates_s = _proj(u, w_tail, O_GATE - O_GLU, 2 * D_MODEL, "sigmoid", bf16, r0=tp, rows=xs.shape[0])
    c_s = _cconv_sample(glu, glu_buf.astype(f32), cw, cb, lg, lb, tp)

    h2 = _merge(oa_p, oa_s, c_p, c_s, gates_p, gates_s, h1, cast(w_branch_a), cast(w_branch_b), cast(w_out),
                lg, lb, row(mix_post))
    h3, hn = _ffn([h2], row(ffn2_pre), cast(ffn2_w_gu), cast(ffn2_w_down), row(ffn2_post), row(ple_pre))
    yp, ys = _ple(hn, pp, ps, h3, cast(w_ple_gate), cast(w_ple_proj), row(ple_post))

    tail = lambda a, width, n: jnp.stack([a[(b + 1) * seq - n:(b + 1) * seq, :width] for b in range(nb)])
    qkv_p = tail(qkvz, QKV_DIM, SHORT_CONV - 1)
    glu_p = tail(glu, CONV_CH, CONV_WIDTH - 1)
    qkv_s = qkvz[tp:, :QKV_DIM].reshape(ns, SEG_S, QKV_DIM)[:, SEG_S - (SHORT_CONV - 1):]
    glu_s = jnp.concatenate([glu_buf[0].astype(f32)[:, SEG_S:], glu[tp:].reshape(ns, SEG_S, CONV_CH)], axis=1)
    return (yp, ys, s_p, qkv_p, glu_p, s_s, qkv_s, glu_s)


def kernel(x_prompt, x_sample, p_prompt, p_sample, state_delta, state_qkv_conv, state_glu_conv, ffn1_pre, ffn1_w_gu, ffn1_w_down, ffn1_post, mix_pre, w_in, w_short_conv, a_log, dt_bias, o_norm, w_dw_conv, b_dw_conv, ln_g, ln_b, w_branch_a, w_branch_b, w_out, mix_post, ffn2_pre, ffn2_w_gu, ffn2_w_down, ffn2_post, ple_pre, w_ple_gate, w_ple_proj, ple_post):
    weights = (ffn1_pre, ffn1_w_gu, ffn1_w_down, ffn1_post, mix_pre, w_in, w_short_conv, a_log,
               dt_bias, o_norm, w_dw_conv, b_dw_conv, ln_g, ln_b, w_branch_a, w_branch_b,
               w_out, mix_post, ffn2_pre, ffn2_w_gu, ffn2_w_down, ffn2_post, ple_pre,
               w_ple_gate, w_ple_proj, ple_post)
    nb, seq, _ = x_prompt.shape
    ns, ls, _ = x_sample.shape
    assert ls == SEG_S and seq % CHUNK == 0 and ns % (CHUNK // SEG_S) == 0
    depth = ffn1_pre.shape[0]
    tp = nb * seq
    xp, xs = x_prompt.reshape(tp, D_MODEL), x_sample.reshape(ns * ls, D_MODEL)
    outs = [[] for _ in range(6)]
    for i in range(depth):
        wi = tuple(wt[i:i + 1] if wt is w_in else wt[i] for wt in weights)
        xp, xs, s_p, q_p, g_p, s_s, q_s, g_s = _layer(
            xp, xs, p_prompt[i].reshape(tp, PLE_DIM), p_sample[i].reshape(ns * ls, PLE_DIM), nb, seq,
            state_delta[i:i + 1], state_qkv_conv[i:i + 1], state_glu_conv[i:i + 1], wi)
        for lst, v in zip(outs, (s_p, q_p, g_p, s_s, q_s, g_s)):
            lst.append(v)
    return (xp.reshape(nb, seq, D_MODEL), xs.reshape(ns, ls, D_MODEL)) + tuple(jnp.stack(lst) for lst in outs)
```

```python
import functools

import jax
import jax.numpy as jnp
from jax import lax
from jax.experimental import pallas as pl
from jax.experimental.pallas import tpu as pltpu

f32 = jnp.float32
bf16 = jnp.bfloat16

D_MODEL = 2048
N_HEADS = 16
HEAD_K = 128
HEAD_V = 128
KEY_DIM = N_HEADS * HEAD_K
VAL_DIM = N_HEADS * HEAD_V
QKV_DIM = 2 * KEY_DIM + VAL_DIM
SHORT_CONV = 4
CONV_CH = D_MODEL
CONV_WIDTH = 31
D_FF = 5632
PLE_DIM = 256
EPS = 1e-6

O_Z = QKV_DIM
O_BETA = O_Z + VAL_DIM
O_A = O_BETA + N_HEADS
O_GLU = O_A + N_HEADS
O_GATE = O_GLU + 2 * CONV_CH

CHUNK = 64
HG = 16
NHG = N_HEADS // HG
HW = HG * HEAD_K
SEG_S = 8
LOCAL_CHUNKS = 2

VMEM_LIMIT = 56 * 1024 * 1024


def _sigmoid(x):
    return 1.0 / (1.0 + jnp.exp(-x))


def _silu(x):
    return x * _sigmoid(x)


def _rms(x, g):
    return x * lax.rsqrt(jnp.mean(x * x, axis=-1, keepdims=True) + EPS) * g


def _dot(a, b):
    return jnp.dot(a, b, preferred_element_type=f32)


def _dot_nt(a, b):
    return lax.dot_general(a, b, (((1,), (1,)), ((), ())), preferred_element_type=f32)


def _params(*sem):
    return pltpu.CompilerParams(dimension_semantics=sem, vmem_limit_bytes=VMEM_LIMIT)


def _per_source(i, na, refs, fn):
    if len(refs) == 1:
        fn(refs[0])
    else:
        pl.when(i < na)(lambda: fn(refs[0]))
        pl.when(i >= na)(lambda: fn(refs[1]))


def _split_specs(block, na, nsrc, col=0):
    if nsrc == 1:
        return [pl.BlockSpec(block, lambda i, *_: (i, col))]
    return [pl.BlockSpec(block, lambda i, *_: (jnp.minimum(i, na - 1), col)),
            pl.BlockSpec(block, lambda i, *_: (jnp.maximum(i - na, 0), col))]


def _ffn_kernel(*refs, na, nsrc):
    x_refs = refs[:nsrc]
    gpre_ref, wg_ref, wu_ref, wd_ref, gpost_ref, gnext_ref, h_ref, hn_ref, xn_scr, acc_scr = refs[nsrc:]
    i, j = pl.program_id(0), pl.program_id(1)

    @pl.when(j == 0)
    def _():
        def norm_in(x_ref):
            xn_scr[...] = _rms(x_ref[...], gpre_ref[...]).astype(bf16)

        _per_source(i, na, x_refs, norm_in)
        acc_scr[...] = jnp.zeros_like(acc_scr)

    xn = xn_scr[...]
    a = _silu(_dot(xn, wg_ref[...])) * _dot(xn, wu_ref[...])
    acc_scr[...] += _dot(a.astype(bf16), wd_ref[...])

    @pl.when(j == pl.num_programs(1) - 1)
    def _():
        def residual_out(x_ref):
            h = x_ref[...] + 0.5 * _rms(acc_scr[...], gpost_ref[...])
            h_ref[...] = h
            hn_ref[...] = _rms(h, gnext_ref[...]).astype(bf16)

        _per_source(i, na, x_refs, residual_out)


def _ffn(xs, gpre, w_gu, w_down, gpost, gnext, tm=512, tf=512):
    na = xs[0].shape[0] // tm
    nt = sum(x.shape[0] for x in xs) // tm
    nf = D_FF // tf
    row = lambda i, j: (i, 0)
    vec = pl.BlockSpec((1, D_MODEL), lambda i, j: (0, 0))
    return pl.pallas_call(
        functools.partial(_ffn_kernel, na=na, nsrc=len(xs)),
        grid=(nt, nf),
        in_specs=_split_specs((tm, D_MODEL), na, len(xs))
        + [vec,
           pl.BlockSpec((D_MODEL, tf), lambda i, j: (0, j)),
           pl.BlockSpec((D_MODEL, tf), lambda i, j: (0, j + nf)),
           pl.BlockSpec((tf, D_MODEL), lambda i, j: (j, 0)), vec, vec],
        out_specs=[pl.BlockSpec((tm, D_MODEL), row), pl.BlockSpec((tm, D_MODEL), row)],
        out_shape=[jax.ShapeDtypeStruct((nt * tm, D_MODEL), f32), jax.ShapeDtypeStruct((nt * tm, D_MODEL), bf16)],
        scratch_shapes=[pltpu.VMEM((tm, D_MODEL), bf16), pltpu.VMEM((tm, D_MODEL), f32)],
        compiler_params=_params("parallel", "arbitrary"),
        name="ffn",
    )(*xs, gpre, w_gu, w_gu, w_down, gpost, gnext)


def _proj_kernel(x_ref, wt_ref, o_ref, *, act):
    y = _dot_nt(x_ref[...], wt_ref[...])
    if act == "sigmoid":
        y = _sigmoid(y)
    o_ref[...] = y.astype(o_ref.dtype)


def _proj(x, wt, n0, n, act, out_dtype, r0=0, rows=None, tm=1024, tn=1024):
    t, k = x.shape
    rows = t if rows is None else rows
    i0, j0 = r0 // tm, n0 // tn
    return pl.pallas_call(
        functools.partial(_proj_kernel, act=act),
        grid=(rows // tm, n // tn),
        in_specs=[pl.BlockSpec((tm, k), lambda i, j: (i + i0, 0)),
                  pl.BlockSpec((tn, k), lambda i, j: (j + j0, 0))],
        out_specs=pl.BlockSpec((tm, tn), lambda i, j: (i, j)),
        out_shape=jax.ShapeDtypeStruct((rows, n), out_dtype),
        compiler_params=_params("parallel", "arbitrary"),
        name="proj_" + act,
    )(x, wt)


def _glu_ba_kernel(x_ref, wa_ref, wb_ref, wba_ref, alog_ref, dtb_ref, o_ref, ba_ref):
    x = x_ref[...]
    o_ref[...] = _dot_nt(x, wa_ref[...]) * _sigmoid(_dot_nt(x, wb_ref[...]))

    @pl.when(pl.program_id(1) == 0)
    def _():
        y = _dot(x, wba_ref[...])
        lane = lax.broadcasted_iota(jnp.int32, y.shape, 1) % 128
        ya = y + dtb_ref[...]
        softplus = jnp.maximum(ya, 0.0) + jnp.log(1.0 + jnp.exp(-jnp.abs(ya)))
        ba_ref[...] = jnp.where(lane < HG, _sigmoid(y), -jnp.exp(alog_ref[...]) * softplus)


def _proj_glu_ba(x, wt, w_ba, alog_row, dtb_row, tm=1024, tn=1024):
    t, k = x.shape
    nj = CONV_CH // tn
    nba = NHG * 128
    vec = pl.BlockSpec((1, nba), lambda i, j: (0, 0))
    return pl.pallas_call(
        _glu_ba_kernel,
        grid=(t // tm, nj),
        in_specs=[pl.BlockSpec((tm, k), lambda i, j: (i, 0)),
                  pl.BlockSpec((tn, k), lambda i, j: (j, 0)),
                  pl.BlockSpec((tn, k), lambda i, j: (j + nj, 0)),
                  pl.BlockSpec((k, nba), lambda i, j: (0, 0)), vec, vec],
        out_specs=[pl.BlockSpec((tm, tn), lambda i, j: (i, j)), pl.BlockSpec((tm, nba), lambda i, j: (i, 0))],
        out_shape=[jax.ShapeDtypeStruct((t, CONV_CH), f32), jax.ShapeDtypeStruct((t, nba), f32)],
        compiler_params=_params("parallel", "arbitrary"),
        name="proj_glu_ba",
    )(x, wt, wt, w_ba, alog_row, dtb_row)


def _group_lanes(v):
    r = jnp.zeros((NHG, 128), f32).at[:, HG:2 * HG].set(v.astype(f32).reshape(NHG, HG))
    return r.reshape(1, NHG * 128)


def _ba_weight(w_ba):
    wb = w_ba[:, :N_HEADS].reshape(D_MODEL, NHG, HG)
    wa = w_ba[:, N_HEADS:].reshape(D_MODEL, NHG, HG)
    w = jnp.zeros((D_MODEL, NHG, 128), w_ba.dtype).at[:, :, :HG].set(wb).at[:, :, HG:2 * HG].set(wa)
    return w.reshape(D_MODEL, NHG * 128)


def _seq_masks(n, seg):
    r = lax.broadcasted_iota(jnp.int32, (n, n), 0)
    c = lax.broadcasted_iota(jnp.int32, (n, n), 1)
    same = (r // seg) == (c // seg)
    return same & (r >= c), same & (r > c), same


def _mask_dot(mask, x):
    l = jnp.where(mask, 1.0, 0.0).astype(bf16)
    hi = x.astype(bf16)
    r1 = x - hi.astype(f32)
    mid = r1.astype(bf16)
    lo = (r1 - mid.astype(f32)).astype(bf16)
    return _dot(l, hi) + _dot(l, mid) + _dot(l, lo)


def _mm(a, b):
    return _dot(a.astype(bf16), b.astype(bf16))


def _unit_lower_solve(ms, rhss, seg):
    n = range(len(ms))
    r = lax.broadcasted_iota(jnp.int32, (CHUNK, CHUNK), 0)
    c = lax.broadcasted_iota(jnp.int32, (CHUNK, CHUNK), 1)
    eye = (r == c).astype(f32)
    if seg <= 8:
        d, lo = ms, None
    else:
        blk = (r // 16) == (c // 16)
        d = [jnp.where(blk, m, 0.0) for m in ms]
        lo = [ms[i] - d[i] for i in n]
    d2 = [_mm(d[i], d[i]) for i in n]
    d3 = [_mm(d[i], d2[i]) for i in n]
    d4 = [_mm(d2[i], d2[i]) for i in n]
    a = [eye - d[i] + d2[i] - d3[i] for i in n]
    td = [a[i] + _mm(a[i], d4[i]) for i in n]
    if seg > 8:
        d8 = [_mm(d4[i], d4[i]) for i in n]
        td = [td[i] + _mm(td[i], d8[i]) for i in n]
    y = [_mm(td[i], rhss[i]) for i in n]
    if lo is None:
        return y
    e = [_mm(td[i], lo[i]) for i in n]
    e2 = [_mm(e[i], e[i]) for i in n]
    w = [y[i] - _mm(e[i], y[i]) for i in n]
    return [w[i] + _mm(e2[i], w[i]) for i in n]


def _chunk_local(qs, ks, vs, betas, gcols, grows, gtots, incl, strict, seg):
    n = range(len(qs))
    qn = [q * lax.rsqrt(jnp.sum(q * q, axis=-1, keepdims=True) + EPS) * (HEAD_K ** -0.5) for q in qs]
    kn = [k * lax.rsqrt(jnp.sum(k * k, axis=-1, keepdims=True) + EPS) for k in ks]
    decay = [jnp.exp(jnp.where(incl, gcols[i] - grows[i], -jnp.inf)) for i in n]
    kb = [k.astype(bf16) for k in kn]
    kk = [_dot_nt(kb[i], kb[i]) for i in n]
    qk = [_dot_nt(qn[i].astype(bf16), kb[i]) for i in n]
    m = [jnp.where(strict, betas[i] * decay[i] * kk[i], 0.0) for i in n]
    gamma = [jnp.exp(g) for g in gcols]
    rhs = [jnp.concatenate([betas[i] * vs[i], (betas[i] * gamma[i]) * kn[i]], axis=1) for i in n]
    sol = _unit_lower_solve(m, rhs, seg)
    uv = [s[:, :HEAD_V] for s in sol]
    wk = [s[:, HEAD_V:] for s in sol]
    qkd = [qk[i] * decay[i] for i in n]
    qg = [qn[i] * gamma[i] for i in n]
    kd = [kn[i] * jnp.exp(gtots[i] - gcols[i]) for i in n]
    return uv, wk, qkd, qg, kd


def _out_gate(o, onorm, z):
    return (_rms(o, onorm) * _silu(z)).astype(bf16)


def _gdn_prompt_kernel(q_ref, k_ref, v_ref, z_ref, ba_ref, wq_ref, wk_ref, wv_ref, onorm_ref,
                       oa_ref, s_ref,
                       eq, ek, ev, s_scr, qc_s, kc_s, vc_s, gc_s, gt_s, gl_s, uv_s, wq_s, kdt_s, qkd_s,
                       *, rows):
    t = pl.program_id(2)
    nc = rows // CHUNK
    heads = range(HG)
    hs = [slice(h * HEAD_K, (h + 1) * HEAD_K) for h in heads]
    gs = [slice(HG + h, HG + h + 1) for h in heads]

    @pl.when(t == 0)
    def _():
        s_scr[...] = jnp.zeros_like(s_scr)
        for e in (eq, ek, ev):
            e[:, 0:8, :] = jnp.zeros((HG, 8, HEAD_K), f32)

    off = 8 - (SHORT_CONV - 1)
    for e, x_ref, w_ref, dst in ((eq, q_ref, wq_ref, qc_s), (ek, k_ref, wk_ref, kc_s), (ev, v_ref, wv_ref, vc_s)):
        for h in heads:
            e[h, 8:, :] = x_ref[:, hs[h]]
            for r0 in range(0, rows, CHUNK):
                acc = w_ref[0:1, hs[h]] * e[h, r0 + off:r0 + off + CHUNK, :]
                for j in range(1, SHORT_CONV):
                    acc += w_ref[j:j + 1, hs[h]] * e[h, r0 + off + j:r0 + off + j + CHUNK, :]
                dst[r0:r0 + CHUNK, hs[h]] = _silu(acc)

    incl_b, _, same_b = _seq_masks(rows, CHUNK)
    ba = ba_ref[...]
    gc_s[...] = _mask_dot(incl_b, ba)
    gtot_b = _mask_dot(same_b, ba)
    gt_s[...] = gtot_b
    gl_s[...] = jnp.exp(gtot_b)
    incl, strict, _ = _seq_masks(CHUNK, CHUNK)

    for cp in range(nc // LOCAL_CHUNKS):
        cs = [cp * LOCAL_CHUNKS + i for i in range(LOCAL_CHUNKS)]
        rss = [slice(c * CHUNK, (c + 1) * CHUNK) for c in cs]
        units = [(i, h) for i in range(LOCAL_CHUNKS) for h in heads]
        bac = [ba_ref[rs, :] for rs in rss]
        gc = [gc_s[rs, :] for rs in rss]
        gtot = [gt_s[rs, :] for rs in rss]
        gct = [g.T for g in gc]
        uv, wk, qkd, qg, kd = _chunk_local(
            [qc_s[rss[i], hs[h]] for i, h in units], [kc_s[rss[i], hs[h]] for i, h in units],
            [vc_s[rss[i], hs[h]] for i, h in units], [bac[i][:, h:h + 1] for i, h in units],
            [gc[i][:, gs[h]] for i, h in units], [gct[i][gs[h], :] for i, h in units],
            [gtot[i][:, gs[h]] for i, h in units], incl, strict, CHUNK)
        for n, (i, h) in enumerate(units):
            uv_s[rss[i], hs[h]] = uv[n]
            wq_s[cs[i], 0:CHUNK, hs[h]] = wk[n].astype(bf16)
            wq_s[cs[i], CHUNK:2 * CHUNK, hs[h]] = qg[n].astype(bf16)
            kdt_s[cs[i], h] = kd[n].T.astype(bf16)
            qkd_s[cs[i], h] = qkd[n].astype(bf16)

    for c in range(nc):
        rs = slice(c * CHUNK, (c + 1) * CHUNK)
        s = [s_scr[h] for h in heads]
        p = [_dot(wq_s[c, :, hs[h]], s[h].astype(bf16)) for h in heads]
        u = [(uv_s[rs, hs[h]] - p[h][:CHUNK]).astype(bf16) for h in heads]
        o = [p[h][CHUNK:] + _dot(qkd_s[c, h], u[h]) for h in heads]
        ds = [_dot(kdt_s[c, h], u[h]) for h in heads]
        for h in heads:
            s_scr[h] = gl_s[c * CHUNK:c * CHUNK + 1, gs[h]] * s[h] + ds[h]
            oa_ref[rs, hs[h]] = _out_gate(o[h], onorm_ref[...], z_ref[rs, hs[h]])

    for e in (eq, ek, ev):
        e[:, 0:8, :] = e[:, rows:rows + 8, :]

    @pl.when(t == pl.num_programs(2) - 1)
    def _():
        s_ref[0] = s_scr[...]


def _gdn_prompt(qkvz, ba, w_conv, onorm, nb, seq, rows=256):
    nt = seq // rows
    nc = rows // CHUNK
    rb = lambda b, hg, t: b * nt + t
    col = lambda off: pl.BlockSpec((rows, HW), lambda b, hg, t: (rb(b, hg, t), off * NHG + hg))
    wcol = lambda off: pl.BlockSpec((SHORT_CONV, HW), lambda b, hg, t: (0, off * NHG + hg))
    return pl.pallas_call(
        functools.partial(_gdn_prompt_kernel, rows=rows),
        grid=(nb, NHG, nt),
        in_specs=[col(0), col(1), col(2), col(3),
                  pl.BlockSpec((rows, 128), lambda b, hg, t: (rb(b, hg, t), hg)),
                  wcol(0), wcol(1), wcol(2),
                  pl.BlockSpec((1, HEAD_V), lambda b, hg, t: (0, 0))],
        out_specs=[pl.BlockSpec((rows, HW), lambda b, hg, t: (rb(b, hg, t), hg)),
                   pl.BlockSpec((1, HG, HEAD_K, HEAD_V), lambda b, hg, t: (b, hg, 0, 0))],
        out_shape=[jax.ShapeDtypeStruct((nb * seq, VAL_DIM), bf16),
                   jax.ShapeDtypeStruct((nb, N_HEADS, HEAD_K, HEAD_V), f32)],
        scratch_shapes=[pltpu.VMEM((HG, rows + 8, HEAD_K), f32)] * 3
        + [pltpu.VMEM((HG, HEAD_K, HEAD_V), f32)]
        + [pltpu.VMEM((rows, HW), f32)] * 3
        + [pltpu.VMEM((rows, 128), f32)] * 3
        + [pltpu.VMEM((rows, HW), f32),
           pltpu.VMEM((nc, 2 * CHUNK, HW), bf16),
           pltpu.VMEM((nc, HG, HEAD_K, CHUNK), bf16),
           pltpu.VMEM((nc, HG, CHUNK, CHUNK), bf16)],
        compiler_params=_params("parallel", "parallel", "arbitrary"),
        name="gdn_prompt",
    )(qkvz, qkvz, qkvz, qkvz, ba, w_conv, w_conv, w_conv, onorm)


def _gdn_sample_kernel(q_ref, k_ref, v_ref, qst_ref, kst_ref, vst_ref, z_ref, ba_ref, wq_ref, wk_ref, wv_ref,
                       onorm_ref, s0_ref, oa_ref, s_ref, e_scr):
    nseq = CHUNK // SEG_S
    heads = range(HG)
    seqs = range(nseq)
    hs = [slice(h * HEAD_K, (h + 1) * HEAD_K) for h in heads]
    gs = [slice(HG + h, HG + h + 1) for h in heads]
    js = [slice(j * SEG_S, (j + 1) * SEG_S) for j in seqs]
    incl, strict, same = _seq_masks(CHUNK, SEG_S)
    hist = SHORT_CONV - 1

    def conv(x_ref, st_ref, w_ref):
        e_scr[:, SEG_S - hist:SEG_S, :] = st_ref[...]
        e_scr[:, SEG_S:, :] = x_ref[...].reshape(nseq, SEG_S, HW)
        off = SEG_S - hist
        acc = w_ref[0:1, :] * e_scr[:, off:off + SEG_S, :].reshape(CHUNK, HW)
        for j in range(1, SHORT_CONV):
            acc += w_ref[j:j + 1, :] * e_scr[:, off + j:off + j + SEG_S, :].reshape(CHUNK, HW)
        return _silu(acc)

    qc, kc, vc = conv(q_ref, qst_ref, wq_ref), conv(k_ref, kst_ref, wk_ref), conv(v_ref, vst_ref, wv_ref)
    bac = ba_ref[...]
    gc = _mask_dot(incl, bac)
    gtot = _mask_dot(same, bac)
    gct = gc.T
    glv = jnp.exp(gtot)
    uv, wk, qkd, qg, kd = _chunk_local(
        [qc[:, hs[h]] for h in heads], [kc[:, hs[h]] for h in heads], [vc[:, hs[h]] for h in heads],
        [bac[:, h:h + 1] for h in heads], [gc[:, gs[h]] for h in heads], [gct[gs[h], :] for h in heads],
        [gtot[:, gs[h]] for h in heads], incl, strict, SEG_S)
    kdt = [kd[h].T.astype(bf16) for h in heads]
    p = [[_dot(jnp.concatenate([wk[h][js[j]], qg[h][js[j]]], axis=0).astype(bf16),
               s0_ref[j, h].astype(bf16)) for j in seqs] for h in heads]
    u = [[uv[h][js[j]] - p[h][j][:SEG_S] for j in seqs] for h in heads]
    zeros = jnp.zeros((SEG_S, HEAD_V), f32)
    for h in heads:
        for j in seqs:
            u_rows = jnp.concatenate([u[h][j] if i == j else zeros for i in seqs], axis=0).astype(bf16)
            s_ref[j, h] = glv[j * SEG_S:j * SEG_S + 1, gs[h]] * s0_ref[j, h] + _dot(kdt[h], u_rows)
    for h in heads:
        u_all = jnp.concatenate(u[h], axis=0).astype(bf16)
        o = jnp.concatenate([p[h][j][SEG_S:] for j in seqs], axis=0) + _dot(qkd[h].astype(bf16), u_all)
        oa_ref[:, hs[h]] = _out_gate(o, onorm_ref[...], z_ref[:, hs[h]])


def _gdn_sample(qkvz, qkv_state, ba, w_conv, onorm, s0, row0):
    nb = qkv_state.shape[1]
    nseq = CHUNK // SEG_S
    rb0 = row0 // CHUNK
    col = lambda off: pl.BlockSpec((CHUNK, HW), lambda g, hg: (rb0 + g, off * NHG + hg))
    st = lambda off: pl.BlockSpec((None, nseq, SHORT_CONV - 1, HW), lambda g, hg: (0, g, 0, off * NHG + hg))
    wcol = lambda off: pl.BlockSpec((SHORT_CONV, HW), lambda g, hg: (0, off * NHG + hg))
    sspec = pl.BlockSpec((nseq, HG, HEAD_K, HEAD_V), lambda g, hg: (g, hg, 0, 0))
    return pl.pallas_call(
        _gdn_sample_kernel,
        grid=(nb // nseq, NHG),
        in_specs=[col(0), col(1), col(2), st(0), st(1), st(2), col(3),
                  pl.BlockSpec((CHUNK, 128), lambda g, hg: (rb0 + g, hg)),
                  wcol(0), wcol(1), wcol(2),
                  pl.BlockSpec((1, HEAD_V), lambda g, hg: (0, 0)),
                  pl.BlockSpec((None, nseq, HG, HEAD_K, HEAD_V), lambda g, hg: (0, g, hg, 0, 0))],
        out_specs=[pl.BlockSpec((CHUNK, HW), lambda g, hg: (g, hg)), sspec],
        out_shape=[jax.ShapeDtypeStruct((nb * SEG_S, VAL_DIM), bf16), jax.ShapeDtypeStruct(s0.shape[1:], f32)],
        scratch_shapes=[pltpu.VMEM((nseq, 2 * SEG_S, HW), f32)],
        compiler_params=_params("parallel", "parallel"),
        name="gdn_sample",
    )(qkvz, qkvz, qkvz, qkv_state, qkv_state, qkv_state, qkvz, ba, w_conv, w_conv, w_conv, onorm, s0)


CONV_LC = 128
CONV_RB = 64


def _ln_silu(c, g, b):
    mu = jnp.mean(c, axis=-1, keepdims=True)
    xc = c - mu
    y = xc * lax.rsqrt(jnp.mean(xc * xc, axis=-1, keepdims=True) + EPS) * g + b
    return _silu(y).astype(bf16)


CONV_PAD = 32


def _gates_conv_kernel(u_ref, wg_ref, x_ref, cw_ref, cb_ref, g_ref, c_ref, ext, *, tiles_per_seq):
    j, i = pl.program_id(0), pl.program_id(1)
    tm, cw = x_ref.shape

    @pl.when((i == 0) & (j == 0))
    def _():
        ext[:, tm:, :] = jnp.zeros((cw // CONV_LC, CONV_PAD, CONV_LC), f32)

    first = (i % tiles_per_seq) == 0
    off = CONV_PAD - (CONV_WIDTH - 1)
    nlc = cw // CONV_LC
    gw = g_ref.shape[1] // nlc
    for lc in range(nlc):
        gs = slice(lc * gw, (lc + 1) * gw)
        g_ref[:, gs] = _sigmoid(_dot_nt(u_ref[...], wg_ref[gs, :])).astype(g_ref.dtype)
        ls = slice(lc * CONV_LC, (lc + 1) * CONV_LC)
        ext[lc, 0:CONV_PAD, :] = jnp.where(first, 0.0, ext[lc, tm:tm + CONV_PAD, :])
        ext[lc, CONV_PAD:, :] = x_ref[:, ls]
        for r0 in range(0, tm, CONV_RB):
            acc = jnp.broadcast_to(cb_ref[:, ls], (CONV_RB, CONV_LC))
            for t in range(CONV_WIDTH):
                acc += cw_ref[t:t + 1, ls] * ext[lc, r0 + off + t:r0 + off + t + CONV_RB, :]
            c_ref[r0:r0 + CONV_RB, ls] = acc


def _gates_conv_prompt(u, w_tail, glu, cw, cb, nb, seq, tm=1024, tn=1024):
    k = u.shape[1]
    ng = 2 * D_MODEL // tn
    cwid = CONV_CH // ng
    j0 = (O_GATE - O_GLU) // tn
    return pl.pallas_call(
        functools.partial(_gates_conv_kernel, tiles_per_seq=seq // tm),
        grid=(ng, nb * seq // tm),
        in_specs=[pl.BlockSpec((tm, k), lambda j, i: (i, 0)),
                  pl.BlockSpec((tn, k), lambda j, i: (j + j0, 0)),
                  pl.BlockSpec((tm, cwid), lambda j, i: (i, j)),
                  pl.BlockSpec((CONV_WIDTH, cwid), lambda j, i: (0, j)),
                  pl.BlockSpec((1, cwid), lambda j, i: (0, j))],
        out_specs=[pl.BlockSpec((tm, tn), lambda j, i: (i, j)), pl.BlockSpec((tm, cwid), lambda j, i: (i, j))],
        out_shape=[jax.ShapeDtypeStruct((nb * seq, 2 * D_MODEL), bf16),
                   jax.ShapeDtypeStruct((nb * seq, CONV_CH), f32)],
        scratch_shapes=[pltpu.VMEM((cwid // CONV_LC, tm + CONV_PAD, CONV_LC), f32)],
        compiler_params=_params("arbitrary", "arbitrary"),
        name="gates_conv",
    )(u, w_tail, glu, cw, cb)


def _cconv_sample_kernel(x_ref, st_ref, w_ref, b_ref, g_ref, beta_ref, o_ref, e_scr, cbuf):
    nseq = CHUNK // SEG_S
    hist = CONV_WIDTH - 1
    for lc in range(CONV_CH // CONV_LC):
        ls = slice(lc * CONV_LC, (lc + 1) * CONV_LC)
        e_scr[lc, :, 0:hist, :] = st_ref[:, :, ls]
        e_scr[lc, :, hist:hist + SEG_S, :] = x_ref[:, ls].reshape(nseq, SEG_S, CONV_LC)
        acc = jnp.broadcast_to(b_ref[:, ls], (CHUNK, CONV_LC))
        for j in range(CONV_WIDTH):
            acc += w_ref[j:j + 1, ls] * e_scr[lc, :, j:j + SEG_S, :].reshape(CHUNK, CONV_LC)
        cbuf[:, ls] = acc
    o_ref[...] = _ln_silu(cbuf[...], g_ref[...], beta_ref[...])


def _cconv_sample(glu, glu_state, w, b, g, beta, row0):
    _, nb, hist, _ = glu_state.shape
    nseq = CHUNK // SEG_S
    rb0 = row0 // CHUNK
    vec = pl.BlockSpec((1, CONV_CH), lambda i: (0, 0))
    return pl.pallas_call(
        _cconv_sample_kernel,
        grid=(nb // nseq,),
        in_specs=[pl.BlockSpec((CHUNK, CONV_CH), lambda i: (rb0 + i, 0)),
                  pl.BlockSpec((None, nseq, hist, CONV_CH), lambda i: (0, i, 0, 0)),
                  pl.BlockSpec((CONV_WIDTH, CONV_CH), lambda i: (0, 0)), vec, vec, vec],
        out_specs=pl.BlockSpec((CHUNK, CONV_CH), lambda i: (i, 0)),
        out_shape=jax.ShapeDtypeStruct((nb * SEG_S, CONV_CH), bf16),
        scratch_shapes=[pltpu.VMEM((CONV_CH // CONV_LC, nseq, hist + SEG_S, CONV_LC), f32),
                        pltpu.VMEM((CHUNK, CONV_CH), f32)],
        compiler_params=_params("parallel"),
        name="cconv_sample",
    )(glu, glu_state, w, b, g, beta)


def _merge_kernel(oap_ref, oas_ref, cp_ref, cs_ref, gap_ref, gas_ref, gbp_ref, gbs_ref, h_ref,
                  wa_ref, wb_ref, wo_ref, lng_ref, lnb_ref, gpost_ref, o_ref, *, na):
    def run(oa_ref, c, ga_ref, gb_ref):
        merged = (ga_ref[...].astype(f32) * _dot(oa_ref[...], wa_ref[...])
                  + gb_ref[...].astype(f32) * _dot(c, wb_ref[...]))
        o_ref[...] = h_ref[...] + _rms(_dot(merged.astype(bf16), wo_ref[...]), gpost_ref[...])

    i = pl.program_id(0)
    pl.when(i < na)(lambda: run(oap_ref, _ln_silu(cp_ref[...], lng_ref[...], lnb_ref[...]), gap_ref, gbp_ref))
    pl.when(i >= na)(lambda: run(oas_ref, cs_ref[...], gas_ref, gbs_ref))


def _merge(oa_p, oa_s, c_p, c_s, gates_p, gates_s, h, wa, wb, wo, ln_g, ln_b, gpost, tm=256):
    t = h.shape[0]
    na = oa_p.shape[0] // tm
    act = pl.BlockSpec((tm, D_MODEL), lambda i: (i, 0))
    vec = pl.BlockSpec((1, D_MODEL), lambda i: (0, 0))
    src = lambda col: _split_specs((tm, D_MODEL), na, 2, col)
    wsp = pl.BlockSpec((D_MODEL, D_MODEL), lambda i: (0, 0), pipeline_mode=pl.Buffered(1))
    return pl.pallas_call(
        functools.partial(_merge_kernel, na=na),
        grid=(t // tm,),
        in_specs=src(0) + src(0) + src(0) + src(1) + [act, wsp, wsp, wsp, vec, vec, vec],
        out_specs=act,
        out_shape=jax.ShapeDtypeStruct((t, D_MODEL), f32),
        compiler_params=_params("parallel"),
        name="merge",
    )(oa_p, oa_s, c_p, c_s, gates_p, gates_s, gates_p, gates_s, h, wa, wb, wo, ln_g, ln_b, gpost)


def _ple_kernel(hn_ref, pa_ref, pb_ref, h_ref, wg_ref, wp_ref, gpost_ref, oa_ref, ob_ref, *, na):
    i = pl.program_id(0)
    gate = _sigmoid(_dot(hn_ref[...], wg_ref[...]))

    def out(p_ref):
        v = gate * _dot(p_ref[...].astype(bf16), wp_ref[...])
        return h_ref[...] + _rms(v, gpost_ref[...])

    @pl.when(i < na)
    def _():
        oa_ref[...] = out(pa_ref)

    @pl.when(i >= na)
    def _():
        ob_ref[...] = out(pb_ref)


def _ple(hn, pa, pb, h, wg, wp, gpost, tm=512):
    na, nb = pa.shape[0] // tm, pb.shape[0] // tm
    row = lambda i: (i, 0)
    first = lambda i: (jnp.minimum(i, na - 1), 0)
    second = lambda i: (jnp.maximum(i - na, 0), 0)
    act = pl.BlockSpec((tm, D_MODEL), row)
    return pl.pallas_call(
        functools.partial(_ple_kernel, na=na),
        grid=(na + nb,),
        in_specs=[act, pl.BlockSpec((tm, PLE_DIM), first), pl.BlockSpec((tm, PLE_DIM), second), act,
                  pl.BlockSpec((D_MODEL, D_MODEL), lambda i: (0, 0), pipeline_mode=pl.Buffered(1)),
                  pl.BlockSpec((PLE_DIM, D_MODEL), lambda i: (0, 0)),
                  pl.BlockSpec((1, D_MODEL), lambda i: (0, 0))],
        out_specs=[pl.BlockSpec((tm, D_MODEL), first), pl.BlockSpec((tm, D_MODEL), second)],
        out_shape=[jax.ShapeDtypeStruct((na * tm, D_MODEL), f32), jax.ShapeDtypeStruct((nb * tm, D_MODEL), f32)],
        compiler_params=_params("arbitrary"),
        name="ple",
    )(hn, pa, pb, h, wg, wp, gpost)


def _layer(xp, xs, pp, ps, nb, seq, s0, qkv_buf, glu_buf, w):
    (ffn1_pre, ffn1_w_gu, ffn1_w_down, ffn1_post, mix_pre, w_in, w_short_conv, a_log, dt_bias,
     o_norm, w_dw_conv, b_dw_conv, ln_g, ln_b, w_branch_a, w_branch_b, w_out, mix_post,
     ffn2_pre, ffn2_w_gu, ffn2_w_down, ffn2_post, ple_pre, w_ple_gate, w_ple_proj,
     ple_post) = w
    tp = nb * seq
    ns = s0.shape[1]
    row = lambda v: v.astype(f32).reshape(1, -1)
    cast = lambda v: v.astype(bf16)

    h1, u = _ffn([xp, xs], row(ffn1_pre), cast(ffn1_w_gu), cast(ffn1_w_down), row(ffn1_post), row(mix_pre))

    cast_t = lambda lo, hi: jnp.swapaxes(w_in[0, :, lo:hi], 0, 1).astype(bf16)
    w_qkvz, w_mid, w_tail = cast_t(0, O_BETA), cast_t(O_BETA, O_GLU), cast_t(O_GLU, w_in.shape[2])
    qkvz = _proj(u, w_qkvz, 0, O_BETA, "none", f32)
    glu, ba = _proj_glu_ba(u, w_tail, _ba_weight(w_mid.T), _group_lanes(a_log), _group_lanes(dt_bias))

    w_conv = w_short_conv.astype(f32)
    onorm = row(o_norm)
    oa_p, s_p = _gdn_prompt(qkvz, ba, w_conv, onorm, nb, seq)
    oa_s, s_s = _gdn_sample(qkvz, qkv_buf.astype(f32), ba, w_conv, onorm, s0.astype(f32), tp)

    cw, cb, lg, lb = w_dw_conv.astype(f32), row(b_dw_conv), row(ln_g), row(ln_b)
    gates_p, c_p = _gates_conv_prompt(u, w_tail, glu, cw, cb, nb, seq)
    gates_s = _proj(u, w_tail, O_GATE - O_GLU, 2 * D_MODEL, "sigmoid", bf16, r0=tp, rows=xs.shape[0])
    c_s = _cconv_sample(glu, glu_buf.astype(f32), cw, cb, lg, lb, tp)

    h2 = _merge(oa_p, oa_s, c_p, c_s, gates_p, gates_s, h1, cast(w_branch_a), cast(w_branch_b), cast(w_out),
                lg, lb, row(mix_post))
    h3, hn = _ffn([h2], row(ffn2_pre), cast(ffn2_w_gu), cast(ffn2_w_down), row(ffn2_post), row(ple_pre))
    yp, ys = _ple(hn, pp, ps, h3, cast(w_ple_gate), cast(w_ple_proj), row(ple_post))

    tail = lambda a, width, n: jnp.stack([a[(b + 1) * seq - n:(b + 1) * seq, :width] for b in range(nb)])
    qkv_p = tail(qkvz, QKV_DIM, SHORT_CONV - 1)
    glu_p = tail(glu, CONV_CH, CONV_WIDTH - 1)
    qkv_s = qkvz[tp:, :QKV_DIM].reshape(ns, SEG_S, QKV_DIM)[:, SEG_S - (SHORT_CONV - 1):]
    glu_s = jnp.concatenate([glu_buf[0].astype(f32)[:, SEG_S:], glu[tp:].reshape(ns, SEG_S, CONV_CH)], axis=1)
    return (yp, ys, s_p, qkv_p, glu_p, s_s, qkv_s, glu_s)


def kernel(x_prompt, x_sample, p_prompt, p_sample, state_delta, state_qkv_conv, state_glu_conv, ffn1_pre, ffn1_w_gu, ffn1_w_down, ffn1_post, mix_pre, w_in, w_short_conv, a_log, dt_bias, o_norm, w_dw_conv, b_dw_conv, ln_g, ln_b, w_branch_a, w_branch_b, w_out, mix_post, ffn2_pre, ffn2_w_gu, ffn2_w_down, ffn2_post, ple_pre, w_ple_gate, w_ple_proj, ple_post):
    weights = (ffn1_pre, ffn1_w_gu, ffn1_w_down, ffn1_post, mix_pre, w_in, w_short_conv, a_log,
               dt_bias, o_norm, w_dw_conv, b_dw_conv, ln_g, ln_b, w_branch_a, w_branch_b,
               w_out, mix_post, ffn2_pre, ffn2_w_gu, ffn2_w_down, ffn2_post, ple_pre,
               w_ple_gate, w_ple_proj, ple_post)
    nb, seq, _ = x_prompt.shape
    ns, ls, _ = x_sample.shape
    assert ls == SEG_S and seq % CHUNK == 0 and ns % (CHUNK // SEG_S) == 0
    depth = ffn1_pre.shape[0]
    tp = nb * seq
    xp, xs = x_prompt.reshape(tp, D_MODEL), x_sample.reshape(ns * ls, D_MODEL)
    outs = [[] for _ in range(6)]
    for i in range(depth):
        wi = tuple(wt[i:i + 1] if wt is w_in else wt[i] for wt in weights)
        xp, xs, s_p, q_p, g_p, s_s, q_s, g_s = _layer(
            xp, xs, p_prompt[i].reshape(tp, PLE_DIM), p_sample[i].reshape(ns * ls, PLE_DIM), nb, seq,
            state_delta[i:i + 1], state_qkv_conv[i:i + 1], state_glu_conv[i:i + 1], wi)
        for lst, v in zip(outs, (s_p, q_p, g_p, s_s, q_s, g_s)):
            lst.append(v)
    return (xp.reshape(nb, seq, D_MODEL), xs.reshape(ns, ls, D_MODEL)) + tuple(jnp.stack(lst) for lst in outs)
```
